```python
import jax, jax.numpy as jnp
from jax import lax
import numpy as np

D_MODEL = 1024
BATCH = 8
SEQ = 4096
DEPTH = 1

N_META = 16
D_MIX = D_MODEL
D_ATTN = D_MIX // 2
D_CONV = D_MIX - D_ATTN
N_HEADS = 8
HEAD_DIM = D_ATTN // N_HEADS
CONV_WIDTH = 31
Q_BLOCK = 128
LN_EPS = 1e-5
ALPHA = (2.0 * DEPTH) ** 0.25
BETA = (8.0 * DEPTH) ** -0.25

OFF_K = D_ATTN
OFF_V = 2 * D_ATTN
OFF_F = 3 * D_ATTN
OFF_GA = OFF_F + N_HEADS
OFF_U = OFF_GA + D_ATTN
OFF_UG = OFF_U + D_CONV
OFF_GC = OFF_UG + D_CONV
D_IN = OFF_GC + D_CONV
SPLITS = (OFF_K, OFF_V, OFF_F, OFF_GA, OFF_U, OFF_UG, OFF_GC)

kernel_name = "fox_conformer_hymba_deepnorm"


def layer_norm(x, g, b):
    xf = x.astype(jnp.float32)
    mu = jnp.mean(xf, axis=-1, keepdims=True)
    var = jnp.mean(jnp.square(xf - mu), axis=-1, keepdims=True)
    return ((xf - mu) * lax.rsqrt(var + LN_EPS) * g + b).astype(x.dtype)


def forgetting_attention(q, k, v, log_f):
    L = q.shape[1]
    scale = HEAD_DIM ** -0.5
    c = jnp.cumsum(log_f, axis=1).transpose(0, 2, 1)
    starts = [0] + list(range(N_META, L, Q_BLOCK))
    ends = starts[1:] + [L]
    outs = []
    for qs, qe in zip(starts, ends):
        qb, kb, vb = q[:, qs:qe], k[:, :qe], v[:, :qe]
        s = jnp.einsum('bqhd,bkhd->bhqk', qb, kb, preferred_element_type=jnp.float32) * scale
        bias = c[:, :, qs:qe, None] - c[:, :, None, :qe]
        qpos = jnp.arange(qs, qe)[:, None]
        kpos = jnp.arange(qe)[None, :]
        s = jnp.where(kpos <= qpos, s + bias, -jnp.inf)
        p = jax.nn.softmax(s, axis=-1)
        outs.append(jnp.einsum('bhqk,bkhd->bqhd', p.astype(vb.dtype), vb))
    return jnp.concatenate(outs, axis=1)


def conformer_conv(u, u_gate, conv_w, conv_b, g, b, w_pw):
    h = u * jax.nn.sigmoid(u_gate)
    h = lax.conv_general_dilated(
        h, conv_w[:, None, :].astype(h.dtype), window_strides=(1,),
        padding=[(CONV_WIDTH - 1, 0)], dimension_numbers=('NWC', 'WIO', 'NWC'),
        feature_group_count=D_CONV) + conv_b
    h = jax.nn.silu(layer_norm(h, g, b))
    return h @ w_pw


def hybrid_layer(h, w_in, b_f, conv_w, conv_b, ln_conv_g, ln_conv_b, w_pw, w_out, ln_out_g, ln_out_b):
    B, L, _ = h.shape
    proj = h @ w_in
    q, k, v, f_logit, g_attn, u, u_gate, g_conv = jnp.split(proj, SPLITS, axis=-1)
    q = q.reshape(B, L, N_HEADS, HEAD_DIM)
    k = k.reshape(B, L, N_HEADS, HEAD_DIM)
    v = v.reshape(B, L, N_HEADS, HEAD_DIM)
    log_f = jax.nn.log_sigmoid((f_logit + b_f).astype(jnp.float32))
    y_attn = forgetting_attention(q, k, v, log_f).reshape(B, L, D_ATTN) * jax.nn.silu(g_attn)
    y_conv = conformer_conv(u, u_gate, conv_w, conv_b, ln_conv_g, ln_conv_b, w_pw) * jax.nn.silu(g_conv)
    y = jnp.concatenate([y_attn, y_conv], axis=-1) @ w_out
    return layer_norm(ALPHA * h + y, ln_out_g, ln_out_b)


def setup_inputs(seed: int = 0) -> dict:
    key = jax.random.key(seed)
    ks = jax.random.split(key, 16)
    f32 = jnp.float32
    x = jax.random.normal(ks[0], (BATCH, SEQ, D_MODEL), f32)
    meta = jax.random.normal(ks[1], (N_META, D_MODEL), f32)
    ln_in_g = 1.0 + 0.02 * jax.random.normal(ks[2], (D_MODEL,), f32)
    ln_in_b = 0.02 * jax.random.normal(ks[3], (D_MODEL,), f32)
    col_scale = jnp.ones((D_IN,), f32).at[OFF_V:OFF_F].set(BETA)
    w_in = jax.random.normal(ks[4], (DEPTH, D_MODEL, D_IN), f32) * (D_MODEL ** -0.5) * col_scale
    b_f = jnp.linspace(1.0, 6.0, N_HEADS, dtype=f32)[None, :] + 0.1 * jax.random.normal(ks[5], (DEPTH, N_HEADS), f32)
    conv_w = jax.random.normal(ks[6], (DEPTH, CONV_WIDTH, D_CONV), f32) * (CONV_WIDTH ** -0.5)
    conv_b = 0.02 * jax.random.normal(ks[7], (DEPTH, D_CONV), f32)
    ln_conv_g = 1.0 + 0.02 * jax.random.normal(ks[8], (DEPTH, D_CONV), f32)
    ln_conv_b = 0.02 * jax.random.normal(ks[9], (DEPTH, D_CONV), f32)
    w_pw = jax.random.normal(ks[10], (DEPTH, D_CONV, D_CONV), f32) * (D_CONV ** -0.5) * BETA
    w_out = jax.random.normal(ks[11], (DEPTH, D_MIX, D_MODEL), f32) * (D_MIX ** -0.5) * BETA
    ln_out_g = 1.0 + 0.02 * jax.random.normal(ks[12], (DEPTH, D_MODEL), f32)
    ln_out_b = 0.02 * jax.random.normal(ks[13], (DEPTH, D_MODEL), f32)
    return {"x": x, "meta": meta, "ln_in_g": ln_in_g, "ln_in_b": ln_in_b, "w_in": w_in,
            "b_f": b_f, "conv_w": conv_w, "conv_b": conv_b, "ln_conv_g": ln_conv_g,
            "ln_conv_b": ln_conv_b, "w_pw": w_pw, "w_out": w_out,
            "ln_out_g": ln_out_g, "ln_out_b": ln_out_b}


def reference(x, meta, ln_in_g, ln_in_b, w_in, b_f, conv_w, conv_b, ln_conv_g, ln_conv_b,
              w_pw, w_out, ln_out_g, ln_out_b):
    B = x.shape[0]
    meta_b = jnp.broadcast_to(meta[None].astype(x.dtype), (B, N_META, D_MODEL))
    h = jnp.concatenate([meta_b, x], axis=1)
    h = layer_norm(h, ln_in_g, ln_in_b)
    for l in range(DEPTH):
        h = hybrid_layer(h, w_in[l], b_f[l], conv_w[l], conv_b[l], ln_conv_g[l], ln_conv_b[l],
                         w_pw[l], w_out[l], ln_out_g[l], ln_out_b[l])
    return h[:, N_META:]
```

```python
import functools

import jax
import jax.numpy as jnp
from jax import lax
from jax.experimental import pallas as pl
from jax.experimental.pallas import tpu as pltpu

F32 = jnp.float32
BF16 = jnp.bfloat16

D_MODEL = 1024
N_META = 16
D_ATTN = 512
D_CONV = 512
N_HEADS = 8
HEAD_DIM = 64
CONV_WIDTH = 31
LN_EPS = 1e-5
ALPHA = 2.0 ** 0.25
LANES = 128
SUBLANES = 8
META_PAD = 128
HALO = 32
MASK_BIAS = 1e30

C_Q = 0
C_K = N_HEADS * LANES
C_V = 2 * N_HEADS * LANES
C_F = 3 * N_HEADS * LANES
C_GA = C_F + LANES
C_U = C_GA + D_ATTN
C_UG = C_U + D_CONV
C_GC = C_UG + D_CONV
C_END = C_GC + D_CONV

TM_PROJ = 512
TQ = 256
TK = 256
TM_OUT = 512
CONV_ROWS = 32


def _dot(a, b):
    return jnp.dot(a, b, preferred_element_type=F32)


def _dot_nt(a, b):
    return lax.dot_general(a, b, (((1,), (1,)), ((), ())), preferred_element_type=F32)


def _layer_norm(x, g, b):
    mu = jnp.mean(x, axis=-1, keepdims=True)
    xc = x - mu
    var = jnp.mean(xc * xc, axis=-1, keepdims=True)
    return xc * lax.rsqrt(var + LN_EPS) * g + b


def _sigmoid(x):
    return 1.0 / (1.0 + jnp.exp(-x))


def _silu(x):
    return x * _sigmoid(x)


def _log_sigmoid(x):
    return jnp.minimum(x, 0.0) - jnp.log1p(jnp.exp(-jnp.abs(x)))


def _split3(c):
    hi = c.astype(BF16).astype(F32)
    r = c - hi
    mid = r.astype(BF16).astype(F32)
    lo = (r - mid).astype(BF16).astype(F32)
    return hi, mid, lo


def _bias_operand(c, lane):
    hi, mid, lo = _split3(c)
    one = jnp.where(lane < 4 * N_HEADS, 1.0, 0.0)
    return jnp.where(lane < N_HEADS, hi,
                     jnp.where(lane < 2 * N_HEADS, mid,
                               jnp.where(lane < 3 * N_HEADS, lo, one)))


def _forget_logs(hb, w_ref, bf_ref, lane):
    f = _dot(hb, w_ref[:, C_F:C_F + LANES])
    return jnp.where(lane < 3 * N_HEADS, _log_sigmoid(f + bf_ref[...]), 0.0)


def _exact_tri_dot(tri, x):
    hi, mid, lo = _split3(x)
    return _dot(tri, hi.astype(BF16)) + _dot(tri, mid.astype(BF16)) + _dot(tri, lo.astype(BF16))


def _meta_body(x_ref, lng_ref, lnb_ref, w_ref, bf_ref, pm_ref, vone_ref,
               k_ref, v_ref, glu_ref):
    rows = x_ref.shape[0]
    hb = _layer_norm(x_ref[...], lng_ref[...], lnb_ref[...]).astype(BF16)
    lane = lax.broadcasted_iota(jnp.int32, (rows, LANES), 1)
    row = lax.broadcasted_iota(jnp.int32, (rows, LANES), 0)
    lf = _forget_logs(hb, w_ref, bf_ref, lane)
    r = lax.broadcasted_iota(jnp.int32, (rows, rows), 0)
    c = lax.broadcasted_iota(jnp.int32, (rows, rows), 1)
    upper = jnp.where((c > r) & (c < N_META), 1.0, 0.0).astype(BF16)
    cm = -_exact_tri_dot(upper, lf)
    cmat = _bias_operand(cm, lane)
    masked = jnp.where(lane < N_HEADS, MASK_BIAS, 0.0)
    cmat = jnp.where(row < N_META, cmat, masked).astype(BF16)
    kk = _dot(hb, w_ref[:, C_K:C_V]) + _dot(cmat, pm_ref[:, C_K:C_V])
    vv = _dot(hb, w_ref[:, C_V:C_F]) + vone_ref[...]
    for h in range(N_HEADS):
        k_ref[h] = kk[:, h * LANES:(h + 1) * LANES].astype(BF16)
        v_ref[h] = vv[:, h * LANES:(h + 1) * LANES].astype(BF16)
    u = _dot(hb, w_ref[:, C_U:C_UG])
    ug = _dot(hb, w_ref[:, C_UG:C_GC])
    glu = u * _sigmoid(ug)
    glu_ref[0:HALO - N_META, :] = jnp.zeros((HALO - N_META, D_CONV), F32)
    glu_ref[HALO - N_META:HALO, :] = glu[0:N_META, :]


def _proj_body(x_ref, lng_ref, lnb_ref, w_ref, bf_ref, pm_ref, vone_ref,
               q_ref, k_ref, v_ref, ga_ref, glu_ref, gc_ref,
               carry_ref, tri_ref):
    tm = x_ref.shape[1]

    @pl.when(pl.program_id(1) == 0)
    def _start_sequence():
        carry_ref[...] = jnp.zeros_like(carry_ref)
        r = lax.broadcasted_iota(jnp.int32, (tm, tm), 0)
        c = lax.broadcasted_iota(jnp.int32, (tm, tm), 1)
        tri_ref[...] = jnp.where(r >= c, 1.0, 0.0).astype(BF16)

    hb = _layer_norm(x_ref[0], lng_ref[...], lnb_ref[...]).astype(BF16)
    lane = lax.broadcasted_iota(jnp.int32, (tm, LANES), 1)
    lf = _forget_logs(hb, w_ref, bf_ref, lane)
    c = _exact_tri_dot(tri_ref[...], lf) + carry_ref[...]
    carry_ref[...] = c[tm - 1:tm, :]
    cmat = _bias_operand(c, lane).astype(BF16)

    qk = _dot(hb, w_ref[:, C_Q:C_V]) + _dot(cmat, pm_ref[...])
    for h in range(N_HEADS):
        q_ref[0, h] = qk[:, C_Q + h * LANES:C_Q + (h + 1) * LANES].astype(BF16)
        k_ref[0, h] = qk[:, C_K + h * LANES:C_K + (h + 1) * LANES].astype(BF16)
    vv = _dot(hb, w_ref[:, C_V:C_F]) + vone_ref[...]
    for h in range(N_HEADS):
        v_ref[0, h] = vv[:, h * LANES:(h + 1) * LANES].astype(BF16)
    ga_ref[0] = _silu(_dot(hb, w_ref[:, C_GA:C_U]))
    u = _dot(hb, w_ref[:, C_U:C_UG])
    ug = _dot(hb, w_ref[:, C_UG:C_GC])
    glu_ref[0] = u * _sigmoid(ug)
    gc_ref[0] = _silu(_dot(hb, w_ref[:, C_GC:C_END]))


def _attn_body(q_ref, k_ref, v_ref, km_ref, vm_ref, o_ref):
    qi = pl.program_id(2)
    tq = q_ref.shape[2]
    row = lax.broadcasted_iota(jnp.int32, (tq, TK), 0)
    col = lax.broadcasted_iota(jnp.int32, (tq, TK), 1)
    causal = col <= row
    halves = []
    for hh in range(2):
        q = q_ref[0, hh]

        s = _dot_nt(q, km_ref[hh])
        m = jnp.max(s, axis=1, keepdims=True)
        p = jnp.exp(s - m)
        acc = _dot(p.astype(BF16), vm_ref[hh])

        def block(j, m, acc, diagonal, hh=hh, q=q):
            off = pl.multiple_of(j * TK, TK)
            s = _dot_nt(q, k_ref[0, hh, pl.ds(off, TK), :])
            if diagonal:
                s = jnp.where(causal, s, -MASK_BIAS)
            m_new = jnp.maximum(m, jnp.max(s, axis=1, keepdims=True))
            p = jnp.exp(s - m_new)
            acc = acc * jnp.exp(m - m_new) + _dot(p.astype(BF16), v_ref[0, hh, pl.ds(off, TK), :])
            return m_new, acc

        m, acc = lax.fori_loop(0, qi, lambda j, c: block(j, c[0], c[1], False), (m, acc))
        m, acc = block(qi, m, acc, True)
        halves.append(acc / acc[:, HEAD_DIM:HEAD_DIM + 1])
    lane = lax.broadcasted_iota(jnp.int32, (tq, LANES), 1)
    o_ref[0] = jnp.where(lane < HEAD_DIM, halves[0], pltpu.roll(halves[1], HEAD_DIM, axis=1))


def _mix_body(x_ref, glu_ref, halo_ref, meta_ref, gc_ref, att_ref, ga_ref,
              cw_ref, cb_ref, lcg_ref, lcb_ref, wpw_ref, wout_ref,
              lig_ref, lib_ref, log_ref, lob_ref,
              o_ref, buf_ref, conv_ref):
    tm = glu_ref.shape[1]
    first = pl.program_id(1) == 0

    @pl.when(first)
    def _meta_halo():
        buf_ref[0:HALO, :] = meta_ref[...]

    @pl.when(jnp.logical_not(first))
    def _tile_halo():
        buf_ref[0:HALO, :] = halo_ref[0]

    buf_ref[HALO:HALO + tm, :] = glu_ref[0]

    shift = HALO - (CONV_WIDTH - 1)
    for chunk in range(tm // CONV_ROWS):
        base = chunk * CONV_ROWS
        acc = jnp.broadcast_to(cb_ref[...], (CONV_ROWS, D_CONV))
        for r in range(SUBLANES):
            taps = [j for j in range(CONV_WIDTH) if (shift + j) % SUBLANES == r]
            lo = base + shift + taps[0]
            win = buf_ref[lo:lo + (taps[-1] - taps[0]) + CONV_ROWS, :]
            for j in taps:
                d = j - taps[0]
                acc = acc + cw_ref[j:j + 1, :] * win[d:d + CONV_ROWS, :]
        conv_ref[base:base + CONV_ROWS, :] = acc

    z = _silu(_layer_norm(conv_ref[...], lcg_ref[...], lcb_ref[...]))
    y_conv = _dot(z.astype(BF16), wpw_ref[...]) * gc_ref[0]
    y_attn = att_ref[0] * ga_ref[0]
    y = _dot(y_attn.astype(BF16), wout_ref[0:D_ATTN, :]) + _dot(y_conv.astype(BF16), wout_ref[D_ATTN:, :])
    h = _layer_norm(x_ref[0], lig_ref[...], lib_ref[...])
    o_ref[0] = _layer_norm(ALPHA * h + y, log_ref[...], lob_ref[...])


def _pad_heads(w):
    w = w.reshape(D_MODEL, N_HEADS, HEAD_DIM)
    w = jnp.pad(w, ((0, 0), (0, 0), (0, LANES - HEAD_DIM)))
    return w.reshape(D_MODEL, N_HEADS * LANES)


def _placement():
    import numpy as np
    pm = np.zeros((LANES, 2 * N_HEADS * LANES), np.float32)
    for h in range(N_HEADS):
        for g in range(3):
            pm[g * N_HEADS + h, C_Q + h * LANES + HEAD_DIM + g] = 1.0
            pm[3 * N_HEADS + h, C_Q + h * LANES + HEAD_DIM + 3 + g] = 1.0
            pm[3 * N_HEADS + h, C_K + h * LANES + HEAD_DIM + g] = 1.0
            pm[g * N_HEADS + h, C_K + h * LANES + HEAD_DIM + 3 + g] = -1.0
    return jnp.asarray(pm, BF16)


def _full(shape):
    return pl.BlockSpec(shape, lambda *_: (0,) * len(shape))


def _params(sem, vmem_mb):
    return pltpu.CompilerParams(dimension_semantics=sem, vmem_limit_bytes=vmem_mb * 1024 * 1024)


def kernel(x, meta, ln_in_g, ln_in_b, w_in, b_f, conv_w, conv_b, ln_conv_g, ln_conv_b,
           w_pw, w_out, ln_out_g, ln_out_b):
    B, S, D = x.shape
    assert D == D_MODEL and w_in.shape[0] == 1 and meta.shape == (N_META, D_MODEL)
    assert S % TM_PROJ == 0 and S % TM_OUT == 0 and S % TQ == 0 and TQ == TK

    w = w_in[0]
    scale = HEAD_DIM ** -0.5
    wf = jnp.pad(jnp.tile(w[:, 3 * D_ATTN:3 * D_ATTN + N_HEADS], (1, 3)), ((0, 0), (0, LANES - 3 * N_HEADS)))
    w_all = jnp.concatenate([
        _pad_heads(w[:, 0:D_ATTN] * scale), _pad_heads(w[:, D_ATTN:2 * D_ATTN]),
        _pad_heads(w[:, 2 * D_ATTN:3 * D_ATTN]), wf, w[:, 3 * D_ATTN + N_HEADS:]], axis=1).astype(BF16)
    bf_row = jnp.pad(jnp.tile(b_f[0], 3), (0, LANES - 3 * N_HEADS)).reshape(1, LANES)
    pm = _placement()
    vone = jnp.zeros((N_HEADS, LANES), F32).at[:, HEAD_DIM].set(1.0).reshape(1, N_HEADS * LANES)
    lig = ln_in_g.reshape(1, D)
    lib = ln_in_b.reshape(1, D)
    meta_pad = jnp.pad(meta, ((0, META_PAD - N_META), (0, 0)))

    k_meta, v_meta, glu_meta = pl.pallas_call(
        _meta_body,
        out_shape=(jax.ShapeDtypeStruct((N_HEADS, META_PAD, LANES), BF16),
                   jax.ShapeDtypeStruct((N_HEADS, META_PAD, LANES), BF16),
                   jax.ShapeDtypeStruct((HALO, D_CONV), F32)),
        compiler_params=pltpu.CompilerParams(vmem_limit_bytes=48 * 1024 * 1024),
        name="meta_proj",
    )(meta_pad, lig, lib, w_all, bf_row, pm, vone)

    nt = S // TM_PROJ
    head_blk = pl.BlockSpec((1, N_HEADS, TM_PROJ, LANES), lambda b, i: (b, 0, i, 0))
    row_blk = pl.BlockSpec((1, TM_PROJ, D_ATTN), lambda b, i: (b, i, 0))
    qp, kp, vp, ga, glu, gc = pl.pallas_call(
        _proj_body,
        grid=(B, nt),
        in_specs=[pl.BlockSpec((1, TM_PROJ, D), lambda b, i: (b, i, 0)),
                  _full((1, D)), _full((1, D)), _full((D, C_END)), _full((1, LANES)),
                  _full((LANES, 2 * N_HEADS * LANES)), _full((1, N_HEADS * LANES))],
        out_specs=(head_blk, head_blk, head_blk, row_blk, row_blk, row_blk),
        out_shape=(jax.ShapeDtypeStruct((B, N_HEADS, S, LANES), BF16),) * 3
                  + (jax.ShapeDtypeStruct((B, S, D_ATTN), F32),) * 3,
        scratch_shapes=[pltpu.VMEM((1, LANES), F32), pltpu.VMEM((TM_PROJ, TM_PROJ), BF16)],
        compiler_params=_params(("arbitrary", "arbitrary"), 56),
        name="in_proj",
    )(x, lig, lib, w_all, bf_row, pm, vone)

    att = pl.pallas_call(
        _attn_body,
        grid=(B, N_HEADS // 2, S // TQ),
        in_specs=[pl.BlockSpec((1, 2, TQ, LANES), lambda b, h, i: (b, h, i, 0)),
                  pl.BlockSpec((1, 2, S, LANES), lambda b, h, i: (b, h, 0, 0)),
                  pl.BlockSpec((1, 2, S, LANES), lambda b, h, i: (b, h, 0, 0)),
                  pl.BlockSpec((2, META_PAD, LANES), lambda b, h, i: (h, 0, 0)),
                  pl.BlockSpec((2, META_PAD, LANES), lambda b, h, i: (h, 0, 0))],
        out_specs=pl.BlockSpec((1, TQ, LANES), lambda b, h, i: (b, i, h)),
        out_shape=jax.ShapeDtypeStruct((B, S, D_ATTN), F32),
        compiler_params=_params(("arbitrary", "arbitrary", "arbitrary"), 32),
        name="fox_attn",
    )(qp, kp, vp, k_meta, v_meta)

    no = S // TM_OUT
    halo_per_tile = TM_OUT // HALO
    tile_blk = pl.BlockSpec((1, TM_OUT, D_CONV), lambda b, i: (b, i, 0))
    cw = jnp.pad(conv_w[0], ((0, HALO - CONV_WIDTH), (0, 0)))
    out = pl.pallas_call(
        _mix_body,
        grid=(B, no),
        in_specs=[pl.BlockSpec((1, TM_OUT, D), lambda b, i: (b, i, 0)),
                  tile_blk,
                  pl.BlockSpec((1, HALO, D_CONV), lambda b, i: (b, jnp.maximum(i * halo_per_tile - 1, 0), 0)),
                  _full((HALO, D_CONV)),
                  tile_blk, tile_blk, tile_blk,
                  _full((HALO, D_CONV)), _full((1, D_CONV)), _full((1, D_CONV)), _full((1, D_CONV)),
                  _full((D_CONV, D_CONV)), _full((D, D)),
                  _full((1, D)), _full((1, D)), _full((1, D)), _full((1, D))],
        out_specs=pl.BlockSpec((1, TM_OUT, D), lambda b, i: (b, i, 0)),
        out_shape=jax.ShapeDtypeStruct((B, S, D), F32),
        scratch_shapes=[pltpu.VMEM((HALO + TM_OUT, D_CONV), F32), pltpu.VMEM((TM_OUT, D_CONV), F32)],
        compiler_params=_params(("arbitrary", "arbitrary"), 48),
        name="mix_out",
    )(x, glu, glu, glu_meta, gc, att, ga,
      cw, conv_b[0].reshape(1, D_CONV), ln_conv_g[0].reshape(1, D_CONV), ln_conv_b[0].reshape(1, D_CONV),
      w_pw[0].astype(BF16), w_out[0].astype(BF16),
      lig, lib, ln_out_g[0].reshape(1, D), ln_out_b[0].reshape(1, D))
    return out
```

```python
import numpy as np

import jax
import jax.numpy as jnp
from jax import lax
from jax.experimental import pallas as pl
from jax.experimental.pallas import tpu as pltpu

F32 = jnp.float32
BF16 = jnp.bfloat16

D_MODEL = 1024
N_META = 16
D_ATTN = 512
D_CONV = 512
N_HEADS = 8
HEAD_DIM = 64
CONV_WIDTH = 31
LN_EPS = 1e-5
ALPHA = 2.0 ** 0.25
LANES = 128
SUBLANES = 8
BF16_ROWS = 16
META_PAD = 128
HALO = 32
MASK_BIAS = 1e30
V_ROWS = HEAD_DIM + BF16_ROWS

C_Q = 0
C_K = N_HEADS * LANES
C_F = 2 * N_HEADS * LANES
C_GA = C_F + LANES
C_U = C_GA + D_ATTN
C_UG = C_U + D_CONV
C_GC = C_UG + D_CONV
C_END = C_GC + D_CONV

TM_PROJ = 512
TQ = 512
TK = 512
HEADS_PER_STEP = 2
TM_OUT = 512
CONV_ROWS = 32


def _dot(a, b):
    return jnp.dot(a, b, preferred_element_type=F32)


def _dot_nt(a, b):
    return lax.dot_general(a, b, (((1,), (1,)), ((), ())), preferred_element_type=F32)


def _layer_norm(x, g, b):
    mu = jnp.mean(x, axis=-1, keepdims=True)
    xc = x - mu
    var = jnp.mean(xc * xc, axis=-1, keepdims=True)
    return xc * lax.rsqrt(var + LN_EPS) * g + b


def _sigmoid(x):
    return 1.0 / (1.0 + jnp.exp(-x))


def _silu(x):
    return x * _sigmoid(x)


def _log_sigmoid(x):
    return jnp.minimum(x, 0.0) - jnp.log1p(jnp.exp(-jnp.abs(x)))


def _split3(c):
    hi = c.astype(BF16).astype(F32)
    r = c - hi
    mid = r.astype(BF16).astype(F32)
    lo = (r - mid).astype(BF16).astype(F32)
    return hi, mid, lo


def _bias_operand(c, lane):
    hi, mid, lo = _split3(c)
    one = jnp.where(lane < 4 * N_HEADS, 1.0, 0.0)
    return jnp.where(lane < N_HEADS, hi,
                     jnp.where(lane < 2 * N_HEADS, mid,
                               jnp.where(lane < 3 * N_HEADS, lo, one))).astype(BF16)


def _forget_logs(hb, w_ref, bf_ref, lane):
    f = _dot(hb, w_ref[:, C_F:C_F + LANES])
    return jnp.where(lane < 3 * N_HEADS, _log_sigmoid(f + bf_ref[...]), 0.0)


def _exact_tri_dot(tri, x):
    hi, mid, lo = _split3(x)
    return _dot(tri, hi.astype(BF16)) + _dot(tri, mid.astype(BF16)) + _dot(tri, lo.astype(BF16))


def _ones_rows(cols, valid_cols):
    r = lax.broadcasted_iota(jnp.int32, (BF16_ROWS, cols), 0)
    c = lax.broadcasted_iota(jnp.int32, (BF16_ROWS, cols), 1)
    return jnp.where((r == 0) & (c < valid_cols), 1.0, 0.0).astype(BF16)


def _meta_body(x_ref, lng_ref, lnb_ref, w_ref, wvt_ref, bf_ref, pm_ref,
               k_ref, vt_ref, glu_ref):
    rows = x_ref.shape[0]
    hb = _layer_norm(x_ref[...], lng_ref[...], lnb_ref[...]).astype(BF16)
    lane = lax.broadcasted_iota(jnp.int32, (rows, LANES), 1)
    lf = _forget_logs(hb, w_ref, bf_ref, lane)
    r = lax.broadcasted_iota(jnp.int32, (rows, rows), 0)
    c = lax.broadcasted_iota(jnp.int32, (rows, rows), 1)
    upper = jnp.where((c > r) & (c < N_META), 1.0, 0.0).astype(BF16)
    cmat = _bias_operand(-_exact_tri_dot(upper, lf), lane)
    kk = _dot(hb, w_ref[:, C_K:C_F]) + _dot(cmat, pm_ref[:, C_K:C_F])
    vt = _dot_nt(wvt_ref[...], hb)
    real = lax.broadcasted_iota(jnp.int32, (D_ATTN, rows), 1) < N_META
    vt = jnp.where(real, vt, 0.0).astype(BF16)
    ones = _ones_rows(rows, N_META)
    for h in range(N_HEADS):
        k_ref[h] = kk[0:N_META, h * LANES:(h + 1) * LANES].astype(BF16)
        vt_ref[h, 0:HEAD_DIM, :] = vt[h * HEAD_DIM:(h + 1) * HEAD_DIM, :]
        vt_ref[h, HEAD_DIM:V_ROWS, :] = ones
    u = _dot(hb, w_ref[:, C_U:C_UG])
    ug = _dot(hb, w_ref[:, C_UG:C_GC])
    glu = u * _sigmoid(ug)
    glu_ref[0:HALO - N_META, :] = jnp.zeros((HALO - N_META, D_CONV), F32)
    glu_ref[HALO - N_META:HALO, :] = glu[0:N_META, :]


def _proj_body(x_ref, lng_ref, lnb_ref, w_ref, wvt_ref, bf_ref, pm_ref,
               q_ref, k_ref, vt_ref, ga_ref, glu_ref, gc_ref,
               carry_ref, tri_ref):
    tm = x_ref.shape[1]

    @pl.when(pl.program_id(1) == 0)
    def _start_sequence():
        carry_ref[...] = jnp.zeros_like(carry_ref)
        r = lax.broadcasted_iota(jnp.int32, (tm, tm), 0)
        c = lax.broadcasted_iota(jnp.int32, (tm, tm), 1)
        tri_ref[...] = jnp.where(r >= c, 1.0, 0.0).astype(BF16)

    hb = _layer_norm(x_ref[0], lng_ref[...], lnb_ref[...]).astype(BF16)
    lane = lax.broadcasted_iota(jnp.int32, (tm, LANES), 1)
    lf = _forget_logs(hb, w_ref, bf_ref, lane)
    c = _exact_tri_dot(tri_ref[...], lf) + carry_ref[...]
    carry_ref[...] = c[tm - 1:tm, :]
    cmat = _bias_operand(c, lane)

    qk = _dot(hb, w_ref[:, C_Q:C_F]) + _dot(cmat, pm_ref[...])
    for h in range(N_HEADS):
        q_ref[0, h] = qk[:, C_Q + h * LANES:C_Q + (h + 1) * LANES].astype(BF16)
        k_ref[0, h] = qk[:, C_K + h * LANES:C_K + (h + 1) * LANES].astype(BF16)
    vt = _dot_nt(wvt_ref[...], hb).astype(BF16)
    ones = _ones_rows(tm, tm)
    for h in range(N_HEADS):
        vt_ref[0, h, 0, 0:HEAD_DIM, :] = vt[h * HEAD_DIM:(h + 1) * HEAD_DIM, :]
        vt_ref[0, h, 0, HEAD_DIM:V_ROWS, :] = ones
    ga_ref[0] = _silu(_dot(hb, w_ref[:, C_GA:C_U]))
    u = _dot(hb, w_ref[:, C_U:C_UG])
    ug = _dot(hb, w_ref[:, C_UG:C_GC])
    glu_ref[0] = u * _sigmoid(ug)
    gc_ref[0] = _silu(_dot(hb, w_ref[:, C_GC:C_END]))


def _attn_body(q_ref, k_ref, vt_ref, km_ref, vmt_ref, o_ref):
    qi = pl.program_id(2)
    key = lax.broadcasted_iota(jnp.int32, (TK, TQ), 0)
    qry = lax.broadcasted_iota(jnp.int32, (TK, TQ), 1)
    causal = key <= qry
    pad = jnp.zeros((META_PAD - N_META, TQ), BF16)

    state = []
    for hh in range(HEADS_PER_STEP):
        s = _dot_nt(km_ref[hh], q_ref[0, hh])
        m = jnp.max(s, axis=0, keepdims=True)
        p = jnp.exp(s - m).astype(BF16)
        acc = _dot(vmt_ref[hh], jnp.concatenate([p, pad], axis=0))
        state += [m, acc]

    def block(j, state, diagonal):
        new = []
        for hh in range(HEADS_PER_STEP):
            m, acc = state[2 * hh], state[2 * hh + 1]
            off = pl.multiple_of(j * TK, TK)
            s = _dot_nt(k_ref[0, hh, pl.ds(off, TK), :], q_ref[0, hh])
            if diagonal:
                s = jnp.where(causal, s, -MASK_BIAS)
            m_new = jnp.maximum(m, jnp.max(s, axis=0, keepdims=True))
            p = jnp.exp(s - m_new).astype(BF16)
            acc = acc * jnp.exp(m - m_new) + _dot(vt_ref[0, hh, j], p)
            new += [m_new, acc]
        return tuple(new)

    state = lax.fori_loop(0, qi, lambda j, st: block(j, st, False), tuple(state))
    state = block(qi, state, True)
    outs = [state[2 * hh + 1][0:HEAD_DIM] / state[2 * hh + 1][HEAD_DIM:HEAD_DIM + 1]
            for hh in range(HEADS_PER_STEP)]
    o_ref[0] = jnp.concatenate(outs, axis=0).T


def _mix_body(x_ref, glu_ref, halo_ref, meta_ref, gc_ref, att_ref, ga_ref,
              cw_ref, cb_ref, lcg_ref, lcb_ref, wpw_ref, wout_ref,
              lig_ref, lib_ref, log_ref, lob_ref,
              o_ref, buf_ref, conv_ref):
    tm = glu_ref.shape[1]
    first = pl.program_id(1) == 0

    @pl.when(first)
    def _meta_halo():
        buf_ref[0:HALO, :] = meta_ref[...]

    @pl.when(jnp.logical_not(first))
    def _tile_halo():
        buf_ref[0:HALO, :] = halo_ref[0]

    buf_ref[HALO:HALO + tm, :] = glu_ref[0]

    shift = HALO - (CONV_WIDTH - 1)
    for chunk in range(tm // CONV_ROWS):
        base = chunk * CONV_ROWS
        acc = jnp.broadcast_to(cb_ref[...], (CONV_ROWS, D_CONV))
        for r in range(SUBLANES):
            taps = [j for j in range(CONV_WIDTH) if (shift + j) % SUBLANES == r]
            lo = base + shift + taps[0]
            win = buf_ref[lo:lo + (taps[-1] - taps[0]) + CONV_ROWS, :]
            for j in taps:
                d = j - taps[0]
                acc = acc + cw_ref[j:j + 1, :] * win[d:d + CONV_ROWS, :]
        conv_ref[base:base + CONV_ROWS, :] = acc

    z = _silu(_layer_norm(conv_ref[...], lcg_ref[...], lcb_ref[...]))
    y_conv = _dot(z.astype(BF16), wpw_ref[...]) * gc_ref[0]
    y_attn = att_ref[0] * ga_ref[0]
    y = _dot(y_attn.astype(BF16), wout_ref[0:D_ATTN, :]) + _dot(y_conv.astype(BF16), wout_ref[D_ATTN:, :])
    h = _layer_norm(x_ref[0], lig_ref[...], lib_ref[...])
    o_ref[0] = _layer_norm(ALPHA * h + y, log_ref[...], lob_ref[...])


def _pad_heads(w):
    w = w.reshape(D_MODEL, N_HEADS, HEAD_DIM)
    w = jnp.pad(w, ((0, 0), (0, 0), (0, LANES - HEAD_DIM)))
    return w.reshape(D_MODEL, N_HEADS * LANES)


def _placement():
    pm = np.zeros((LANES, 2 * N_HEADS * LANES), np.float32)
    for h in range(N_HEADS):
        for g in range(3):
            pm[g * N_HEADS + h, C_Q + h * LANES + HEAD_DIM + g] = 1.0
            pm[3 * N_HEADS + h, C_Q + h * LANES + HEAD_DIM + 3 + g] = 1.0
            pm[3 * N_HEADS + h, C_K + h * LANES + HEAD_DIM + g] = 1.0
            pm[g * N_HEADS + h, C_K + h * LANES + HEAD_DIM + 3 + g] = -1.0
    return jnp.asarray(pm, BF16)


def _full(shape):
    return pl.BlockSpec(shape, lambda *_: (0,) * len(shape))


def _params(sem, vmem_mb):
    return pltpu.CompilerParams(dimension_semantics=sem, vmem_limit_bytes=vmem_mb * 1024 * 1024)


def kernel(x, meta, ln_in_g, ln_in_b, w_in, b_f, conv_w, conv_b, ln_conv_g, ln_conv_b,
           w_pw, w_out, ln_out_g, ln_out_b):
    B, S, D = x.shape
    assert D == D_MODEL and w_in.shape[0] == 1 and meta.shape == (N_META, D_MODEL)
    assert S % TM_PROJ == 0 and S % TM_OUT == 0 and S % TQ == 0 and TQ == TK == TM_PROJ

    w = w_in[0]
    scale = HEAD_DIM ** -0.5
    wf = jnp.pad(jnp.tile(w[:, 3 * D_ATTN:3 * D_ATTN + N_HEADS], (1, 3)), ((0, 0), (0, LANES - 3 * N_HEADS)))
    w_all = jnp.concatenate([
        _pad_heads(w[:, 0:D_ATTN] * scale), _pad_heads(w[:, D_ATTN:2 * D_ATTN]),
        wf, w[:, 3 * D_ATTN + N_HEADS:]], axis=1).astype(BF16)
    wvt = w[:, 2 * D_ATTN:3 * D_ATTN].T.astype(BF16)
    bf_row = jnp.pad(jnp.tile(b_f[0], 3), (0, LANES - 3 * N_HEADS)).reshape(1, LANES)
    pm = _placement()
    lig = ln_in_g.reshape(1, D)
    lib = ln_in_b.reshape(1, D)
    meta_pad = jnp.pad(meta, ((0, META_PAD - N_META), (0, 0)))

    k_meta, vt_meta, glu_meta = pl.pallas_call(
        _meta_body,
        out_shape=(jax.ShapeDtypeStruct((N_HEADS, N_META, LANES), BF16),
                   jax.ShapeDtypeStruct((N_HEADS, V_ROWS, META_PAD), BF16),
                   jax.ShapeDtypeStruct((HALO, D_CONV), F32)),
        compiler_params=pltpu.CompilerParams(vmem_limit_bytes=48 * 1024 * 1024),
        name="meta_proj",
    )(meta_pad, lig, lib, w_all, wvt, bf_row, pm)

    nt = S // TM_PROJ
    head_blk = pl.BlockSpec((1, N_HEADS, TM_PROJ, LANES), lambda b, i: (b, 0, i, 0))
    row_blk = pl.BlockSpec((1, TM_PROJ, D_ATTN), lambda b, i: (b, i, 0))
    qp, kp, vtp, ga, glu, gc = pl.pallas_call(
        _proj_body,
        grid=(B, nt),
        in_specs=[pl.BlockSpec((1, TM_PROJ, D), lambda b, i: (b, i, 0)),
                  _full((1, D)), _full((1, D)), _full((D, C_END)), _full((D_ATTN, D)), _full((1, LANES)),
                  _full((LANES, 2 * N_HEADS * LANES))],
        out_specs=(head_blk, head_blk,
                   pl.BlockSpec((1, N_HEADS, 1, V_ROWS, TM_PROJ), lambda b, i: (b, 0, i, 0, 0)),
                   row_blk, row_blk, row_blk),
        out_shape=(jax.ShapeDtypeStruct((B, N_HEADS, S, LANES), BF16),) * 2
                  + (jax.ShapeDtypeStruct((B, N_HEADS, nt, V_ROWS, TM_PROJ), BF16),)
                  + (jax.ShapeDtypeStruct((B, S, D_ATTN), F32),) * 3,
        scratch_shapes=[pltpu.VMEM((1, LANES), F32), pltpu.VMEM((TM_PROJ, TM_PROJ), BF16)],
        compiler_params=_params(("arbitrary", "arbitrary"), 56),
        name="in_proj",
    )(x, lig, lib, w_all, wvt, bf_row, pm)

    hp = HEADS_PER_STEP
    att = pl.pallas_call(
        _attn_body,
        grid=(B, N_HEADS // hp, S // TQ),
        in_specs=[pl.BlockSpec((1, hp, TQ, LANES), lambda b, h, i: (b, h, i, 0)),
                  pl.BlockSpec((1, hp, S, LANES), lambda b, h, i: (b, h, 0, 0)),
                  pl.BlockSpec((1, hp, nt, V_ROWS, TK), lambda b, h, i: (b, h, 0, 0, 0)),
                  pl.BlockSpec((hp, N_META, LANES), lambda b, h, i: (h, 0, 0)),
                  pl.BlockSpec((hp, V_ROWS, META_PAD), lambda b, h, i: (h, 0, 0))],
        out_specs=pl.BlockSpec((1, TQ, hp * HEAD_DIM), lambda b, h, i: (b, i, h)),
        out_shape=jax.ShapeDtypeStruct((B, S, D_ATTN), F32),
        compiler_params=_params(("arbitrary", "arbitrary", "arbitrary"), 32),
        name="fox_attn",
    )(qp, kp, vtp, k_meta, vt_meta)

    no = S // TM_OUT
    halo_per_tile = TM_OUT // HALO
    tile_blk = pl.BlockSpec((1, TM_OUT, D_CONV), lambda b, i: (b, i, 0))
    cw = jnp.pad(conv_w[0], ((0, HALO - CONV_WIDTH), (0, 0)))
    out = pl.pallas_call(
        _mix_body,
        grid=(B, no),
        in_specs=[pl.BlockSpec((1, TM_OUT, D), lambda b, i: (b, i, 0)),
                  tile_blk,
                  pl.BlockSpec((1, HALO, D_CONV), lambda b, i: (b, jnp.maximum(i * halo_per_tile - 1, 0), 0)),
                  _full((HALO, D_CONV)),
                  tile_blk, tile_blk, tile_blk,
                  _full((HALO, D_CONV)), _full((1, D_CONV)), _full((1, D_CONV)), _full((1, D_CONV)),
                  _full((D_CONV, D_CONV)), _full((D, D)),
                  _full((1, D)), _full((1, D)), _full((1, D)), _full((1, D))],
        out_specs=pl.BlockSpec((1, TM_OUT, D), lambda b, i: (b, i, 0)),
        out_shape=jax.ShapeDtypeStruct((B, S, D), F32),
        scratch_shapes=[pltpu.VMEM((HALO + TM_OUT, D_CONV), F32), pltpu.VMEM((TM_OUT, D_CONV), F32)],
        compiler_params=_params(("arbitrary", "arbitrary"), 48),
        name="mix_out",
    )(x, glu, glu, glu_meta, gc, att, ga,
      cw, conv_b[0].reshape(1, D_CONV), ln_conv_g[0].reshape(1, D_CONV), ln_conv_b[0].reshape(1, D_CONV),
      w_pw[0].astype(BF16), w_out[0].astype(BF16),
      lig, lib, ln_out_g[0].reshape(1, D), ln_out_b[0].reshape(1, D))
    return out
```

```python
import numpy as np

import jax
import jax.numpy as jnp
from jax import lax
from jax.experimental import pallas as pl
from jax.experimental.pallas import tpu as pltpu

F32 = jnp.float32
BF16 = jnp.bfloat16

D_MODEL = 1024
N_META = 16
D_ATTN = 512
D_CONV = 512
N_HEADS = 8
HEAD_DIM = 64
CONV_WIDTH = 31
LN_EPS = 1e-5
ALPHA = 2.0 ** 0.25
LANES = 128
SUBLANES = 8
BF16_ROWS = 16
META_PAD = 128
HALO = 32
MASK_BIAS = 1e30
LOG2E = 1.4426950408889634
V_ROWS = HEAD_DIM + BF16_ROWS

C_Q = 0
C_K = N_HEADS * LANES
C_F = 2 * N_HEADS * LANES
C_GA = C_F + LANES
C_U = C_GA + D_ATTN
C_UG = C_U + D_CONV
C_GC = C_UG + D_CONV
C_END = C_GC + D_CONV

TM_PROJ = 512
TQ = 512
TK = 512
HEADS_PER_STEP = 8
TM_OUT = 512
CONV_ROWS = 32


def _dot(a, b):
    return jnp.dot(a, b, preferred_element_type=F32)


def _dot_nt(a, b):
    return lax.dot_general(a, b, (((1,), (1,)), ((), ())), preferred_element_type=F32)


def _layer_norm(x, g, b):
    mu = jnp.mean(x, axis=-1, keepdims=True)
    xc = x - mu
    var = jnp.mean(xc * xc, axis=-1, keepdims=True)
    return xc * lax.rsqrt(var + LN_EPS) * g + b


def _sigmoid(x):
    return 1.0 / (1.0 + jnp.exp(-x))


def _silu(x):
    return x * _sigmoid(x)


def _log_sigmoid(x):
    return jnp.minimum(x, 0.0) - jnp.log1p(jnp.exp(-jnp.abs(x)))


def _split3(c):
    hi = c.astype(BF16).astype(F32)
    r = c - hi
    mid = r.astype(BF16).astype(F32)
    lo = (r - mid).astype(BF16).astype(F32)
    return hi, mid, lo


def _bias_operand(c, lane):
    hi, mid, lo = _split3(c * LOG2E)
    one = jnp.where(lane < 4 * N_HEADS, 1.0, 0.0)
    return jnp.where(lane < N_HEADS, hi,
                     jnp.where(lane < 2 * N_HEADS, mid,
                               jnp.where(lane < 3 * N_HEADS, lo, one))).astype(BF16)


def _forget_logs(hb, w_ref, bf_ref, lane):
    f = _dot(hb, w_ref[:, C_F:C_F + LANES])
    return jnp.where(lane < 3 * N_HEADS, _log_sigmoid(f + bf_ref[...]), 0.0)


def _exact_tri_dot(tri, x):
    hi, mid, lo = _split3(x)
    return _dot(tri, hi.astype(BF16)) + _dot(tri, mid.astype(BF16)) + _dot(tri, lo.astype(BF16))


def _ones_rows(cols, valid_cols):
    r = lax.broadcasted_iota(jnp.int32, (BF16_ROWS, cols), 0)
    c = lax.broadcasted_iota(jnp.int32, (BF16_ROWS, cols), 1)
    return jnp.where((r == 0) & (c < valid_cols), 1.0, 0.0).astype(BF16)


def _meta_body(x_ref, lng_ref, lnb_ref, w_ref, wvt_ref, bf_ref, pm_ref,
               k_ref, vt_ref, glu_ref):
    rows = x_ref.shape[0]
    hb = _layer_norm(x_ref[...], lng_ref[...], lnb_ref[...]).astype(BF16)
    lane = lax.broadcasted_iota(jnp.int32, (rows, LANES), 1)
    lf = _forget_logs(hb, w_ref, bf_ref, lane)
    r = lax.broadcasted_iota(jnp.int32, (rows, rows), 0)
    c = lax.broadcasted_iota(jnp.int32, (rows, rows), 1)
    upper = jnp.where((c > r) & (c < N_META), 1.0, 0.0).astype(BF16)
    cmat = _bias_operand(-_exact_tri_dot(upper, lf), lane)
    kk = _dot(hb, w_ref[:, C_K:C_F]) + _dot(cmat, pm_ref[:, C_K:C_F])
    vt = _dot_nt(wvt_ref[...], hb)
    real = lax.broadcasted_iota(jnp.int32, (D_ATTN, rows), 1) < N_META
    vt = jnp.where(real, vt, 0.0).astype(BF16)
    ones = _ones_rows(rows, N_META)
    for h in range(N_HEADS):
        k_ref[h] = kk[0:N_META, h * LANES:(h + 1) * LANES].astype(BF16)
        vt_ref[h, 0:HEAD_DIM, :] = vt[h * HEAD_DIM:(h + 1) * HEAD_DIM, :]
        vt_ref[h, HEAD_DIM:V_ROWS, :] = ones
    u = _dot(hb, w_ref[:, C_U:C_UG])
    ug = _dot(hb, w_ref[:, C_UG:C_GC])
    glu = u * _sigmoid(ug)
    glu_ref[0:HALO - N_META, :] = jnp.zeros((HALO - N_META, D_CONV), F32)
    glu_ref[HALO - N_META:HALO, :] = glu[0:N_META, :]


def _proj_body(x_ref, lng_ref, lnb_ref, w_ref, wvt_ref, bf_ref, pm_ref,
               q_ref, k_ref, vt_ref, ga_ref, glu_ref, gc_ref,
               carry_ref, tri_ref):
    tm = x_ref.shape[1]

    @pl.when(pl.program_id(1) == 0)
    def _start_sequence():
        carry_ref[...] = jnp.zeros_like(carry_ref)
        r = lax.broadcasted_iota(jnp.int32, (tm, tm), 0)
        c = lax.broadcasted_iota(jnp.int32, (tm, tm), 1)
        tri_ref[...] = jnp.where(r >= c, 1.0, 0.0).astype(BF16)

    hb = _layer_norm(x_ref[0], lng_ref[...], lnb_ref[...]).astype(BF16)
    lane = lax.broadcasted_iota(jnp.int32, (tm, LANES), 1)
    lf = _forget_logs(hb, w_ref, bf_ref, lane)
    c = _exact_tri_dot(tri_ref[...], lf) + carry_ref[...]
    carry_ref[...] = c[tm - 1:tm, :]
    cmat = _bias_operand(c, lane)

    qk = _dot(hb, w_ref[:, C_Q:C_F]) + _dot(cmat, pm_ref[...])
    for h in range(N_HEADS):
        q_ref[0, h] = qk[:, C_Q + h * LANES:C_Q + (h + 1) * LANES].astype(BF16)
        k_ref[0, h] = qk[:, C_K + h * LANES:C_K + (h + 1) * LANES].astype(BF16)
    vt = _dot_nt(wvt_ref[...], hb).astype(BF16)
    ones = _ones_rows(tm, tm)
    for h in range(N_HEADS):
        vt_ref[0, h, 0, 0:HEAD_DIM, :] = vt[h * HEAD_DIM:(h + 1) * HEAD_DIM, :]
        vt_ref[0, h, 0, HEAD_DIM:V_ROWS, :] = ones
    ga_ref[0] = _silu(_dot(hb, w_ref[:, C_GA:C_U]))
    u = _dot(hb, w_ref[:, C_U:C_UG])
    ug = _dot(hb, w_ref[:, C_UG:C_GC])
    glu_ref[0] = u * _sigmoid(ug)
    gc_ref[0] = _silu(_dot(hb, w_ref[:, C_GC:C_END]))


def _attn_body(q_ref, k_ref, vt_ref, km_ref, vmt_ref, o_ref, s_ref, pen_ref, m_ref, acc_ref):
    qi = pl.program_id(2)
    heads = range(HEADS_PER_STEP)

    @pl.when((pl.program_id(0) == 0) & (qi == 0))
    def _causal_penalty():
        key = lax.broadcasted_iota(jnp.int32, (TK, TQ), 0)
        qry = lax.broadcasted_iota(jnp.int32, (TK, TQ), 1)
        pen_ref[0] = jnp.zeros((TK, TQ), F32)
        pen_ref[1] = jnp.where(key <= qry, 0.0, -MASK_BIAS)

    pad = jnp.zeros((META_PAD - N_META, TQ), BF16)
    s_meta = [_dot_nt(km_ref[hh], q_ref[0, hh]) for hh in heads]
    m_meta = [jnp.max(s, axis=0, keepdims=True) for s in s_meta]
    p_meta = [jnp.concatenate([jnp.exp2(s - m).astype(BF16), pad], axis=0) for s, m in zip(s_meta, m_meta)]
    for hh in heads:
        m_ref[hh] = m_meta[hh]
        acc_ref[hh] = _dot(vmt_ref[hh], p_meta[hh])

    def produce(j, hh, penalty):
        off = pl.multiple_of(j * TK, TK)
        s = _dot_nt(k_ref[0, hh, pl.ds(off, TK), :], q_ref[0, hh])
        if penalty is not None:
            s = s + pen_ref[penalty]
        s_ref[hh % 2] = s
        return jnp.max(s, axis=0, keepdims=True)

    def consume(j, hh, col_max):
        m = m_ref[hh]
        m_new = jnp.maximum(m, col_max)
        p = jnp.exp2(s_ref[hh % 2] - m_new).astype(BF16)
        acc_ref[hh] = acc_ref[hh] * jnp.exp2(m - m_new) + _dot(vt_ref[0, hh, j], p)
        m_ref[hh] = m_new

    def diagonal_flag(j):
        return jnp.where(j < qi, 0, 1)

    def body(j, col_max):
        for hh in heads:
            if hh + 1 < HEADS_PER_STEP:
                nxt_max = produce(j, hh + 1, None)
            else:
                nxt_max = produce(j + 1, 0, diagonal_flag(j + 1))
            consume(j, hh, col_max)
            col_max = nxt_max
        return col_max

    col_max = lax.fori_loop(0, qi, body, produce(0, 0, diagonal_flag(0)))
    for hh in heads:
        nxt_max = produce(qi, hh + 1, 1) if hh + 1 < HEADS_PER_STEP else None
        consume(qi, hh, col_max)
        col_max = nxt_max
    outs = [acc_ref[hh, 0:HEAD_DIM, :] / acc_ref[hh, HEAD_DIM:HEAD_DIM + 1, :] for hh in heads]
    o_ref[0] = jnp.concatenate(outs, axis=0).T


def _mix_body(x_ref, glu_ref, halo_ref, meta_ref, gc_ref, att_ref, ga_ref,
              cw_ref, cb_ref, lcg_ref, lcb_ref, wpw_ref, wout_ref,
              lig_ref, lib_ref, log_ref, lob_ref,
              o_ref, buf_ref, conv_ref):
    tm = glu_ref.shape[1]
    first = pl.program_id(1) == 0

    @pl.when(first)
    def _meta_halo():
        buf_ref[0:HALO, :] = meta_ref[...]

    @pl.when(jnp.logical_not(first))
    def _tile_halo():
        buf_ref[0:HALO, :] = halo_ref[0]

    buf_ref[HALO:HALO + tm, :] = glu_ref[0]

    shift = HALO - (CONV_WIDTH - 1)
    for chunk in range(tm // CONV_ROWS):
        base = chunk * CONV_ROWS
        acc = jnp.broadcast_to(cb_ref[...], (CONV_ROWS, D_CONV))
        for r in range(SUBLANES):
            taps = [j for j in range(CONV_WIDTH) if (shift + j) % SUBLANES == r]
            lo = base + shift + taps[0]
            win = buf_ref[lo:lo + (taps[-1] - taps[0]) + CONV_ROWS, :]
            for j in taps:
                d = j - taps[0]
                acc = acc + cw_ref[j:j + 1, :] * win[d:d + CONV_ROWS, :]
        conv_ref[base:base + CONV_ROWS, :] = acc

    z = _silu(_layer_norm(conv_ref[...], lcg_ref[...], lcb_ref[...]))
    y_conv = _dot(z.astype(BF16), wpw_ref[...]) * gc_ref[0]
    y_attn = att_ref[0] * ga_ref[0]
    y = _dot(y_attn.astype(BF16), wout_ref[0:D_ATTN, :]) + _dot(y_conv.astype(BF16), wout_ref[D_ATTN:, :])
    h = _layer_norm(x_ref[0], lig_ref[...], lib_ref[...])
    o_ref[0] = _layer_norm(ALPHA * h + y, log_ref[...], lob_ref[...])


def _pad_heads(w):
    w = w.reshape(D_MODEL, N_HEADS, HEAD_DIM)
    w = jnp.pad(w, ((0, 0), (0, 0), (0, LANES - HEAD_DIM)))
    return w.reshape(D_MODEL, N_HEADS * LANES)


def _placement():
    pm = np.zeros((LANES, 2 * N_HEADS * LANES), np.float32)
    for h in range(N_HEADS):
        for g in range(3):
            pm[g * N_HEADS + h, C_Q + h * LANES + HEAD_DIM + g] = 1.0
            pm[3 * N_HEADS + h, C_Q + h * LANES + HEAD_DIM + 3 + g] = 1.0
            pm[3 * N_HEADS + h, C_K + h * LANES + HEAD_DIM + g] = 1.0
            pm[g * N_HEADS + h, C_K + h * LANES + HEAD_DIM + 3 + g] = -1.0
    return jnp.asarray(pm, BF16)


def _full(shape):
    return pl.BlockSpec(shape, lambda *_: (0,) * len(shape))


def _params(sem, vmem_mb):
    return pltpu.CompilerParams(dimension_semantics=sem, vmem_limit_bytes=vmem_mb * 1024 * 1024)


def kernel(x, meta, ln_in_g, ln_in_b, w_in, b_f, conv_w, conv_b, ln_conv_g, ln_conv_b,
           w_pw, w_out, ln_out_g, ln_out_b):
    B, S, D = x.shape
    assert D == D_MODEL and w_in.shape[0] == 1 and meta.shape == (N_META, D_MODEL)
    assert S % TM_PROJ == 0 and S % TM_OUT == 0 and S % TQ == 0 and TQ == TK == TM_PROJ

    w = w_in[0]
    scale = HEAD_DIM ** -0.5 * LOG2E
    wf = jnp.pad(jnp.tile(w[:, 3 * D_ATTN:3 * D_ATTN + N_HEADS], (1, 3)), ((0, 0), (0, LANES - 3 * N_HEADS)))
    w_all = jnp.concatenate([
        _pad_heads(w[:, 0:D_ATTN] * scale), _pad_heads(w[:, D_ATTN:2 * D_ATTN]),
        wf, w[:, 3 * D_ATTN + N_HEADS:]], axis=1).astype(BF16)
    wvt = w[:, 2 * D_ATTN:3 * D_ATTN].T.astype(BF16)
    bf_row = jnp.pad(jnp.tile(b_f[0], 3), (0, LANES - 3 * N_HEADS)).reshape(1, LANES)
    pm = _placement()
    lig = ln_in_g.reshape(1, D)
    lib = ln_in_b.reshape(1, D)
    meta_pad = jnp.pad(meta, ((0, META_PAD - N_META), (0, 0)))

    k_meta, vt_meta, glu_meta = pl.pallas_call(
        _meta_body,
        out_shape=(jax.ShapeDtypeStruct((N_HEADS, N_META, LANES), BF16),
                   jax.ShapeDtypeStruct((N_HEADS, V_ROWS, META_PAD), BF16),
                   jax.ShapeDtypeStruct((HALO, D_CONV), F32)),
        compiler_params=pltpu.CompilerParams(vmem_limit_bytes=48 * 1024 * 1024),
        name="meta_proj",
    )(meta_pad, lig, lib, w_all, wvt, bf_row, pm)

    nt = S // TM_PROJ
    head_blk = pl.BlockSpec((1, N_HEADS, TM_PROJ, LANES), lambda b, i: (b, 0, i, 0))
    row_blk = pl.BlockSpec((1, TM_PROJ, D_ATTN), lambda b, i: (b, i, 0))
    qp, kp, vtp, ga, glu, gc = pl.pallas_call(
        _proj_body,
        grid=(B, nt),
        in_specs=[pl.BlockSpec((1, TM_PROJ, D), lambda b, i: (b, i, 0)),
                  _full((1, D)), _full((1, D)), _full((D, C_END)), _full((D_ATTN, D)), _full((1, LANES)),
                  _full((LANES, 2 * N_HEADS * LANES))],
        out_specs=(head_blk, head_blk,
                   pl.BlockSpec((1, N_HEADS, 1, V_ROWS, TM_PROJ), lambda b, i: (b, 0, i, 0, 0)),
                   row_blk, row_blk, row_blk),
        out_shape=(jax.ShapeDtypeStruct((B, N_HEADS, S, LANES), BF16),) * 2
                  + (jax.ShapeDtypeStruct((B, N_HEADS, nt, V_ROWS, TM_PROJ), BF16),)
                  + (jax.ShapeDtypeStruct((B, S, D_ATTN), F32),) * 3,
        scratch_shapes=[pltpu.VMEM((1, LANES), F32), pltpu.VMEM((TM_PROJ, TM_PROJ), BF16)],
        compiler_params=_params(("arbitrary", "arbitrary"), 56),
        name="in_proj",
    )(x, lig, lib, w_all, wvt, bf_row, pm)

    hp = HEADS_PER_STEP
    att = pl.pallas_call(
        _attn_body,
        grid=(B, N_HEADS // hp, S // TQ),
        in_specs=[pl.BlockSpec((1, hp, TQ, LANES), lambda b, h, i: (b, h, i, 0)),
                  pl.BlockSpec((1, hp, S, LANES), lambda b, h, i: (b, h, 0, 0)),
                  pl.BlockSpec((1, hp, nt, V_ROWS, TK), lambda b, h, i: (b, h, 0, 0, 0)),
                  pl.BlockSpec((hp, N_META, LANES), lambda b, h, i: (h, 0, 0)),
                  pl.BlockSpec((hp, V_ROWS, META_PAD), lambda b, h, i: (h, 0, 0))],
        out_specs=pl.BlockSpec((1, TQ, hp * HEAD_DIM), lambda b, h, i: (b, i, h)),
        out_shape=jax.ShapeDtypeStruct((B, S, D_ATTN), F32),
        scratch_shapes=[pltpu.VMEM((2, TK, TQ), F32), pltpu.VMEM((2, TK, TQ), F32),
                        pltpu.VMEM((hp, 1, TQ), F32), pltpu.VMEM((hp, V_ROWS, TQ), F32)],
        compiler_params=_params(("arbitrary", "arbitrary", "arbitrary"), 48),
        name="fox_attn",
    )(qp, kp, vtp, k_meta, vt_meta)

    no = S // TM_OUT
    halo_per_tile = TM_OUT // HALO
    tile_blk = pl.BlockSpec((1, TM_OUT, D_CONV), lambda b, i: (b, i, 0))
    cw = jnp.pad(conv_w[0], ((0, HALO - CONV_WIDTH), (0, 0)))
    out = pl.pallas_call(
        _mix_body,
        grid=(B, no),
        in_specs=[pl.BlockSpec((1, TM_OUT, D), lambda b, i: (b, i, 0)),
                  tile_blk,
                  pl.BlockSpec((1, HALO, D_CONV), lambda b, i: (b, jnp.maximum(i * halo_per_tile - 1, 0), 0)),
                  _full((HALO, D_CONV)),
                  tile_blk, tile_blk, tile_blk,
                  _full((HALO, D_CONV)), _full((1, D_CONV)), _full((1, D_CONV)), _full((1, D_CONV)),
                  _full((D_CONV, D_CONV)), _full((D, D)),
                  _full((1, D)), _full((1, D)), _full((1, D)), _full((1, D))],
        out_specs=pl.BlockSpec((1, TM_OUT, D), lambda b, i: (b, i, 0)),
        out_shape=jax.ShapeDtypeStruct((B, S, D), F32),
        scratch_shapes=[pltpu.VMEM((HALO + TM_OUT, D_CONV), F32), pltpu.VMEM((TM_OUT, D_CONV), F32)],
        compiler_params=_params(("arbitrary", "arbitrary"), 48),
        name="mix_out",
    )(x, glu, glu, glu_meta, gc, att, ga,
      cw, conv_b[0].reshape(1, D_CONV), ln_conv_g[0].reshape(1, D_CONV), ln_conv_b[0].reshape(1, D_CONV),
      w_pw[0].astype(BF16), w_out[0].astype(BF16),
      lig, lib, ln_out_g[0].reshape(1, D), ln_out_b[0].reshape(1, D))
    return out
```

```python
import numpy as np

import jax
import jax.numpy as jnp
from jax import lax
from jax.experimental import pallas as pl
from jax.experimental.pallas import tpu as pltpu

F32 = jnp.float32
BF16 = jnp.bfloat16

D_MODEL = 1024
N_META = 16
D_ATTN = 512
D_CONV = 512
N_HEADS = 8
HEAD_DIM = 64
CONV_WIDTH = 31
LN_EPS = 1e-5
ALPHA = 2.0 ** 0.25
LANES = 128
SUBLANES = 8
BF16_ROWS = 16
META_PAD = 128
HALO = 32
MASK_BIAS = 1e30
LOG2E = 1.4426950408889634
V_ROWS = HEAD_DIM + BF16_ROWS

C_Q = 0
C_K = N_HEADS * LANES
C_F = 2 * N_HEADS * LANES
C_GA = C_F + LANES
C_U = C_GA + D_ATTN
C_UG = C_U + D_CONV
C_GC = C_UG + D_CONV
C_END = C_GC + D_CONV

TM_PROJ = 512
TQ = 512
TK = 512
HEADS_PER_STEP = 8
TM_OUT = 512
CONV_ROWS = 64


def _dot(a, b):
    return jnp.dot(a, b, preferred_element_type=F32)


def _dot_nt(a, b):
    return lax.dot_general(a, b, (((1,), (1,)), ((), ())), preferred_element_type=F32)


def _layer_norm(x, g, b):
    mu = jnp.mean(x, axis=-1, keepdims=True)
    xc = x - mu
    var = jnp.mean(xc * xc, axis=-1, keepdims=True)
    return xc * lax.rsqrt(var + LN_EPS) * g + b


def _sigmoid(x):
    return 1.0 / (1.0 + jnp.exp(-x))


def _silu(x):
    return x * _sigmoid(x)


def _log_sigmoid(x):
    return jnp.minimum(x, 0.0) - jnp.log1p(jnp.exp(-jnp.abs(x)))


def _split3(c):
    hi = c.astype(BF16).astype(F32)
    r = c - hi
    mid = r.astype(BF16).astype(F32)
    lo = (r - mid).astype(BF16).astype(F32)
    return hi, mid, lo


def _bias_operand(c, lane):
    hi, mid, lo = _split3(c * LOG2E)
    one = jnp.where(lane < 4 * N_HEADS, 1.0, 0.0)
    return jnp.where(lane < N_HEADS, hi,
                     jnp.where(lane < 2 * N_HEADS, mid,
                               jnp.where(lane < 3 * N_HEADS, lo, one))).astype(BF16)


def _forget_logs(hb, w_ref, bf_ref, lane):
    f = _dot(hb, w_ref[:, C_F:C_F + LANES])
    return jnp.where(lane < 3 * N_HEADS, _log_sigmoid(f + bf_ref[...]), 0.0)


def _exact_tri_dot(tri, x):
    hi, mid, lo = _split3(x)
    return _dot(tri, hi.astype(BF16)) + _dot(tri, mid.astype(BF16)) + _dot(tri, lo.astype(BF16))


def _ones_rows(cols, valid_cols):
    r = lax.broadcasted_iota(jnp.int32, (BF16_ROWS, cols), 0)
    c = lax.broadcasted_iota(jnp.int32, (BF16_ROWS, cols), 1)
    return jnp.where((r == 0) & (c < valid_cols), 1.0, 0.0).astype(BF16)


def _meta_body(x_ref, lng_ref, lnb_ref, w_ref, wvt_ref, bf_ref, pm_ref,
               k_ref, vt_ref, glu_ref):
    rows = x_ref.shape[0]
    hb = _layer_norm(x_ref[...], lng_ref[...], lnb_ref[...]).astype(BF16)
    lane = lax.broadcasted_iota(jnp.int32, (rows, LANES), 1)
    lf = _forget_logs(hb, w_ref, bf_ref, lane)
    r = lax.broadcasted_iota(jnp.int32, (rows, rows), 0)
    c = lax.broadcasted_iota(jnp.int32, (rows, rows), 1)
    upper = jnp.where((c > r) & (c < N_META), 1.0, 0.0).astype(BF16)
    cmat = _bias_operand(-_exact_tri_dot(upper, lf), lane)
    kk = _dot(hb, w_ref[:, C_K:C_F]) + _dot(cmat, pm_ref[:, C_K:C_F])
    vt = _dot_nt(wvt_ref[...], hb)
    real = lax.broadcasted_iota(jnp.int32, (D_ATTN, rows), 1) < N_META
    vt = jnp.where(real, vt, 0.0).astype(BF16)
    ones = _ones_rows(rows, N_META)
    for h in range(N_HEADS):
        k_ref[h] = kk[0:N_META, h * LANES:(h + 1) * LANES].astype(BF16)
        vt_ref[h, 0:HEAD_DIM, :] = vt[h * HEAD_DIM:(h + 1) * HEAD_DIM, :]
        vt_ref[h, HEAD_DIM:V_ROWS, :] = ones
    u = _dot(hb, w_ref[:, C_U:C_UG])
    ug = _dot(hb, w_ref[:, C_UG:C_GC])
    glu = u * _sigmoid(ug)
    glu_ref[0:HALO - N_META, :] = jnp.zeros((HALO - N_META, D_CONV), F32)
    glu_ref[HALO - N_META:HALO, :] = glu[0:N_META, :]


def _conv_rows(buf_ref, cw_ref, cb_ref, out_ref, base, zero):
    shift = HALO - (CONV_WIDTH - 1)
    for lo in range(0, D_CONV, LANES):
        cols = slice(lo, lo + LANES)
        out = jnp.broadcast_to(cb_ref[:, cols] + zero, (CONV_ROWS, LANES))
        for r in range(SUBLANES):
            window = CONV_ROWS + (SUBLANES if r else 0)
            part = None
            for j in range(CONV_WIDTH):
                if (shift + j) % SUBLANES != r:
                    continue
                start = base + shift + j - r
                term = cw_ref[j:j + 1, cols] * buf_ref[start:start + window, cols]
                part = term if part is None else part + term
            out = out + part[r:r + CONV_ROWS, :]
        out_ref[base:base + CONV_ROWS, cols] = out


def _proj_body(x_ref, lng_ref, lnb_ref, w_ref, wvt_ref, bf_ref, pm_ref, meta_glu_ref,
               cw_ref, cb_ref, lcg_ref, lcb_ref, wpw_ref, zero_ref,
               q_ref, k_ref, vt_ref, ga_ref, yc_ref,
               carry_ref, tri_ref, buf_ref, conv_ref):
    tm = x_ref.shape[1]
    first = pl.program_id(1) == 0

    @pl.when(first)
    def _start_sequence():
        carry_ref[...] = jnp.zeros_like(carry_ref)
        r = lax.broadcasted_iota(jnp.int32, (tm, tm), 0)
        c = lax.broadcasted_iota(jnp.int32, (tm, tm), 1)
        tri_ref[...] = jnp.where(r >= c, 1.0, 0.0).astype(BF16)
        buf_ref[0:HALO, :] = meta_glu_ref[...]

    @pl.when(jnp.logical_not(first))
    def _carry_conv_context():
        buf_ref[0:HALO, :] = buf_ref[tm:tm + HALO, :]

    hb = _layer_norm(x_ref[0], lng_ref[...], lnb_ref[...]).astype(BF16)
    u = _dot(hb, w_ref[:, C_U:C_UG])
    ug = _dot(hb, w_ref[:, C_UG:C_GC])
    buf_ref[HALO:HALO + tm, :] = u * _sigmoid(ug)

    lane = lax.broadcasted_iota(jnp.int32, (tm, LANES), 1)
    lf = _forget_logs(hb, w_ref, bf_ref, lane)
    c = _exact_tri_dot(tri_ref[...], lf) + carry_ref[...]
    carry_ref[...] = c[tm - 1:tm, :]
    cmat = _bias_operand(c, lane)
    ones = _ones_rows(tm, tm)

    def zero_after(tile):
        bits = lax.bitcast_convert_type(tile, jnp.int32) & zero_ref[...]
        return lax.bitcast_convert_type(bits, F32)[0:1, :]

    def qk_heads(out_ref, col0, first_head, n_heads):
        cols = slice(col0 + first_head * LANES, col0 + (first_head + n_heads) * LANES)
        blk = _dot(hb, w_ref[:, cols]) + _dot(cmat, pm_ref[:, cols])
        for h in range(n_heads):
            out_ref[0, first_head + h] = blk[:, h * LANES:(h + 1) * LANES].astype(BF16)
        return zero_after(blk[tm - SUBLANES:, (n_heads - 1) * LANES:])

    def v_heads(first_head, n_heads):
        rows = slice(first_head * HEAD_DIM, (first_head + n_heads) * HEAD_DIM)
        vt = _dot_nt(wvt_ref[rows, :], hb)
        for h in range(n_heads):
            vt_ref[0, first_head + h, 0, 0:HEAD_DIM, :] = vt[h * HEAD_DIM:(h + 1) * HEAD_DIM, :].astype(BF16)
            vt_ref[0, first_head + h, 0, HEAD_DIM:V_ROWS, :] = ones
        return zero_after(vt[n_heads * HEAD_DIM - SUBLANES:, tm - LANES:])

    def attn_gate():
        ga = _dot(hb, w_ref[:, C_GA:C_U])
        ga_ref[0] = _silu(ga)
        return zero_after(ga[tm - SUBLANES:, D_ATTN - LANES:])

    half = N_HEADS // 2
    mxu_groups = [lambda: v_heads(0, half), lambda: v_heads(half, half), attn_gate,
                  lambda: qk_heads(q_ref, C_Q, 0, half), lambda: qk_heads(q_ref, C_Q, half, half),
                  lambda: qk_heads(k_ref, C_K, 0, half), lambda: qk_heads(k_ref, C_K, half, half)]
    chunks = list(range(0, tm, CONV_ROWS))
    assert len(chunks) == len(mxu_groups) + 1
    after = jnp.zeros((1, LANES), F32)
    for n, base in enumerate(chunks):
        _conv_rows(buf_ref, cw_ref, cb_ref, conv_ref, base, after)
        if n < len(mxu_groups):
            after = mxu_groups[n]()

    gate = _silu(_dot(hb, w_ref[:, C_GC:C_END]))
    z = _silu(_layer_norm(conv_ref[...], lcg_ref[...], lcb_ref[...]))
    yc_ref[0] = (_dot(z.astype(BF16), wpw_ref[...]) * gate).astype(BF16)


def _attn_body(q_ref, k_ref, vt_ref, km_ref, vmt_ref, o_ref, s_ref, pen_ref, m_ref, acc_ref):
    qi = pl.program_id(2)
    heads = range(HEADS_PER_STEP)

    @pl.when((pl.program_id(0) == 0) & (qi == 0))
    def _causal_penalty():
        key = lax.broadcasted_iota(jnp.int32, (TK, TQ), 0)
        qry = lax.broadcasted_iota(jnp.int32, (TK, TQ), 1)
        pen_ref[0] = jnp.zeros((TK, TQ), F32)
        pen_ref[1] = jnp.where(key <= qry, 0.0, -MASK_BIAS)

    pad = jnp.zeros((META_PAD - N_META, TQ), BF16)
    s_meta = [_dot_nt(km_ref[hh], q_ref[0, hh]) for hh in heads]
    m_meta = [jnp.max(s, axis=0, keepdims=True) for s in s_meta]
    p_meta = [jnp.concatenate([jnp.exp2(s - m).astype(BF16), pad], axis=0) for s, m in zip(s_meta, m_meta)]
    for hh in heads:
        m_ref[hh] = m_meta[hh]
        acc_ref[hh] = _dot(vmt_ref[hh], p_meta[hh])

    def produce(j, hh, penalty):
        off = pl.multiple_of(j * TK, TK)
        s = _dot_nt(k_ref[0, hh, pl.ds(off, TK), :], q_ref[0, hh])
        if penalty is not None:
            s = s + pen_ref[penalty]
        s_ref[hh % 2] = s
        return jnp.max(s, axis=0, keepdims=True)

    def consume(j, hh, col_max):
        m = m_ref[hh]
        m_new = jnp.maximum(m, col_max)
        p = jnp.exp2(s_ref[hh % 2] - m_new).astype(BF16)
        acc_ref[hh] = acc_ref[hh] * jnp.exp2(m - m_new) + _dot(vt_ref[0, hh, j], p)
        m_ref[hh] = m_new

    def diagonal_flag(j):
        return jnp.where(j < qi, 0, 1)

    def body(j, col_max):
        for hh in heads:
            if hh + 1 < HEADS_PER_STEP:
                nxt_max = produce(j, hh + 1, None)
            else:
                nxt_max = produce(j + 1, 0, diagonal_flag(j + 1))
            consume(j, hh, col_max)
            col_max = nxt_max
        return col_max

    col_max = lax.fori_loop(0, qi, body, produce(0, 0, diagonal_flag(0)))
    for hh in heads:
        nxt_max = produce(qi, hh + 1, 1) if hh + 1 < HEADS_PER_STEP else None
        consume(qi, hh, col_max)
        col_max = nxt_max
    outs = [acc_ref[hh, 0:HEAD_DIM, :] / acc_ref[hh, HEAD_DIM:HEAD_DIM + 1, :] for hh in heads]
    o_ref[0] = jnp.concatenate(outs, axis=0).T


def _out_body(x_ref, att_ref, ga_ref, yc_ref, wout_ref, lig_ref, lib_ref, log_ref, lob_ref, o_ref):
    y_attn = (att_ref[0] * ga_ref[0]).astype(BF16)
    y = _dot(y_attn, wout_ref[0:D_ATTN, :]) + _dot(yc_ref[0], wout_ref[D_ATTN:, :])
    h = _layer_norm(x_ref[0], lig_ref[...], lib_ref[...])
    o_ref[0] = _layer_norm(ALPHA * h + y, log_ref[...], lob_ref[...])


def _pad_heads(w):
    w = w.reshape(D_MODEL, N_HEADS, HEAD_DIM)
    w = jnp.pad(w, ((0, 0), (0, 0), (0, LANES - HEAD_DIM)))
    return w.reshape(D_MODEL, N_HEADS * LANES)


def _placement():
    pm = np.zeros((LANES, 2 * N_HEADS * LANES), np.float32)
    for h in range(N_HEADS):
        for g in range(3):
            pm[g * N_HEADS + h, C_Q + h * LANES + HEAD_DIM + g] = 1.0
            pm[3 * N_HEADS + h, C_Q + h * LANES + HEAD_DIM + 3 + g] = 1.0
            pm[3 * N_HEADS + h, C_K + h * LANES + HEAD_DIM + g] = 1.0
            pm[g * N_HEADS + h, C_K + h * LANES + HEAD_DIM + 3 + g] = -1.0
    return jnp.asarray(pm, BF16)


def _full(shape):
    return pl.BlockSpec(shape, lambda *_: (0,) * len(shape))


def _params(sem, vmem_mb):
    return pltpu.CompilerParams(dimension_semantics=sem, vmem_limit_bytes=vmem_mb * 1024 * 1024)


def kernel(x, meta, ln_in_g, ln_in_b, w_in, b_f, conv_w, conv_b, ln_conv_g, ln_conv_b,
           w_pw, w_out, ln_out_g, ln_out_b):
    B, S, D = x.shape
    assert D == D_MODEL and w_in.shape[0] == 1 and meta.shape == (N_META, D_MODEL)
    assert S % TM_PROJ == 0 and S % TM_OUT == 0 and S % TQ == 0 and TQ == TK == TM_PROJ

    w = w_in[0]
    scale = HEAD_DIM ** -0.5 * LOG2E
    wf = jnp.pad(jnp.tile(w[:, 3 * D_ATTN:3 * D_ATTN + N_HEADS], (1, 3)), ((0, 0), (0, LANES - 3 * N_HEADS)))
    w_all = jnp.concatenate([
        _pad_heads(w[:, 0:D_ATTN] * scale), _pad_heads(w[:, D_ATTN:2 * D_ATTN]),
        wf, w[:, 3 * D_ATTN + N_HEADS:]], axis=1).astype(BF16)
    wvt = w[:, 2 * D_ATTN:3 * D_ATTN].T.astype(BF16)
    bf_row = jnp.pad(jnp.tile(b_f[0], 3), (0, LANES - 3 * N_HEADS)).reshape(1, LANES)
    pm = _placement()
    lig = ln_in_g.reshape(1, D)
    lib = ln_in_b.reshape(1, D)
    meta_pad = jnp.pad(meta, ((0, META_PAD - N_META), (0, 0)))

    k_meta, vt_meta, glu_meta = pl.pallas_call(
        _meta_body,
        out_shape=(jax.ShapeDtypeStruct((N_HEADS, N_META, LANES), BF16),
                   jax.ShapeDtypeStruct((N_HEADS, V_ROWS, META_PAD), BF16),
                   jax.ShapeDtypeStruct((HALO, D_CONV), F32)),
        compiler_params=pltpu.CompilerParams(vmem_limit_bytes=48 * 1024 * 1024),
        name="meta_proj",
    )(meta_pad, lig, lib, w_all, wvt, bf_row, pm)

    nt = S // TM_PROJ
    head_blk = pl.BlockSpec((1, N_HEADS, TM_PROJ, LANES), lambda b, i: (b, 0, i, 0))
    row_blk = pl.BlockSpec((1, TM_PROJ, D_ATTN), lambda b, i: (b, i, 0))
    qp, kp, vtp, ga, yc = pl.pallas_call(
        _proj_body,
        grid=(B, nt),
        in_specs=[pl.BlockSpec((1, TM_PROJ, D), lambda b, i: (b, i, 0)),
                  _full((1, D)), _full((1, D)), _full((D, C_END)), _full((D_ATTN, D)), _full((1, LANES)),
                  _full((LANES, 2 * N_HEADS * LANES)), _full((HALO, D_CONV)),
                  _full((CONV_WIDTH, D_CONV)), _full((1, D_CONV)), _full((1, D_CONV)), _full((1, D_CONV)),
                  _full((D_CONV, D_CONV)), _full((SUBLANES, LANES))],
        out_specs=(head_blk, head_blk,
                   pl.BlockSpec((1, N_HEADS, 1, V_ROWS, TM_PROJ), lambda b, i: (b, 0, i, 0, 0)),
                   row_blk, row_blk),
        out_shape=(jax.ShapeDtypeStruct((B, N_HEADS, S, LANES), BF16),) * 2
                  + (jax.ShapeDtypeStruct((B, N_HEADS, nt, V_ROWS, TM_PROJ), BF16),
                     jax.ShapeDtypeStruct((B, S, D_ATTN), F32),
                     jax.ShapeDtypeStruct((B, S, D_CONV), BF16)),
        scratch_shapes=[pltpu.VMEM((1, LANES), F32), pltpu.VMEM((TM_PROJ, TM_PROJ), BF16),
                        pltpu.VMEM((HALO + TM_PROJ, D_CONV), F32), pltpu.VMEM((TM_PROJ, D_CONV), F32)],
        compiler_params=_params(("arbitrary", "arbitrary"), 56),
        name="in_proj",
    )(x, lig, lib, w_all, wvt, bf_row, pm, glu_meta,
      conv_w[0], conv_b[0].reshape(1, D_CONV), ln_conv_g[0].reshape(1, D_CONV), ln_conv_b[0].reshape(1, D_CONV),
      w_pw[0].astype(BF16), jnp.zeros((SUBLANES, LANES), jnp.int32))

    hp = HEADS_PER_STEP
    att = pl.pallas_call(
        _attn_body,
        grid=(B, N_HEADS // hp, S // TQ),
        in_specs=[pl.BlockSpec((1, hp, TQ, LANES), lambda b, h, i: (b, h, i, 0)),
                  pl.BlockSpec((1, hp, S, LANES), lambda b, h, i: (b, h, 0, 0)),
                  pl.BlockSpec((1, hp, nt, V_ROWS, TK), lambda b, h, i: (b, h, 0, 0, 0)),
                  pl.BlockSpec((hp, N_META, LANES), lambda b, h, i: (h, 0, 0)),
                  pl.BlockSpec((hp, V_ROWS, META_PAD), lambda b, h, i: (h, 0, 0))],
        out_specs=pl.BlockSpec((1, TQ, hp * HEAD_DIM), lambda b, h, i: (b, i, h)),
        out_shape=jax.ShapeDtypeStruct((B, S, D_ATTN), F32),
        scratch_shapes=[pltpu.VMEM((2, TK, TQ), F32), pltpu.VMEM((2, TK, TQ), F32),
                        pltpu.VMEM((hp, 1, TQ), F32), pltpu.VMEM((hp, V_ROWS, TQ), F32)],
        compiler_params=_params(("arbitrary", "arbitrary", "arbitrary"), 48),
        name="fox_attn",
    )(qp, kp, vtp, k_meta, vt_meta)

    tile_blk = pl.BlockSpec((1, TM_OUT, D_ATTN), lambda b, i: (b, i, 0))
    out = pl.pallas_call(
        _out_body,
        grid=(B, S // TM_OUT),
        in_specs=[pl.BlockSpec((1, TM_OUT, D), lambda b, i: (b, i, 0)), tile_blk, tile_blk, tile_blk,
                  _full((D, D)), _full((1, D)), _full((1, D)), _full((1, D)), _full((1, D))],
        out_specs=pl.BlockSpec((1, TM_OUT, D), lambda b, i: (b, i, 0)),
        out_shape=jax.ShapeDtypeStruct((B, S, D), F32),
        compiler_params=_params(("arbitrary", "arbitrary"), 32),
        name="out_proj",
    )(x, att, ga, yc, w_out[0].astype(BF16), lig, lib, ln_out_g[0].reshape(1, D), ln_out_b[0].reshape(1, D))
    return out
```

```python
import numpy as np

import jax
import jax.numpy as jnp
from jax import lax
from jax.experimental import pallas as pl
from jax.experimental.pallas import tpu as pltpu

F32 = jnp.float32
BF16 = jnp.bfloat16

D_MODEL = 1024
N_META = 16
D_ATTN = 512
D_CONV = 512
N_HEADS = 8
HEAD_DIM = 64
CONV_WIDTH = 31
LN_EPS = 1e-5
ALPHA = 2.0 ** 0.25
LANES = 128
SUBLANES = 8
BF16_ROWS = 16
META_PAD = 128
HALO = 32
MASK_BIAS = 1e30
LOG2E = 1.4426950408889634
V_ROWS = HEAD_DIM + BF16_ROWS

C_Q = 0
C_K = D_ATTN
C_F = 2 * D_ATTN
C_GA = C_F + LANES
C_U = C_GA + D_ATTN
C_UG = C_U + D_CONV
C_GC = C_UG + D_CONV
C_END = C_GC + D_CONV

TM_PROJ = 512
TQ = 512
TK = 512
HEADS_PER_STEP = 8
TM_OUT = 512
CONV_ROWS = 64


def _dot(a, b):
    return jnp.dot(a, b, preferred_element_type=F32)


def _dot_nt(a, b):
    return lax.dot_general(a, b, (((1,), (1,)), ((), ())), preferred_element_type=F32)


def _layer_norm(x, g, b):
    mu = jnp.mean(x, axis=-1, keepdims=True)
    xc = x - mu
    var = jnp.mean(xc * xc, axis=-1, keepdims=True)
    return xc * lax.rsqrt(var + LN_EPS) * g + b


def _sigmoid(x):
    return 1.0 / (1.0 + jnp.exp(-x))


def _silu(x):
    return x * _sigmoid(x)


def _log_sigmoid(x):
    return jnp.minimum(x, 0.0) - jnp.log1p(jnp.exp(-jnp.abs(x)))


def _split3(c):
    hi = c.astype(BF16).astype(F32)
    r = c - hi
    mid = r.astype(BF16).astype(F32)
    lo = (r - mid).astype(BF16).astype(F32)
    return hi, mid, lo


def _bias_operand(c, lane):
    hi, mid, lo = _split3(c * LOG2E)
    one = jnp.where(lane < 4 * N_HEADS, 1.0, 0.0)
    return jnp.where(lane < N_HEADS, hi,
                     jnp.where(lane < 2 * N_HEADS, mid,
                               jnp.where(lane < 3 * N_HEADS, lo, one))).astype(BF16)


def _forget_logs(hb, w_ref, bf_ref, lane):
    f = _dot(hb, w_ref[:, C_F:C_F + LANES])
    return jnp.where(lane < 3 * N_HEADS, _log_sigmoid(f + bf_ref[...]), 0.0)


def _exact_tri_dot(tri, x):
    hi, mid, lo = _split3(x)
    return _dot(tri, hi.astype(BF16)) + _dot(tri, mid.astype(BF16)) + _dot(tri, lo.astype(BF16))


def _head_tiles(proj, bias_cols, lane):
    low_half = lane < HEAD_DIM
    tiles = []
    for p in range(proj.shape[1] // LANES):
        pair = proj[:, p * LANES:(p + 1) * LANES]
        extra = bias_cols[:, p * LANES:(p + 1) * LANES]
        tiles.append(jnp.where(low_half, pair, extra).astype(BF16))
        tiles.append(pltpu.roll(jnp.where(low_half, extra, pair), HEAD_DIM, axis=1).astype(BF16))
    return tiles


def _ones_rows(cols, valid_cols):
    r = lax.broadcasted_iota(jnp.int32, (BF16_ROWS, cols), 0)
    c = lax.broadcasted_iota(jnp.int32, (BF16_ROWS, cols), 1)
    return jnp.where((r == 0) & (c < valid_cols), 1.0, 0.0).astype(BF16)


def _meta_body(x_ref, lng_ref, lnb_ref, w_ref, wvt_ref, bf_ref, pm_ref,
               k_ref, vt_ref, glu_ref):
    rows = x_ref.shape[0]
    hb = _layer_norm(x_ref[...], lng_ref[...], lnb_ref[...]).astype(BF16)
    lane = lax.broadcasted_iota(jnp.int32, (rows, LANES), 1)
    lf = _forget_logs(hb, w_ref, bf_ref, lane)
    r = lax.broadcasted_iota(jnp.int32, (rows, rows), 0)
    c = lax.broadcasted_iota(jnp.int32, (rows, rows), 1)
    upper = jnp.where((c > r) & (c < N_META), 1.0, 0.0).astype(BF16)
    cmat = _bias_operand(-_exact_tri_dot(upper, lf), lane)
    k_tiles = _head_tiles(_dot(hb, w_ref[:, C_K:C_F]), _dot(cmat, pm_ref[:, C_K:C_F]), lane)
    vt = _dot_nt(wvt_ref[...], hb)
    real = lax.broadcasted_iota(jnp.int32, (D_ATTN, rows), 1) < N_META
    vt = jnp.where(real, vt, 0.0).astype(BF16)
    ones = _ones_rows(rows, N_META)
    for h in range(N_HEADS):
        k_ref[h] = k_tiles[h][0:N_META, :]
        vt_ref[h, 0:HEAD_DIM, :] = vt[h * HEAD_DIM:(h + 1) * HEAD_DIM, :]
        vt_ref[h, HEAD_DIM:V_ROWS, :] = ones
    u = _dot(hb, w_ref[:, C_U:C_UG])
    ug = _dot(hb, w_ref[:, C_UG:C_GC])
    glu = u * _sigmoid(ug)
    glu_ref[0:HALO - N_META, :] = jnp.zeros((HALO - N_META, D_CONV), F32)
    glu_ref[HALO - N_META:HALO, :] = glu[0:N_META, :]


def _conv_rows(buf_ref, cw_ref, cb_ref, out_ref, base, zero):
    shift = HALO - (CONV_WIDTH - 1)
    for lo in range(0, D_CONV, LANES):
        cols = slice(lo, lo + LANES)
        out = jnp.broadcast_to(cb_ref[:, cols] + zero, (CONV_ROWS, LANES))
        for r in range(SUBLANES):
            window = CONV_ROWS + (SUBLANES if r else 0)
            part = None
            for j in range(CONV_WIDTH):
                if (shift + j) % SUBLANES != r:
                    continue
                start = base + shift + j - r
                term = cw_ref[j:j + 1, cols] * buf_ref[start:start + window, cols]
                part = term if part is None else part + term
            out = out + part[r:r + CONV_ROWS, :]
        out_ref[base:base + CONV_ROWS, cols] = out


def _proj_body(x_ref, lng_ref, lnb_ref, w_ref, wvt_ref, bf_ref, pm_ref, meta_glu_ref,
               cw_ref, cb_ref, lcg_ref, lcb_ref, wpw_ref, zero_ref,
               q_ref, k_ref, vt_ref, ga_ref, yc_ref,
               carry_ref, tri_ref, buf_ref, conv_ref):
    tm = x_ref.shape[1]
    first = pl.program_id(1) == 0

    @pl.when(first)
    def _start_sequence():
        carry_ref[...] = jnp.zeros_like(carry_ref)
        r = lax.broadcasted_iota(jnp.int32, (tm, tm), 0)
        c = lax.broadcasted_iota(jnp.int32, (tm, tm), 1)
        tri_ref[...] = jnp.where(r >= c, 1.0, 0.0).astype(BF16)
        buf_ref[0:HALO, :] = meta_glu_ref[...]

    @pl.when(jnp.logical_not(first))
    def _carry_conv_context():
        buf_ref[0:HALO, :] = buf_ref[tm:tm + HALO, :]

    hb = _layer_norm(x_ref[0], lng_ref[...], lnb_ref[...]).astype(BF16)
    u = _dot(hb, w_ref[:, C_U:C_UG])
    ug = _dot(hb, w_ref[:, C_UG:C_GC])
    buf_ref[HALO:HALO + tm, :] = u * _sigmoid(ug)

    lane = lax.broadcasted_iota(jnp.int32, (tm, LANES), 1)
    lf = _forget_logs(hb, w_ref, bf_ref, lane)
    c = _exact_tri_dot(tri_ref[...], lf) + carry_ref[...]
    carry_ref[...] = c[tm - 1:tm, :]
    cmat = _bias_operand(c, lane)
    ones = _ones_rows(tm, tm)

    def zero_after(tile):
        bits = lax.bitcast_convert_type(tile, jnp.int32) & zero_ref[...]
        return lax.bitcast_convert_type(bits, F32)[0:1, :]

    def qk_heads(out_ref, col0, first_pair, n_pairs):
        cols = slice(col0 + first_pair * LANES, col0 + (first_pair + n_pairs) * LANES)
        ext = _dot(cmat, pm_ref[:, cols])
        tiles = _head_tiles(_dot(hb, w_ref[:, cols]), ext, lane)
        for h, tile in enumerate(tiles):
            out_ref[0, 2 * first_pair + h] = tile
        return zero_after(ext[tm - SUBLANES:, (n_pairs - 1) * LANES:])

    def v_heads(first_head, n_heads):
        rows = slice(first_head * HEAD_DIM, (first_head + n_heads) * HEAD_DIM)
        vt = _dot_nt(wvt_ref[rows, :], hb)
        for h in range(n_heads):
            vt_ref[0, first_head + h, 0, 0:HEAD_DIM, :] = vt[h * HEAD_DIM:(h + 1) * HEAD_DIM, :].astype(BF16)
            vt_ref[0, first_head + h, 0, HEAD_DIM:V_ROWS, :] = ones
        return zero_after(vt[n_heads * HEAD_DIM - SUBLANES:, tm - LANES:])

    def attn_gate():
        ga = _dot(hb, w_ref[:, C_GA:C_U])
        ga_ref[0] = _silu(ga)
        return zero_after(ga[tm - SUBLANES:, D_ATTN - LANES:])

    half = N_HEADS // 2
    quarter = N_HEADS // 4
    mxu_groups = [lambda: v_heads(0, half), lambda: v_heads(half, half), attn_gate,
                  lambda: qk_heads(q_ref, C_Q, 0, quarter), lambda: qk_heads(q_ref, C_Q, quarter, quarter),
                  lambda: qk_heads(k_ref, C_K, 0, quarter), lambda: qk_heads(k_ref, C_K, quarter, quarter)]
    chunks = list(range(0, tm, CONV_ROWS))
    assert len(chunks) == len(mxu_groups) + 1
    after = jnp.zeros((1, LANES), F32)
    for n, base in enumerate(chunks):
        _conv_rows(buf_ref, cw_ref, cb_ref, conv_ref, base, after)
        if n < len(mxu_groups):
            after = mxu_groups[n]()

    gate = _silu(_dot(hb, w_ref[:, C_GC:C_END]))
    z = _silu(_layer_norm(conv_ref[...], lcg_ref[...], lcb_ref[...]))
    yc_ref[0] = (_dot(z.astype(BF16), wpw_ref[...]) * gate).astype(BF16)


def _attn_body(q_ref, k_ref, vt_ref, km_ref, vmt_ref, ga_ref, o_ref, s_ref, pen_ref, m_ref, acc_ref):
    qi = pl.program_id(2)
    heads = range(HEADS_PER_STEP)

    @pl.when((pl.program_id(0) == 0) & (qi == 0))
    def _causal_penalty():
        key = lax.broadcasted_iota(jnp.int32, (TK, TQ), 0)
        qry = lax.broadcasted_iota(jnp.int32, (TK, TQ), 1)
        pen_ref[0] = jnp.zeros((TK, TQ), F32)
        pen_ref[1] = jnp.where(key <= qry, 0.0, -MASK_BIAS)

    pad = jnp.zeros((META_PAD - N_META, TQ), BF16)
    s_meta = [_dot_nt(km_ref[hh], q_ref[0, hh]) for hh in heads]
    m_meta = [jnp.max(s, axis=0, keepdims=True) for s in s_meta]
    p_meta = [jnp.concatenate([jnp.exp2(s - m).astype(BF16), pad], axis=0) for s, m in zip(s_meta, m_meta)]
    for hh in heads:
        m_ref[hh] = m_meta[hh]
        acc_ref[hh] = _dot(vmt_ref[hh], p_meta[hh])

    def produce(j, hh, penalty):
        off = pl.multiple_of(j * TK, TK)
        s = _dot_nt(k_ref[0, hh, pl.ds(off, TK), :], q_ref[0, hh])
        if penalty is not None:
            s = s + pen_ref[penalty]
        s_ref[hh % 2] = s
        return jnp.max(s, axis=0, keepdims=True)

    def consume(j, hh, col_max):
        m = m_ref[hh]
        m_new = jnp.maximum(m, col_max)
        p = jnp.exp2(s_ref[hh % 2] - m_new).astype(BF16)
        acc_ref[hh] = acc_ref[hh] * jnp.exp2(m - m_new) + _dot(vt_ref[0, hh, j], p)
        m_ref[hh] = m_new

    def diagonal_flag(j):
        return jnp.where(j < qi, 0, 1)

    def body(j, col_max):
        for hh in heads:
            if hh + 1 < HEADS_PER_STEP:
                nxt_max = produce(j, hh + 1, None)
            else:
                nxt_max = produce(j + 1, 0, diagonal_flag(j + 1))
            consume(j, hh, col_max)
            col_max = nxt_max
        return col_max

    def normalized(hh):
        return acc_ref[hh, 0:HEAD_DIM, :] / acc_ref[hh, HEAD_DIM:HEAD_DIM + 1, :]

    col_max = lax.fori_loop(0, qi, body, produce(0, 0, diagonal_flag(0)))
    for hh in heads:
        nxt_max = produce(qi, hh + 1, 1) if hh + 1 < HEADS_PER_STEP else None
        consume(qi, hh, col_max)
        col_max = nxt_max
        if hh % 2:
            cols = slice((hh - 1) * HEAD_DIM, (hh + 1) * HEAD_DIM)
            pair = jnp.concatenate([normalized(hh - 1), normalized(hh)], axis=0).T
            o_ref[0, :, cols] = (pair * ga_ref[0, :, cols]).astype(BF16)


def _out_body(x_ref, ya_ref, yc_ref, wout_ref, lig_ref, lib_ref, log_ref, lob_ref, o_ref):
    y = _dot(ya_ref[0], wout_ref[0:D_ATTN, :]) + _dot(yc_ref[0], wout_ref[D_ATTN:, :])
    h = _layer_norm(x_ref[0], lig_ref[...], lib_ref[...])
    o_ref[0] = _layer_norm(ALPHA * h + y, log_ref[...], lob_ref[...])


def _placement():
    pm = np.zeros((LANES, 2 * D_ATTN), np.float32)
    for h in range(N_HEADS):
        base = (h // 2) * LANES + (HEAD_DIM if h % 2 == 0 else 0)
        for g in range(3):
            pm[g * N_HEADS + h, C_Q + base + g] = 1.0
            pm[3 * N_HEADS + h, C_Q + base + 3 + g] = 1.0
            pm[3 * N_HEADS + h, C_K + base + g] = 1.0
            pm[g * N_HEADS + h, C_K + base + 3 + g] = -1.0
    return jnp.asarray(pm, BF16)


def _full(shape):
    return pl.BlockSpec(shape, lambda *_: (0,) * len(shape))


def _params(sem, vmem_mb):
    return pltpu.CompilerParams(dimension_semantics=sem, vmem_limit_bytes=vmem_mb * 1024 * 1024)


def kernel(x, meta, ln_in_g, ln_in_b, w_in, b_f, conv_w, conv_b, ln_conv_g, ln_conv_b,
           w_pw, w_out, ln_out_g, ln_out_b):
    B, S, D = x.shape
    assert D == D_MODEL and w_in.shape[0] == 1 and meta.shape == (N_META, D_MODEL)
    assert S % TM_PROJ == 0 and S % TM_OUT == 0 and S % TQ == 0 and TQ == TK == TM_PROJ

    w = w_in[0]
    scale = HEAD_DIM ** -0.5 * LOG2E
    wf = jnp.pad(jnp.tile(w[:, 3 * D_ATTN:3 * D_ATTN + N_HEADS], (1, 3)), ((0, 0), (0, LANES - 3 * N_HEADS)))
    w_all = jnp.concatenate([
        w[:, 0:D_ATTN] * scale, w[:, D_ATTN:2 * D_ATTN], wf, w[:, 3 * D_ATTN + N_HEADS:]], axis=1).astype(BF16)
    wvt = w[:, 2 * D_ATTN:3 * D_ATTN].T.astype(BF16)
    bf_row = jnp.pad(jnp.tile(b_f[0], 3), (0, LANES - 3 * N_HEADS)).reshape(1, LANES)
    pm = _placement()
    lig = ln_in_g.reshape(1, D)
    lib = ln_in_b.reshape(1, D)
    meta_pad = jnp.pad(meta, ((0, META_PAD - N_META), (0, 0)))

    k_meta, vt_meta, glu_meta = pl.pallas_call(
        _meta_body,
        out_shape=(jax.ShapeDtypeStruct((N_HEADS, N_META, LANES), BF16),
                   jax.ShapeDtypeStruct((N_HEADS, V_ROWS, META_PAD), BF16),
                   jax.ShapeDtypeStruct((HALO, D_CONV), F32)),
        compiler_params=pltpu.CompilerParams(vmem_limit_bytes=48 * 1024 * 1024),
        name="meta_proj",
    )(meta_pad, lig, lib, w_all, wvt, bf_row, pm)

    nt = S // TM_PROJ
    head_blk = pl.BlockSpec((1, N_HEADS, TM_PROJ, LANES), lambda b, i: (b, 0, i, 0))
    row_blk = pl.BlockSpec((1, TM_PROJ, D_ATTN), lambda b, i: (b, i, 0))
    qp, kp, vtp, ga, yc = pl.pallas_call(
        _proj_body,
        grid=(B, nt),
        in_specs=[pl.BlockSpec((1, TM_PROJ, D), lambda b, i: (b, i, 0)),
                  _full((1, D)), _full((1, D)), _full((D, C_END)), _full((D_ATTN, D)), _full((1, LANES)),
                  _full((LANES, 2 * D_ATTN)), _full((HALO, D_CONV)),
                  _full((CONV_WIDTH, D_CONV)), _full((1, D_CONV)), _full((1, D_CONV)), _full((1, D_CONV)),
                  _full((D_CONV, D_CONV)), _full((SUBLANES, LANES))],
        out_specs=(head_blk, head_blk,
                   pl.BlockSpec((1, N_HEADS, 1, V_ROWS, TM_PROJ), lambda b, i: (b, 0, i, 0, 0)),
                   row_blk, row_blk),
        out_shape=(jax.ShapeDtypeStruct((B, N_HEADS, S, LANES), BF16),) * 2
                  + (jax.ShapeDtypeStruct((B, N_HEADS, nt, V_ROWS, TM_PROJ), BF16),
                     jax.ShapeDtypeStruct((B, S, D_ATTN), F32),
                     jax.ShapeDtypeStruct((B, S, D_CONV), BF16)),
        scratch_shapes=[pltpu.VMEM((1, LANES), F32), pltpu.VMEM((TM_PROJ, TM_PROJ), BF16),
                        pltpu.VMEM((HALO + TM_PROJ, D_CONV), F32), pltpu.VMEM((TM_PROJ, D_CONV), F32)],
        compiler_params=_params(("arbitrary", "arbitrary"), 56),
        name="in_proj",
    )(x, lig, lib, w_all, wvt, bf_row, pm, glu_meta,
      conv_w[0], conv_b[0].reshape(1, D_CONV), ln_conv_g[0].reshape(1, D_CONV), ln_conv_b[0].reshape(1, D_CONV),
      w_pw[0].astype(BF16), jnp.zeros((SUBLANES, LANES), jnp.int32))

    hp = HEADS_PER_STEP
    att = pl.pallas_call(
        _attn_body,
        grid=(B, N_HEADS // hp, S // TQ),
        in_specs=[pl.BlockSpec((1, hp, TQ, LANES), lambda b, h, i: (b, h, i, 0)),
                  pl.BlockSpec((1, hp, S, LANES), lambda b, h, i: (b, h, 0, 0)),
                  pl.BlockSpec((1, hp, nt, V_ROWS, TK), lambda b, h, i: (b, h, 0, 0, 0)),
                  pl.BlockSpec((hp, N_META, LANES), lambda b, h, i: (h, 0, 0)),
                  pl.BlockSpec((hp, V_ROWS, META_PAD), lambda b, h, i: (h, 0, 0)),
                  pl.BlockSpec((1, TQ, hp * HEAD_DIM), lambda b, h, i: (b, i, h))],
        out_specs=pl.BlockSpec((1, TQ, hp * HEAD_DIM), lambda b, h, i: (b, i, h)),
        out_shape=jax.ShapeDtypeStruct((B, S, D_ATTN), BF16),
        scratch_shapes=[pltpu.VMEM((2, TK, TQ), F32), pltpu.VMEM((2, TK, TQ), F32),
                        pltpu.VMEM((hp, 1, TQ), F32), pltpu.VMEM((hp, V_ROWS, TQ), F32)],
        compiler_params=_params(("arbitrary", "arbitrary", "arbitrary"), 48),
        name="fox_attn",
    )(qp, kp, vtp, k_meta, vt_meta, ga)

    tile_blk = pl.BlockSpec((1, TM_OUT, D_ATTN), lambda b, i: (b, i, 0))
    out = pl.pallas_call(
        _out_body,
        grid=(B, S // TM_OUT),
        in_specs=[pl.BlockSpec((1, TM_OUT, D), lambda b, i: (b, i, 0)), tile_blk, tile_blk,
                  _full((D, D)), _full((1, D)), _full((1, D)), _full((1, D)), _full((1, D))],
        out_specs=pl.BlockSpec((1, TM_OUT, D), lambda b, i: (b, i, 0)),
        out_shape=jax.ShapeDtypeStruct((B, S, D), F32),
        compiler_params=_params(("arbitrary", "arbitrary"), 32),
        name="out_proj",
    )(x, att, yc, w_out[0].astype(BF16), lig, lib, ln_out_g[0].reshape(1, D), ln_out_b[0].reshape(1, D))
    return out
```

```python
import numpy as np

import jax
import jax.numpy as jnp
from jax import lax
from jax.experimental import pallas as pl
from jax.experimental.pallas import tpu as pltpu

F32 = jnp.float32
BF16 = jnp.bfloat16

D_MODEL = 1024
N_META = 16
D_ATTN = 512
D_CONV = 512
N_HEADS = 8
HEAD_DIM = 64
CONV_WIDTH = 31
LN_EPS = 1e-5
ALPHA = 2.0 ** 0.25
LANES = 128
SUBLANES = 8
BF16_ROWS = 16
META_PAD = 128
HALO = 32
MASK_BIAS = 1e30
LOG2E = 1.4426950408889634
V_ROWS = HEAD_DIM + BF16_ROWS

C_Q = 0
C_K = D_ATTN
C_F = 2 * D_ATTN
C_GA = C_F + LANES
C_U = C_GA + D_ATTN
C_UG = C_U + D_CONV
C_GC = C_UG + D_CONV
C_END = C_GC + D_CONV

TM_PROJ = 512
TQ = 512
TK = 512
HEADS_PER_STEP = 8
LOOKAHEAD = 2
SCORE_SLOTS = 4
TM_OUT = 512
CONV_ROWS = 64


def _dot(a, b):
    return jnp.dot(a, b, preferred_element_type=F32)


def _dot_nt(a, b):
    return lax.dot_general(a, b, (((1,), (1,)), ((), ())), preferred_element_type=F32)


def _layer_norm(x, g, b):
    mu = jnp.mean(x, axis=-1, keepdims=True)
    xc = x - mu
    var = jnp.mean(xc * xc, axis=-1, keepdims=True)
    return xc * lax.rsqrt(var + LN_EPS) * g + b


def _sigmoid(x):
    return 1.0 / (1.0 + jnp.exp(-x))


def _silu(x):
    return x * _sigmoid(x)


def _log_sigmoid(x):
    return jnp.minimum(x, 0.0) - jnp.log1p(jnp.exp(-jnp.abs(x)))


def _split3(c):
    hi = c.astype(BF16).astype(F32)
    r = c - hi
    mid = r.astype(BF16).astype(F32)
    lo = (r - mid).astype(BF16).astype(F32)
    return hi, mid, lo


def _bias_operand(c, lane):
    hi, mid, lo = _split3(c * LOG2E)
    one = jnp.where(lane < 4 * N_HEADS, 1.0, 0.0)
    return jnp.where(lane < N_HEADS, hi,
                     jnp.where(lane < 2 * N_HEADS, mid,
                               jnp.where(lane < 3 * N_HEADS, lo, one))).astype(BF16)


def _forget_logs(hb, w_ref, bf_ref, lane):
    f = _dot(hb, w_ref[:, C_F:C_F + LANES])
    return jnp.where(lane < 3 * N_HEADS, _log_sigmoid(f + bf_ref[...]), 0.0)


def _exact_tri_dot(tri, x):
    hi, mid, lo = _split3(x)
    return _dot(tri, hi.astype(BF16)) + _dot(tri, mid.astype(BF16)) + _dot(tri, lo.astype(BF16))


def _head_tiles(proj, bias_cols, lane):
    low_half = lane < HEAD_DIM
    tiles = []
    for p in range(proj.shape[1] // LANES):
        pair = proj[:, p * LANES:(p + 1) * LANES]
        extra = bias_cols[:, p * LANES:(p + 1) * LANES]
        tiles.append(jnp.where(low_half, pair, extra).astype(BF16))
        tiles.append(pltpu.roll(jnp.where(low_half, extra, pair), HEAD_DIM, axis=1).astype(BF16))
    return tiles


def _ones_rows(cols, valid_cols):
    r = lax.broadcasted_iota(jnp.int32, (BF16_ROWS, cols), 0)
    c = lax.broadcasted_iota(jnp.int32, (BF16_ROWS, cols), 1)
    return jnp.where((r == 0) & (c < valid_cols), 1.0, 0.0).astype(BF16)


def _meta_body(x_ref, lng_ref, lnb_ref, w_ref, wvt_ref, bf_ref, pm_ref,
               k_ref, vt_ref, glu_ref):
    rows = x_ref.shape[0]
    hb = _layer_norm(x_ref[...], lng_ref[...], lnb_ref[...]).astype(BF16)
    lane = lax.broadcasted_iota(jnp.int32, (rows, LANES), 1)
    lf = _forget_logs(hb, w_ref, bf_ref, lane)
    r = lax.broadcasted_iota(jnp.int32, (rows, rows), 0)
    c = lax.broadcasted_iota(jnp.int32, (rows, rows), 1)
    upper = jnp.where((c > r) & (c < N_META), 1.0, 0.0).astype(BF16)
    cmat = _bias_operand(-_exact_tri_dot(upper, lf), lane)
    k_tiles = _head_tiles(_dot(hb, w_ref[:, C_K:C_F]), _dot(cmat, pm_ref[:, C_K:C_F]), lane)
    vt = _dot_nt(wvt_ref[...], hb)
    real = lax.broadcasted_iota(jnp.int32, (D_ATTN, rows), 1) < N_META
    vt = jnp.where(real, vt, 0.0).astype(BF16)
    ones = _ones_rows(rows, N_META)
    for h in range(N_HEADS):
        k_ref[h] = k_tiles[h][0:N_META, :]
        vt_ref[h, 0:HEAD_DIM, :] = vt[h * HEAD_DIM:(h + 1) * HEAD_DIM, :]
        vt_ref[h, HEAD_DIM:V_ROWS, :] = ones
    u = _dot(hb, w_ref[:, C_U:C_UG])
    ug = _dot(hb, w_ref[:, C_UG:C_GC])
    glu = u * _sigmoid(ug)
    glu_ref[0:HALO - N_META, :] = jnp.zeros((HALO - N_META, D_CONV), F32)
    glu_ref[HALO - N_META:HALO, :] = glu[0:N_META, :]


def _conv_rows(buf_ref, cw_ref, cb_ref, out_ref, base, zero):
    shift = HALO - (CONV_WIDTH - 1)
    for lo in range(0, D_CONV, LANES):
        cols = slice(lo, lo + LANES)
        out = jnp.broadcast_to(cb_ref[:, cols] + zero, (CONV_ROWS, LANES))
        for r in range(SUBLANES):
            window = CONV_ROWS + (SUBLANES if r else 0)
            part = None
            for j in range(CONV_WIDTH):
                if (shift + j) % SUBLANES != r:
                    continue
                start = base + shift + j - r
                term = cw_ref[j:j + 1, cols] * buf_ref[start:start + window, cols]
                part = term if part is None else part + term
            out = out + part[r:r + CONV_ROWS, :]
        out_ref[base:base + CONV_ROWS, cols] = out


def _proj_body(x_ref, lng_ref, lnb_ref, w_ref, wvt_ref, bf_ref, pm_ref, meta_glu_ref,
               cw_ref, cb_ref, lcg_ref, lcb_ref, wpw_ref, zero_ref,
               q_ref, k_ref, vt_ref, ga_ref, yc_ref,
               carry_ref, tri_ref, buf_ref, conv_ref):
    tm = x_ref.shape[1]
    first = pl.program_id(1) == 0

    @pl.when(first)
    def _start_sequence():
        carry_ref[...] = jnp.zeros_like(carry_ref)
        r = lax.broadcasted_iota(jnp.int32, (tm, tm), 0)
        c = lax.broadcasted_iota(jnp.int32, (tm, tm), 1)
        tri_ref[...] = jnp.where(r >= c, 1.0, 0.0).astype(BF16)
        buf_ref[0:HALO, :] = meta_glu_ref[...]

    @pl.when(jnp.logical_not(first))
    def _carry_conv_context():
        buf_ref[0:HALO, :] = buf_ref[tm:tm + HALO, :]

    def zero_after(tile):
        bits = lax.bitcast_convert_type(tile, jnp.int32) & zero_ref[...]
        return lax.bitcast_convert_type(bits, F32)[0:1, :]

    hb = _layer_norm(x_ref[0], lng_ref[...], lnb_ref[...]).astype(BF16)
    u = _dot(hb, w_ref[:, C_U:C_UG])
    ug = _dot(hb, w_ref[:, C_UG:C_GC])
    buf_ref[HALO:HALO + tm, :] = u * _sigmoid(ug)

    lane = lax.broadcasted_iota(jnp.int32, (tm, LANES), 1)
    lf = _forget_logs(hb, w_ref, bf_ref, lane)
    c = _exact_tri_dot(tri_ref[...], lf) + carry_ref[...]
    carry_ref[...] = c[tm - 1:tm, :]
    cmat = _bias_operand(c, lane)
    ones = _ones_rows(tm, tm)

    def qk_heads(out_ref, col0, first_pair, n_pairs):
        cols = slice(col0 + first_pair * LANES, col0 + (first_pair + n_pairs) * LANES)
        ext = _dot(cmat, pm_ref[:, cols])
        tiles = _head_tiles(_dot(hb, w_ref[:, cols]), ext, lane)
        for h, tile in enumerate(tiles):
            out_ref[0, 2 * first_pair + h] = tile
        return zero_after(ext[tm - SUBLANES:, (n_pairs - 1) * LANES:])

    def v_heads(first_head, n_heads):
        rows = slice(first_head * HEAD_DIM, (first_head + n_heads) * HEAD_DIM)
        vt = _dot_nt(wvt_ref[rows, :], hb)
        for h in range(n_heads):
            vt_ref[0, first_head + h, 0, 0:HEAD_DIM, :] = vt[h * HEAD_DIM:(h + 1) * HEAD_DIM, :].astype(BF16)
            vt_ref[0, first_head + h, 0, HEAD_DIM:V_ROWS, :] = ones
        return zero_after(vt[n_heads * HEAD_DIM - SUBLANES:, tm - LANES:])

    def attn_gate():
        ga = _dot(hb, w_ref[:, C_GA:C_U])
        ga_ref[0] = _silu(ga)
        return zero_after(ga[tm - SUBLANES:, D_ATTN - LANES:])

    half = N_HEADS // 2
    quarter = N_HEADS // 4
    mxu_groups = [lambda: v_heads(0, half), lambda: v_heads(half, half), attn_gate,
                  lambda: qk_heads(q_ref, C_Q, 0, quarter), lambda: qk_heads(q_ref, C_Q, quarter, quarter),
                  lambda: qk_heads(k_ref, C_K, 0, quarter), lambda: qk_heads(k_ref, C_K, quarter, quarter)]
    chunks = list(range(0, tm, CONV_ROWS))
    per_chunk = -(-len(mxu_groups) // len(chunks))
    after = jnp.zeros((1, LANES), F32)
    for n, base in enumerate(chunks):
        _conv_rows(buf_ref, cw_ref, cb_ref, conv_ref, base, after)
        for group in mxu_groups[n * per_chunk:(n + 1) * per_chunk]:
            after = group()

    gate = _silu(_dot(hb, w_ref[:, C_GC:C_END]))
    z = _silu(_layer_norm(conv_ref[...], lcg_ref[...], lcb_ref[...]))
    yc_ref[0] = (_dot(z.astype(BF16), wpw_ref[...]) * gate).astype(BF16)


def _attn_body(q_ref, k_ref, vt_ref, km_ref, vmt_ref, ga_ref, o_ref, s_ref, pen_ref, m_ref, acc_ref):
    qi = pl.program_id(2)
    heads = range(HEADS_PER_STEP)

    @pl.when((pl.program_id(0) == 0) & (qi == 0))
    def _causal_penalty():
        key = lax.broadcasted_iota(jnp.int32, (TK, TQ), 0)
        qry = lax.broadcasted_iota(jnp.int32, (TK, TQ), 1)
        pen_ref[0] = jnp.zeros((TK, TQ), F32)
        pen_ref[1] = jnp.where(key <= qry, 0.0, -MASK_BIAS)

    pad = jnp.zeros((META_PAD - N_META, TQ), BF16)
    s_meta = [_dot_nt(km_ref[hh], q_ref[0, hh]) for hh in heads]
    m_meta = [jnp.max(s, axis=0, keepdims=True) for s in s_meta]
    p_meta = [jnp.concatenate([jnp.exp2(s - m).astype(BF16), pad], axis=0) for s, m in zip(s_meta, m_meta)]
    for hh in heads:
        m_ref[hh] = m_meta[hh]
        acc_ref[hh] = _dot(vmt_ref[hh], p_meta[hh])

    def produce(j, hh, penalty):
        off = pl.multiple_of(j * TK, TK)
        s = _dot_nt(k_ref[0, hh, pl.ds(off, TK), :], q_ref[0, hh])
        if penalty is not None:
            s = s + pen_ref[penalty]
        s_ref[hh % SCORE_SLOTS] = s
        return jnp.max(s, axis=0, keepdims=True)

    def consume(j, hh, col_max):
        m = m_ref[hh]
        m_new = jnp.maximum(m, col_max)
        p = jnp.exp2(s_ref[hh % SCORE_SLOTS] - m_new).astype(BF16)
        acc_ref[hh] = acc_ref[hh] * jnp.exp2(m - m_new) + _dot(vt_ref[0, hh, j], p)
        m_ref[hh] = m_new

    def diagonal_flag(j):
        return jnp.where(j < qi, 0, 1)

    def body(j, pending):
        pending = list(pending)
        for hh in heads:
            ahead = hh + LOOKAHEAD
            if ahead < HEADS_PER_STEP:
                pending.append(produce(j, ahead, None))
            else:
                pending.append(produce(j + 1, ahead - HEADS_PER_STEP, diagonal_flag(j + 1)))
            consume(j, hh, pending.pop(0))
        return tuple(pending)

    half = TK // 2

    def produce_diagonal(hh):
        slot = hh % SCORE_SLOTS
        off = pl.multiple_of(qi * TK, TK)
        top = _dot_nt(k_ref[0, hh, pl.ds(off, half), :], q_ref[0, hh]) + pen_ref[1, 0:half, :]
        low = (_dot_nt(k_ref[0, hh, pl.ds(off + half, half), :], q_ref[0, hh, half:, :])
               + pen_ref[1, half:, half:])
        s_ref[slot, 0:half, :] = top
        s_ref[slot, half:, half:] = low
        top_max = jnp.max(top, axis=0, keepdims=True)
        low_max = jnp.max(low, axis=0, keepdims=True)
        return jnp.concatenate([top_max[:, 0:half], jnp.maximum(top_max[:, half:], low_max)], axis=1)

    def consume_diagonal(hh, col_max):
        slot = hh % SCORE_SLOTS
        m = m_ref[hh]
        m_new = jnp.maximum(m, col_max)
        p_top = jnp.exp2(s_ref[slot, 0:half, :] - m_new).astype(BF16)
        p_low = jnp.exp2(s_ref[slot, half:, half:] - m_new[:, half:]).astype(BF16)
        acc = acc_ref[hh] * jnp.exp2(m - m_new) + _dot(vt_ref[0, hh, qi, :, 0:half], p_top)
        acc_ref[hh, :, 0:half] = acc[:, 0:half]
        acc_ref[hh, :, half:] = acc[:, half:] + _dot(vt_ref[0, hh, qi, :, half:], p_low)
        m_ref[hh] = m_new

    def normalized(hh):
        return acc_ref[hh, 0:HEAD_DIM, :] / acc_ref[hh, HEAD_DIM:HEAD_DIM + 1, :]

    pending = lax.fori_loop(0, qi, body, tuple(produce(0, hh, diagonal_flag(0)) for hh in range(LOOKAHEAD)))
    pending = list(pending)
    for hh in heads:
        if hh + LOOKAHEAD < HEADS_PER_STEP:
            pending.append(produce_diagonal(hh + LOOKAHEAD))
        if hh < LOOKAHEAD:
            consume(qi, hh, pending.pop(0))
        else:
            consume_diagonal(hh, pending.pop(0))
        if hh % 2:
            cols = slice((hh - 1) * HEAD_DIM, (hh + 1) * HEAD_DIM)
            pair = jnp.concatenate([normalized(hh - 1), normalized(hh)], axis=0).T
            o_ref[0, :, cols] = (pair * ga_ref[0, :, cols]).astype(BF16)


def _out_body(x_ref, ya_ref, yc_ref, wout_ref, lig_ref, lib_ref, log_ref, lob_ref, o_ref):
    y = _dot(ya_ref[0], wout_ref[0:D_ATTN, :]) + _dot(yc_ref[0], wout_ref[D_ATTN:, :])
    h = _layer_norm(x_ref[0], lig_ref[...], lib_ref[...])
    o_ref[0] = _layer_norm(ALPHA * h + y, log_ref[...], lob_ref[...])


def _placement():
    pm = np.zeros((LANES, 2 * D_ATTN), np.float32)
    for h in range(N_HEADS):
        base = (h // 2) * LANES + (HEAD_DIM if h % 2 == 0 else 0)
        for g in range(3):
            pm[g * N_HEADS + h, C_Q + base + g] = 1.0
            pm[3 * N_HEADS + h, C_Q + base + 3 + g] = 1.0
            pm[3 * N_HEADS + h, C_K + base + g] = 1.0
            pm[g * N_HEADS + h, C_K + base + 3 + g] = -1.0
    return jnp.asarray(pm, BF16)


def _full(shape):
    return pl.BlockSpec(shape, lambda *_: (0,) * len(shape))


def _params(sem, vmem_mb):
    return pltpu.CompilerParams(dimension_semantics=sem, vmem_limit_bytes=vmem_mb * 1024 * 1024)


def kernel(x, meta, ln_in_g, ln_in_b, w_in, b_f, conv_w, conv_b, ln_conv_g, ln_conv_b,
           w_pw, w_out, ln_out_g, ln_out_b):
    B, S, D = x.shape
    assert D == D_MODEL and w_in.shape[0] == 1 and meta.shape == (N_META, D_MODEL)
    assert S % TM_PROJ == 0 and S % TM_OUT == 0 and S % TQ == 0 and TQ == TK == TM_PROJ

    w = w_in[0]
    scale = HEAD_DIM ** -0.5 * LOG2E
    wf = jnp.pad(jnp.tile(w[:, 3 * D_ATTN:3 * D_ATTN + N_HEADS], (1, 3)), ((0, 0), (0, LANES - 3 * N_HEADS)))
    w_all = jnp.concatenate([
        w[:, 0:D_ATTN] * scale, w[:, D_ATTN:2 * D_ATTN], wf, w[:, 3 * D_ATTN + N_HEADS:]], axis=1).astype(BF16)
    wvt = w[:, 2 * D_ATTN:3 * D_ATTN].T.astype(BF16)
    bf_row = jnp.pad(jnp.tile(b_f[0], 3), (0, LANES - 3 * N_HEADS)).reshape(1, LANES)
    pm = _placement()
    lig = ln_in_g.reshape(1, D)
    lib = ln_in_b.reshape(1, D)
    meta_pad = jnp.pad(meta, ((0, META_PAD - N_META), (0, 0)))

    k_meta, vt_meta, glu_meta = pl.pallas_call(
        _meta_body,
        out_shape=(jax.ShapeDtypeStruct((N_HEADS, N_META, LANES), BF16),
                   jax.ShapeDtypeStruct((N_HEADS, V_ROWS, META_PAD), BF16),
                   jax.ShapeDtypeStruct((HALO, D_CONV), F32)),
        compiler_params=pltpu.CompilerParams(vmem_limit_bytes=48 * 1024 * 1024),
        name="meta_proj",
    )(meta_pad, lig, lib, w_all, wvt, bf_row, pm)

    nt = S // TM_PROJ
    head_blk = pl.BlockSpec((1, N_HEADS, TM_PROJ, LANES), lambda b, i: (b, 0, i, 0))
    row_blk = pl.BlockSpec((1, TM_PROJ, D_ATTN), lambda b, i: (b, i, 0))
    qp, kp, vtp, ga, yc = pl.pallas_call(
        _proj_body,
        grid=(B, nt),
        in_specs=[pl.BlockSpec((1, TM_PROJ, D), lambda b, i: (b, i, 0)),
                  _full((1, D)), _full((1, D)), _full((D, C_END)), _full((D_ATTN, D)), _full((1, LANES)),
                  _full((LANES, 2 * D_ATTN)), _full((HALO, D_CONV)),
                  _full((CONV_WIDTH, D_CONV)), _full((1, D_CONV)), _full((1, D_CONV)), _full((1, D_CONV)),
                  _full((D_CONV, D_CONV)), _full((SUBLANES, LANES))],
        out_specs=(head_blk, head_blk,
                   pl.BlockSpec((1, N_HEADS, 1, V_ROWS, TM_PROJ), lambda b, i: (b, 0, i, 0, 0)),
                   row_blk, row_blk),
        out_shape=(jax.ShapeDtypeStruct((B, N_HEADS, S, LANES), BF16),) * 2
                  + (jax.ShapeDtypeStruct((B, N_HEADS, nt, V_ROWS, TM_PROJ), BF16),
                     jax.ShapeDtypeStruct((B, S, D_ATTN), F32),
                     jax.ShapeDtypeStruct((B, S, D_CONV), BF16)),
        scratch_shapes=[pltpu.VMEM((1, LANES), F32), pltpu.VMEM((TM_PROJ, TM_PROJ), BF16),
                        pltpu.VMEM((HALO + TM_PROJ, D_CONV), F32), pltpu.VMEM((TM_PROJ, D_CONV), F32)],
        compiler_params=_params(("arbitrary", "arbitrary"), 56),
        name="in_proj",
    )(x, lig, lib, w_all, wvt, bf_row, pm, glu_meta,
      conv_w[0], conv_b[0].reshape(1, D_CONV), ln_conv_g[0].reshape(1, D_CONV), ln_conv_b[0].reshape(1, D_CONV),
      w_pw[0].astype(BF16), jnp.zeros((SUBLANES, LANES), jnp.int32))

    hp = HEADS_PER_STEP
    att = pl.pallas_call(
        _attn_body,
        grid=(B, N_HEADS // hp, S // TQ),
        in_specs=[pl.BlockSpec((1, hp, TQ, LANES), lambda b, h, i: (b, h, i, 0)),
                  pl.BlockSpec((1, hp, S, LANES), lambda b, h, i: (b, h, 0, 0)),
                  pl.BlockSpec((1, hp, nt, V_ROWS, TK), lambda b, h, i: (b, h, 0, 0, 0)),
                  pl.BlockSpec((hp, N_META, LANES), lambda b, h, i: (h, 0, 0)),
                  pl.BlockSpec((hp, V_ROWS, META_PAD), lambda b, h, i: (h, 0, 0)),
                  pl.BlockSpec((1, TQ, hp * HEAD_DIM), lambda b, h, i: (b, i, h))],
        out_specs=pl.BlockSpec((1, TQ, hp * HEAD_DIM), lambda b, h, i: (b, i, h)),
        out_shape=jax.ShapeDtypeStruct((B, S, D_ATTN), BF16),
        scratch_shapes=[pltpu.VMEM((SCORE_SLOTS, TK, TQ), F32), pltpu.VMEM((2, TK, TQ), F32),
                        pltpu.VMEM((hp, 1, TQ), F32), pltpu.VMEM((hp, V_ROWS, TQ), F32)],
        compiler_params=_params(("arbitrary", "arbitrary", "arbitrary"), 48),
        name="fox_attn",
    )(qp, kp, vtp, k_meta, vt_meta, ga)

    tile_blk = pl.BlockSpec((1, TM_OUT, D_ATTN), lambda b, i: (b, i, 0))
    out = pl.pallas_call(
        _out_body,
        grid=(B, S // TM_OUT),
        in_specs=[pl.BlockSpec((1, TM_OUT, D), lambda b, i: (b, i, 0)), tile_blk, tile_blk,
                  _full((D, D)), _full((1, D)), _full((1, D)), _full((1, D)), _full((1, D))],
        out_specs=pl.BlockSpec((1, TM_OUT, D), lambda b, i: (b, i, 0)),
        out_shape=jax.ShapeDtypeStruct((B, S, D), F32),
        compiler_params=_params(("arbitrary", "arbitrary"), 32),
        name="out_proj",
    )(x, att, yc, w_out[0].astype(BF16), lig, lib, ln_out_g[0].reshape(1, D), ln_out_b[0].reshape(1, D))
    return out
```

```python
import numpy as np

import jax
import jax.numpy as jnp
from jax import lax
from jax.experimental import pallas as pl
from jax.experimental.pallas import tpu as pltpu

F32 = jnp.float32
BF16 = jnp.bfloat16

D_MODEL = 1024
N_META = 16
D_ATTN = 512
D_CONV = 512
N_HEADS = 8
HEAD_DIM = 64
CONV_WIDTH = 31
LN_EPS = 1e-5
ALPHA = 2.0 ** 0.25
LANES = 128
SUBLANES = 8
BF16_ROWS = 16
META_PAD = 128
HALO = 32
MASK_BIAS = 1e30
LOG2E = 1.4426950408889634
V_ROWS = HEAD_DIM + BF16_ROWS

C_Q = 0
C_K = D_ATTN
C_F = 2 * D_ATTN
C_GA = C_F + LANES
C_U = C_GA + D_ATTN
C_UG = C_U + D_CONV
C_GC = C_UG + D_CONV
C_END = C_GC + D_CONV

TM_PROJ = 512
TQ = 512
TK = 512
HEADS_PER_STEP = 8
LOOKAHEAD = 2
SCORE_SLOTS = 4
TM_OUT = 512
CONV_ROWS = 64


def _dot(a, b):
    return jnp.dot(a, b, preferred_element_type=F32)


def _dot_nt(a, b):
    return lax.dot_general(a, b, (((1,), (1,)), ((), ())), preferred_element_type=F32)


def _layer_norm(x, g, b):
    mu = jnp.mean(x, axis=-1, keepdims=True)
    xc = x - mu
    var = jnp.mean(xc * xc, axis=-1, keepdims=True)
    return xc * lax.rsqrt(var + LN_EPS) * g + b


def _sigmoid(x):
    return 1.0 / (1.0 + jnp.exp(-x))


def _silu(x):
    return x * _sigmoid(x)


def _log_sigmoid(x):
    return jnp.minimum(x, 0.0) - jnp.log1p(jnp.exp(-jnp.abs(x)))


def _split3(c):
    hi = c.astype(BF16).astype(F32)
    r = c - hi
    mid = r.astype(BF16).astype(F32)
    lo = (r - mid).astype(BF16).astype(F32)
    return hi, mid, lo


def _bias_operand(c, lane):
    hi, mid, lo = _split3(c * LOG2E)
    one = jnp.where(lane < 4 * N_HEADS, 1.0, 0.0)
    return jnp.where(lane < N_HEADS, hi,
                     jnp.where(lane < 2 * N_HEADS, mid,
                               jnp.where(lane < 3 * N_HEADS, lo, one))).astype(BF16)


def _forget_logs(hb, w_ref, bf_ref, lane):
    f = _dot(hb, w_ref[:, C_F:C_F + LANES])
    return jnp.where(lane < 3 * N_HEADS, _log_sigmoid(f + bf_ref[...]), 0.0)


def _exact_tri_dot(tri, x):
    hi, mid, lo = _split3(x)
    return _dot(tri, hi.astype(BF16)) + _dot(tri, mid.astype(BF16)) + _dot(tri, lo.astype(BF16))


def _head_tiles(proj, bias_cols, lane):
    low_half = lane < HEAD_DIM
    tiles = []
    for p in range(proj.shape[1] // LANES):
        pair = proj[:, p * LANES:(p + 1) * LANES]
        extra = bias_cols[:, p * LANES:(p + 1) * LANES]
        tiles.append(jnp.where(low_half, pair, extra).astype(BF16))
        tiles.append(pltpu.roll(jnp.where(low_half, extra, pair), HEAD_DIM, axis=1).astype(BF16))
    return tiles


def _ones_rows(cols, valid_cols):
    r = lax.broadcasted_iota(jnp.int32, (BF16_ROWS, cols), 0)
    c = lax.broadcasted_iota(jnp.int32, (BF16_ROWS, cols), 1)
    return jnp.where((r == 0) & (c < valid_cols), 1.0, 0.0).astype(BF16)


def _meta_body(x_ref, lng_ref, lnb_ref, w_ref, wvt_ref, bf_ref, pm_ref,
               k_ref, vt_ref, glu_ref):
    rows = x_ref.shape[0]
    hb = _layer_norm(x_ref[...], lng_ref[...], lnb_ref[...]).astype(BF16)
    lane = lax.broadcasted_iota(jnp.int32, (rows, LANES), 1)
    lf = _forget_logs(hb, w_ref, bf_ref, lane)
    r = lax.broadcasted_iota(jnp.int32, (rows, rows), 0)
    c = lax.broadcasted_iota(jnp.int32, (rows, rows), 1)
    upper = jnp.where((c > r) & (c < N_META), 1.0, 0.0).astype(BF16)
    cmat = _bias_operand(-_exact_tri_dot(upper, lf), lane)
    k_tiles = _head_tiles(_dot(hb, w_ref[:, C_K:C_F]), _dot(cmat, pm_ref[:, C_K:C_F]), lane)
    vt = _dot_nt(wvt_ref[...], hb)
    real = lax.broadcasted_iota(jnp.int32, (D_ATTN, rows), 1) < N_META
    vt = jnp.where(real, vt, 0.0).astype(BF16)
    ones = _ones_rows(rows, N_META)
    for h in range(N_HEADS):
        k_ref[h] = k_tiles[h][0:N_META, :]
        vt_ref[h, 0:HEAD_DIM, :] = vt[h * HEAD_DIM:(h + 1) * HEAD_DIM, :]
        vt_ref[h, HEAD_DIM:V_ROWS, :] = ones
    u = _dot(hb, w_ref[:, C_U:C_UG])
    ug = _dot(hb, w_ref[:, C_UG:C_GC])
    glu = u * _sigmoid(ug)
    glu_ref[0:HALO - N_META, :] = jnp.zeros((HALO - N_META, D_CONV), F32)
    glu_ref[HALO - N_META:HALO, :] = glu[0:N_META, :]


def _conv_rows(buf_ref, cw_ref, cb_ref, out_ref, base, zero):
    shift = HALO - (CONV_WIDTH - 1)
    for lo in range(0, D_CONV, LANES):
        cols = slice(lo, lo + LANES)
        out = jnp.broadcast_to(cb_ref[:, cols] + zero, (CONV_ROWS, LANES))
        for r in range(SUBLANES):
            window = CONV_ROWS + (SUBLANES if r else 0)
            part = None
            for j in range(CONV_WIDTH):
                if (shift + j) % SUBLANES != r:
                    continue
                start = base + shift + j - r
                term = cw_ref[j:j + 1, cols] * buf_ref[start:start + window, cols]
                part = term if part is None else part + term
            out = out + part[r:r + CONV_ROWS, :]
        out_ref[base:base + CONV_ROWS, cols] = out


def _proj_body(x_ref, lng_ref, lnb_ref, w_ref, wvt_ref, bf_ref, pm_ref, meta_glu_ref,
               cw_ref, cb_ref, lcg_ref, lcb_ref, wpw_ref, zero_ref,
               q_ref, k_ref, vt_ref, ga_ref, yc_ref,
               carry_ref, tri_ref, buf_ref, conv_ref):
    tm = x_ref.shape[1]
    first = pl.program_id(1) == 0

    @pl.when(first)
    def _start_sequence():
        carry_ref[...] = jnp.zeros_like(carry_ref)
        r = lax.broadcasted_iota(jnp.int32, (tm, tm), 0)
        c = lax.broadcasted_iota(jnp.int32, (tm, tm), 1)
        tri_ref[...] = jnp.where(r >= c, 1.0, 0.0).astype(BF16)
        buf_ref[0:HALO, :] = meta_glu_ref[...]

    @pl.when(jnp.logical_not(first))
    def _carry_conv_context():
        buf_ref[0:HALO, :] = buf_ref[tm:tm + HALO, :]

    def zero_after(tile):
        bits = lax.bitcast_convert_type(tile, jnp.int32) & zero_ref[...]
        return lax.bitcast_convert_type(bits, F32)[0:1, :]

    hb = _layer_norm(x_ref[0], lng_ref[...], lnb_ref[...]).astype(BF16)
    u = _dot(hb, w_ref[:, C_U:C_UG])
    ug = _dot(hb, w_ref[:, C_UG:C_GC])
    buf_ref[HALO:HALO + tm, :] = u * _sigmoid(ug)

    lane = lax.broadcasted_iota(jnp.int32, (tm, LANES), 1)
    lf = _forget_logs(hb, w_ref, bf_ref, lane)
    c = _exact_tri_dot(tri_ref[...], lf) + carry_ref[...]
    carry_ref[...] = c[tm - 1:tm, :]
    cmat = _bias_operand(c, lane)
    ones = _ones_rows(tm, tm)

    def qk_heads(out_ref, col0, first_pair, n_pairs):
        cols = slice(col0 + first_pair * LANES, col0 + (first_pair + n_pairs) * LANES)
        ext = _dot(cmat, pm_ref[:, cols])
        tiles = _head_tiles(_dot(hb, w_ref[:, cols]), ext, lane)
        for h, tile in enumerate(tiles):
            out_ref[0, 2 * first_pair + h] = tile
        return zero_after(ext[tm - SUBLANES:, (n_pairs - 1) * LANES:])

    def v_heads(first_head, n_heads):
        rows = slice(first_head * HEAD_DIM, (first_head + n_heads) * HEAD_DIM)
        vt = _dot_nt(wvt_ref[rows, :], hb)
        for h in range(n_heads):
            vt_ref[0, first_head + h, 0, 0:HEAD_DIM, :] = vt[h * HEAD_DIM:(h + 1) * HEAD_DIM, :].astype(BF16)
            vt_ref[0, first_head + h, 0, HEAD_DIM:V_ROWS, :] = ones
        return zero_after(vt[n_heads * HEAD_DIM - SUBLANES:, tm - LANES:])

    def attn_gate():
        ga = _dot(hb, w_ref[:, C_GA:C_U])
        ga_ref[0] = ga
        return zero_after(ga[tm - SUBLANES:, D_ATTN - LANES:])

    half = N_HEADS // 2
    quarter = N_HEADS // 4
    mxu_groups = [lambda: v_heads(0, half), lambda: v_heads(half, half), attn_gate,
                  lambda: qk_heads(q_ref, C_Q, 0, quarter), lambda: qk_heads(q_ref, C_Q, quarter, quarter),
                  lambda: qk_heads(k_ref, C_K, 0, quarter), lambda: qk_heads(k_ref, C_K, quarter, quarter)]
    chunks = list(range(0, tm, CONV_ROWS))
    per_chunk = -(-len(mxu_groups) // len(chunks))
    after = jnp.zeros((1, LANES), F32)
    for n, base in enumerate(chunks):
        _conv_rows(buf_ref, cw_ref, cb_ref, conv_ref, base, after)
        for group in mxu_groups[n * per_chunk:(n + 1) * per_chunk]:
            after = group()

    gate = _silu(_dot(hb, w_ref[:, C_GC:C_END]))
    z = _silu(_layer_norm(conv_ref[...], lcg_ref[...], lcb_ref[...]))
    yc_ref[0] = (_dot(z.astype(BF16), wpw_ref[...]) * gate).astype(BF16)


def _attn_body(q_ref, k_ref, vt_ref, km_ref, vmt_ref, ga_ref, o_ref, s_ref, pen_ref, m_ref, acc_ref):
    qi = pl.program_id(2)
    heads = range(HEADS_PER_STEP)

    @pl.when((pl.program_id(0) == 0) & (qi == 0))
    def _causal_penalty():
        key = lax.broadcasted_iota(jnp.int32, (TK, TQ), 0)
        qry = lax.broadcasted_iota(jnp.int32, (TK, TQ), 1)
        pen_ref[0] = jnp.zeros((TK, TQ), F32)
        pen_ref[1] = jnp.where(key <= qry, 0.0, -MASK_BIAS)

    pad = jnp.zeros((META_PAD - N_META, TQ), BF16)
    s_meta = [_dot_nt(km_ref[hh], q_ref[0, hh]) for hh in heads]
    m_meta = [jnp.max(s, axis=0, keepdims=True) for s in s_meta]
    p_meta = [jnp.concatenate([jnp.exp2(s - m).astype(BF16), pad], axis=0) for s, m in zip(s_meta, m_meta)]
    for hh in heads:
        m_ref[hh] = m_meta[hh]
        acc_ref[hh] = _dot(vmt_ref[hh], p_meta[hh])

    def produce(j, hh, penalty):
        off = pl.multiple_of(j * TK, TK)
        s = _dot_nt(k_ref[0, hh, pl.ds(off, TK), :], q_ref[0, hh])
        if penalty is not None:
            s = s + pen_ref[penalty]
        s_ref[hh % SCORE_SLOTS] = s
        return jnp.max(s, axis=0, keepdims=True)

    def consume(j, hh, col_max):
        m = m_ref[hh]
        m_new = jnp.maximum(m, col_max)
        p = jnp.exp2(s_ref[hh % SCORE_SLOTS] - m_new).astype(BF16)
        acc_ref[hh] = acc_ref[hh] * jnp.exp2(m - m_new) + _dot(vt_ref[0, hh, j], p)
        m_ref[hh] = m_new

    def diagonal_flag(j):
        return jnp.where(j < qi, 0, 1)

    def body(j, pending):
        pending = list(pending)
        for hh in heads:
            ahead = hh + LOOKAHEAD
            if ahead < HEADS_PER_STEP:
                pending.append(produce(j, ahead, None))
            else:
                pending.append(produce(j + 1, ahead - HEADS_PER_STEP, diagonal_flag(j + 1)))
            consume(j, hh, pending.pop(0))
        return tuple(pending)

    half = TK // 2

    def produce_diagonal(hh):
        slot = hh % SCORE_SLOTS
        off = pl.multiple_of(qi * TK, TK)
        top = _dot_nt(k_ref[0, hh, pl.ds(off, half), :], q_ref[0, hh]) + pen_ref[1, 0:half, :]
        low = (_dot_nt(k_ref[0, hh, pl.ds(off + half, half), :], q_ref[0, hh, half:, :])
               + pen_ref[1, half:, half:])
        s_ref[slot, 0:half, :] = top
        s_ref[slot, half:, half:] = low
        top_max = jnp.max(top, axis=0, keepdims=True)
        low_max = jnp.max(low, axis=0, keepdims=True)
        return jnp.concatenate([top_max[:, 0:half], jnp.maximum(top_max[:, half:], low_max)], axis=1)

    def consume_diagonal(hh, col_max):
        slot = hh % SCORE_SLOTS
        m = m_ref[hh]
        m_new = jnp.maximum(m, col_max)
        p_top = jnp.exp2(s_ref[slot, 0:half, :] - m_new).astype(BF16)
        p_low = jnp.exp2(s_ref[slot, half:, half:] - m_new[:, half:]).astype(BF16)
        acc = acc_ref[hh] * jnp.exp2(m - m_new) + _dot(vt_ref[0, hh, qi, :, 0:half], p_top)
        acc_ref[hh, :, 0:half] = acc[:, 0:half]
        acc_ref[hh, :, half:] = acc[:, half:] + _dot(vt_ref[0, hh, qi, :, half:], p_low)
        m_ref[hh] = m_new

    def normalized(hh):
        return acc_ref[hh, 0:HEAD_DIM, :] / acc_ref[hh, HEAD_DIM:HEAD_DIM + 1, :]

    pending = lax.fori_loop(0, qi, body, tuple(produce(0, hh, diagonal_flag(0)) for hh in range(LOOKAHEAD)))
    pending = list(pending)
    for hh in heads:
        if hh + LOOKAHEAD < HEADS_PER_STEP:
            pending.append(produce_diagonal(hh + LOOKAHEAD))
        if hh < LOOKAHEAD:
            consume(qi, hh, pending.pop(0))
        else:
            consume_diagonal(hh, pending.pop(0))
        if hh % 2:
            cols = slice((hh - 1) * HEAD_DIM, (hh + 1) * HEAD_DIM)
            pair = jnp.concatenate([normalized(hh - 1), normalized(hh)], axis=0).T
            o_ref[0, :, cols] = (pair * _silu(ga_ref[0, :, cols])).astype(BF16)


def _out_body(x_ref, ya_ref, yc_ref, wout_ref, lig_ref, lib_ref, log_ref, lob_ref, o_ref):
    y = _dot(ya_ref[0], wout_ref[0:D_ATTN, :]) + _dot(yc_ref[0], wout_ref[D_ATTN:, :])
    scaled_h = _layer_norm(x_ref[0], lig_ref[...], lib_ref[...])
    o_ref[0] = _layer_norm(scaled_h + y, log_ref[...], lob_ref[...])


def _placement():
    pm = np.zeros((LANES, 2 * D_ATTN), np.float32)
    for h in range(N_HEADS):
        base = (h // 2) * LANES + (HEAD_DIM if h % 2 == 0 else 0)
        for g in range(3):
            pm[g * N_HEADS + h, C_Q + base + g] = 1.0
            pm[3 * N_HEADS + h, C_Q + base + 3 + g] = 1.0
            pm[3 * N_HEADS + h, C_K + base + g] = 1.0
            pm[g * N_HEADS + h, C_K + base + 3 + g] = -1.0
    return jnp.asarray(pm, BF16)


def _full(shape):
    return pl.BlockSpec(shape, lambda *_: (0,) * len(shape))


def _params(sem, vmem_mb):
    return pltpu.CompilerParams(dimension_semantics=sem, vmem_limit_bytes=vmem_mb * 1024 * 1024)


def kernel(x, meta, ln_in_g, ln_in_b, w_in, b_f, conv_w, conv_b, ln_conv_g, ln_conv_b,
           w_pw, w_out, ln_out_g, ln_out_b):
    B, S, D = x.shape
    assert D == D_MODEL and w_in.shape[0] == 1 and meta.shape == (N_META, D_MODEL)
    assert S % TM_PROJ == 0 and S % TM_OUT == 0 and S % TQ == 0 and TQ == TK == TM_PROJ

    w = w_in[0]
    scale = HEAD_DIM ** -0.5 * LOG2E
    wf = jnp.pad(jnp.tile(w[:, 3 * D_ATTN:3 * D_ATTN + N_HEADS], (1, 3)), ((0, 0), (0, LANES - 3 * N_HEADS)))
    w_all = jnp.concatenate([
        w[:, 0:D_ATTN] * scale, w[:, D_ATTN:2 * D_ATTN], wf, w[:, 3 * D_ATTN + N_HEADS:]], axis=1).astype(BF16)
    wvt = w[:, 2 * D_ATTN:3 * D_ATTN].T.astype(BF16)
    bf_row = jnp.pad(jnp.tile(b_f[0], 3), (0, LANES - 3 * N_HEADS)).reshape(1, LANES)
    pm = _placement()
    lig = ln_in_g.reshape(1, D)
    lib = ln_in_b.reshape(1, D)
    meta_pad = jnp.pad(meta, ((0, META_PAD - N_META), (0, 0)))

    k_meta, vt_meta, glu_meta = pl.pallas_call(
        _meta_body,
        out_shape=(jax.ShapeDtypeStruct((N_HEADS, N_META, LANES), BF16),
                   jax.ShapeDtypeStruct((N_HEADS, V_ROWS, META_PAD), BF16),
                   jax.ShapeDtypeStruct((HALO, D_CONV), F32)),
        compiler_params=pltpu.CompilerParams(vmem_limit_bytes=48 * 1024 * 1024),
        name="meta_proj",
    )(meta_pad, lig, lib, w_all, wvt, bf_row, pm)

    nt = S // TM_PROJ
    head_blk = pl.BlockSpec((1, N_HEADS, TM_PROJ, LANES), lambda b, i: (b, 0, i, 0))
    row_blk = pl.BlockSpec((1, TM_PROJ, D_ATTN), lambda b, i: (b, i, 0))
    qp, kp, vtp, ga, yc = pl.pallas_call(
        _proj_body,
        grid=(B, nt),
        in_specs=[pl.BlockSpec((1, TM_PROJ, D), lambda b, i: (b, i, 0)),
                  _full((1, D)), _full((1, D)), _full((D, C_END)), _full((D_ATTN, D)), _full((1, LANES)),
                  _full((LANES, 2 * D_ATTN)), _full((HALO, D_CONV)),
                  _full((CONV_WIDTH, D_CONV)), _full((1, D_CONV)), _full((1, D_CONV)), _full((1, D_CONV)),
                  _full((D_CONV, D_CONV)), _full((SUBLANES, LANES))],
        out_specs=(head_blk, head_blk,
                   pl.BlockSpec((1, N_HEADS, 1, V_ROWS, TM_PROJ), lambda b, i: (b, 0, i, 0, 0)),
                   row_blk, row_blk),
        out_shape=(jax.ShapeDtypeStruct((B, N_HEADS, S, LANES), BF16),) * 2
                  + (jax.ShapeDtypeStruct((B, N_HEADS, nt, V_ROWS, TM_PROJ), BF16),
                     jax.ShapeDtypeStruct((B, S, D_ATTN), F32),
                     jax.ShapeDtypeStruct((B, S, D_CONV), BF16)),
        scratch_shapes=[pltpu.VMEM((1, LANES), F32), pltpu.VMEM((TM_PROJ, TM_PROJ), BF16),
                        pltpu.VMEM((HALO + TM_PROJ, D_CONV), F32), pltpu.VMEM((TM_PROJ, D_CONV), F32)],
        compiler_params=_params(("arbitrary", "arbitrary"), 56),
        name="in_proj",
    )(x, lig, lib, w_all, wvt, bf_row, pm, glu_meta,
      conv_w[0], conv_b[0].reshape(1, D_CONV), ln_conv_g[0].reshape(1, D_CONV), ln_conv_b[0].reshape(1, D_CONV),
      w_pw[0].astype(BF16), jnp.zeros((SUBLANES, LANES), jnp.int32))

    hp = HEADS_PER_STEP
    att = pl.pallas_call(
        _attn_body,
        grid=(B, N_HEADS // hp, S // TQ),
        in_specs=[pl.BlockSpec((1, hp, TQ, LANES), lambda b, h, i: (b, h, i, 0)),
                  pl.BlockSpec((1, hp, S, LANES), lambda b, h, i: (b, h, 0, 0)),
                  pl.BlockSpec((1, hp, nt, V_ROWS, TK), lambda b, h, i: (b, h, 0, 0, 0)),
                  pl.BlockSpec((hp, N_META, LANES), lambda b, h, i: (h, 0, 0)),
                  pl.BlockSpec((hp, V_ROWS, META_PAD), lambda b, h, i: (h, 0, 0)),
                  pl.BlockSpec((1, TQ, hp * HEAD_DIM), lambda b, h, i: (b, i, h))],
        out_specs=pl.BlockSpec((1, TQ, hp * HEAD_DIM), lambda b, h, i: (b, i, h)),
        out_shape=jax.ShapeDtypeStruct((B, S, D_ATTN), BF16),
        scratch_shapes=[pltpu.VMEM((SCORE_SLOTS, TK, TQ), F32), pltpu.VMEM((2, TK, TQ), F32),
                        pltpu.VMEM((hp, 1, TQ), F32), pltpu.VMEM((hp, V_ROWS, TQ), F32)],
        compiler_params=_params(("arbitrary", "arbitrary", "arbitrary"), 48),
        name="fox_attn",
    )(qp, kp, vtp, k_meta, vt_meta, ga)

    tile_blk = pl.BlockSpec((1, TM_OUT, D_ATTN), lambda b, i: (b, i, 0))
    out = pl.pallas_call(
        _out_body,
        grid=(B, S // TM_OUT),
        in_specs=[pl.BlockSpec((1, TM_OUT, D), lambda b, i: (b, i, 0)), tile_blk, tile_blk,
                  _full((D, D)), _full((1, D)), _full((1, D)), _full((1, D)), _full((1, D))],
        out_specs=pl.BlockSpec((1, TM_OUT, D), lambda b, i: (b, i, 0)),
        out_shape=jax.ShapeDtypeStruct((B, S, D), F32),
        compiler_params=_params(("arbitrary", "arbitrary"), 32),
        name="out_proj",
    )(x, att, yc, w_out[0].astype(BF16), ALPHA * lig, ALPHA * lib,
      ln_out_g[0].reshape(1, D), ln_out_b[0].reshape(1, D))
    return out
```

```python
import numpy as np

import jax
import jax.numpy as jnp
from jax import lax
from jax.experimental import pallas as pl
from jax.experimental.pallas import tpu as pltpu

F32 = jnp.float32
BF16 = jnp.bfloat16

D_MODEL = 1024
N_META = 16
D_ATTN = 512
D_CONV = 512
N_HEADS = 8
HEAD_DIM = 64
CONV_WIDTH = 31
LN_EPS = 1e-5
ALPHA = 2.0 ** 0.25
V7X_VMEM_BYTES = 64 * 1024 * 1024
VALUE_SPILL_BYTES = 10 * 1024 * 1024
LANES = 128
SUBLANES = 8
BF16_ROWS = 16
META_PAD = 128
HALO = 32
MASK_BIAS = 1e30
LOG2E = 1.4426950408889634
V_ROWS = HEAD_DIM + BF16_ROWS

C_Q = 0
C_K = D_ATTN
C_F = 2 * D_ATTN
C_GA = C_F + LANES
C_U = C_GA + D_ATTN
C_UG = C_U + D_CONV
C_GC = C_UG + D_CONV
C_END = C_GC + D_CONV

TM_PROJ = 512
TQ = 512
TK = 512
HEADS_PER_STEP = 8
LOOKAHEAD = 2
SCORE_SLOTS = 4
TM_OUT = 512
OUT_ROWS = 256
CONV_ROWS = 64


def _dot(a, b):
    return jnp.dot(a, b, preferred_element_type=F32)


def _dot_nt(a, b):
    return lax.dot_general(a, b, (((1,), (1,)), ((), ())), preferred_element_type=F32)


def _layer_norm(x, g, b):
    mu = jnp.mean(x, axis=-1, keepdims=True)
    xc = x - mu
    var = jnp.mean(xc * xc, axis=-1, keepdims=True)
    return xc * lax.rsqrt(var + LN_EPS) * g + b


def _sigmoid(x):
    return 1.0 / (1.0 + jnp.exp(-x))


def _silu(x):
    return x * _sigmoid(x)


def _log_sigmoid(x):
    return jnp.minimum(x, 0.0) - jnp.log1p(jnp.exp(-jnp.abs(x)))


def _split3(c):
    hi = c.astype(BF16).astype(F32)
    r = c - hi
    mid = r.astype(BF16).astype(F32)
    lo = (r - mid).astype(BF16).astype(F32)
    return hi, mid, lo


def _bias_operand(c, lane):
    hi, mid, lo = _split3(c * LOG2E)
    one = jnp.where(lane < 4 * N_HEADS, 1.0, 0.0)
    return jnp.where(lane < N_HEADS, hi,
                     jnp.where(lane < 2 * N_HEADS, mid,
                               jnp.where(lane < 3 * N_HEADS, lo, one))).astype(BF16)


def _forget_logs(hb, w_ref, bf_ref, lane):
    f = _dot(hb, w_ref[:, C_F:C_F + LANES])
    return jnp.where(lane < 3 * N_HEADS, _log_sigmoid(f + bf_ref[...]), 0.0)


def _exact_tri_dot(tri, x):
    hi, mid, lo = _split3(x)
    return _dot(tri, hi.astype(BF16)) + _dot(tri, mid.astype(BF16)) + _dot(tri, lo.astype(BF16))


def _head_tiles(proj, bias_cols, lane):
    low_half = lane < HEAD_DIM
    tiles = []
    for p in range(proj.shape[1] // LANES):
        pair = proj[:, p * LANES:(p + 1) * LANES]
        extra = bias_cols[:, p * LANES:(p + 1) * LANES]
        tiles.append(jnp.where(low_half, pair, extra).astype(BF16))
        tiles.append(pltpu.roll(jnp.where(low_half, extra, pair), HEAD_DIM, axis=1).astype(BF16))
    return tiles


def _ones_rows(cols, valid_cols):
    r = lax.broadcasted_iota(jnp.int32, (BF16_ROWS, cols), 0)
    c = lax.broadcasted_iota(jnp.int32, (BF16_ROWS, cols), 1)
    return jnp.where((r == 0) & (c < valid_cols), 1.0, 0.0).astype(BF16)


def _meta_body(x_ref, lng_ref, lnb_ref, w_ref, wvt_ref, bf_ref, pm_ref,
               k_ref, vt_ref, glu_ref):
    rows = x_ref.shape[0]
    hb = _layer_norm(x_ref[...], lng_ref[...], lnb_ref[...]).astype(BF16)
    lane = lax.broadcasted_iota(jnp.int32, (rows, LANES), 1)
    lf = _forget_logs(hb, w_ref, bf_ref, lane)
    r = lax.broadcasted_iota(jnp.int32, (rows, rows), 0)
    c = lax.broadcasted_iota(jnp.int32, (rows, rows), 1)
    upper = jnp.where((c > r) & (c < N_META), 1.0, 0.0).astype(BF16)
    cmat = _bias_operand(-_exact_tri_dot(upper, lf), lane)
    k_tiles = _head_tiles(_dot(hb, w_ref[:, C_K:C_F]), _dot(cmat, pm_ref[:, C_K:C_F]), lane)
    vt = _dot_nt(wvt_ref[...], hb)
    real = lax.broadcasted_iota(jnp.int32, (D_ATTN, rows), 1) < N_META
    vt = jnp.where(real, vt, 0.0).astype(BF16)
    ones = _ones_rows(rows, N_META)
    for h in range(N_HEADS):
        k_ref[h] = k_tiles[h][0:N_META, :]
        vt_ref[h, 0:HEAD_DIM, :] = vt[h * HEAD_DIM:(h + 1) * HEAD_DIM, :]
        vt_ref[h, HEAD_DIM:V_ROWS, :] = ones
    u = _dot(hb, w_ref[:, C_U:C_UG])
    ug = _dot(hb, w_ref[:, C_UG:C_GC])
    glu = u * _sigmoid(ug)
    glu_ref[0:HALO - N_META, :] = jnp.zeros((HALO - N_META, D_CONV), F32)
    glu_ref[HALO - N_META:HALO, :] = glu[0:N_META, :]


def _conv_rows(buf_ref, cw_ref, cb_ref, out_ref, base, zero):
    shift = HALO - (CONV_WIDTH - 1)
    for lo in range(0, D_CONV, LANES):
        cols = slice(lo, lo + LANES)
        out = jnp.broadcast_to(cb_ref[:, cols] + zero, (CONV_ROWS, LANES))
        for r in range(SUBLANES):
            window = CONV_ROWS + (SUBLANES if r else 0)
            part = None
            for j in range(CONV_WIDTH):
                if (shift + j) % SUBLANES != r:
                    continue
                start = base + shift + j - r
                term = cw_ref[j:j + 1, cols] * buf_ref[start:start + window, cols]
                part = term if part is None else part + term
            out = out + part[r:r + CONV_ROWS, :]
        out_ref[base:base + CONV_ROWS, cols] = out


def _proj_body(x_ref, lng_ref, lnb_ref, w_ref, wvt_ref, bf_ref, pm_ref, meta_glu_ref,
               cw_ref, cb_ref, lcg_ref, lcb_ref, wpw_ref, zero_ref,
               q_ref, k_ref, vt_ref, ga_ref, yc_ref,
               carry_ref, tri_ref, buf_ref, conv_ref):
    tm = x_ref.shape[1]
    first = pl.program_id(1) == 0

    @pl.when(first)
    def _start_sequence():
        carry_ref[...] = jnp.zeros_like(carry_ref)
        r = lax.broadcasted_iota(jnp.int32, (tm, tm), 0)
        c = lax.broadcasted_iota(jnp.int32, (tm, tm), 1)
        tri_ref[...] = jnp.where(r >= c, 1.0, 0.0).astype(BF16)
        buf_ref[0:HALO, :] = meta_glu_ref[...]

    @pl.when(jnp.logical_not(first))
    def _carry_conv_context():
        buf_ref[0:HALO, :] = buf_ref[tm:tm + HALO, :]

    def zero_after(tile):
        bits = lax.bitcast_convert_type(tile, jnp.int32) & zero_ref[...]
        return lax.bitcast_convert_type(bits, F32)[0:1, :]

    hb = _layer_norm(x_ref[0], lng_ref[...], lnb_ref[...]).astype(BF16)
    u = _dot(hb, w_ref[:, C_U:C_UG])
    ug = _dot(hb, w_ref[:, C_UG:C_GC])
    buf_ref[HALO:HALO + tm, :] = u * _sigmoid(ug)

    lane = lax.broadcasted_iota(jnp.int32, (tm, LANES), 1)
    lf = _forget_logs(hb, w_ref, bf_ref, lane)
    c = _exact_tri_dot(tri_ref[...], lf) + carry_ref[...]
    carry_ref[...] = c[tm - 1:tm, :]
    cmat = _bias_operand(c, lane)
    ones = _ones_rows(tm, tm)

    def qk_heads(out_ref, col0, first_pair, n_pairs):
        cols = slice(col0 + first_pair * LANES, col0 + (first_pair + n_pairs) * LANES)
        ext = _dot(cmat, pm_ref[:, cols])
        tiles = _head_tiles(_dot(hb, w_ref[:, cols]), ext, lane)
        for h, tile in enumerate(tiles):
            out_ref[0, 2 * first_pair + h] = tile
        return zero_after(ext[tm - SUBLANES:, (n_pairs - 1) * LANES:])

    def v_heads(first_head, n_heads):
        rows = slice(first_head * HEAD_DIM, (first_head + n_heads) * HEAD_DIM)
        vt = _dot_nt(wvt_ref[rows, :], hb)
        for h in range(n_heads):
            vt_ref[0, first_head + h, 0, 0:HEAD_DIM, :] = vt[h * HEAD_DIM:(h + 1) * HEAD_DIM, :].astype(BF16)
            vt_ref[0, first_head + h, 0, HEAD_DIM:V_ROWS, :] = ones
        return zero_after(vt[n_heads * HEAD_DIM - SUBLANES:, tm - LANES:])

    def attn_gate():
        ga = _dot(hb, w_ref[:, C_GA:C_U])
        ga_ref[0] = ga
        return zero_after(ga[tm - SUBLANES:, D_ATTN - LANES:])

    half = N_HEADS // 2
    quarter = N_HEADS // 4
    mxu_groups = [lambda: v_heads(0, half), lambda: v_heads(half, half), attn_gate,
                  lambda: qk_heads(q_ref, C_Q, 0, quarter), lambda: qk_heads(q_ref, C_Q, quarter, quarter),
                  lambda: qk_heads(k_ref, C_K, 0, quarter), lambda: qk_heads(k_ref, C_K, quarter, quarter)]
    chunks = list(range(0, tm, CONV_ROWS))
    per_chunk = -(-len(mxu_groups) // len(chunks))
    after = jnp.zeros((1, LANES), F32)
    for n, base in enumerate(chunks):
        _conv_rows(buf_ref, cw_ref, cb_ref, conv_ref, base, after)
        for group in mxu_groups[n * per_chunk:(n + 1) * per_chunk]:
            after = group()

    gate = _silu(_dot(hb, w_ref[:, C_GC:C_END]))
    z = _silu(_layer_norm(conv_ref[...], lcg_ref[...], lcb_ref[...]))
    yc_ref[0] = (_dot(z.astype(BF16), wpw_ref[...]) * gate).astype(BF16)


def _attn_body(q_ref, k_ref, vt_ref, km_ref, vmt_ref, ga_ref, o_ref, s_ref, pen_ref, m_ref, acc_ref):
    qi = pl.program_id(2)
    heads = range(HEADS_PER_STEP)

    @pl.when((pl.program_id(0) == 0) & (qi == 0))
    def _causal_penalty():
        key = lax.broadcasted_iota(jnp.int32, (TK, TQ), 0)
        qry = lax.broadcasted_iota(jnp.int32, (TK, TQ), 1)
        pen_ref[0] = jnp.zeros((TK, TQ), F32)
        pen_ref[1] = jnp.where(key <= qry, 0.0, -MASK_BIAS)

    pad = jnp.zeros((META_PAD - N_META, TQ), BF16)
    s_meta = [_dot_nt(km_ref[hh], q_ref[0, hh]) for hh in heads]
    m_meta = [jnp.max(s, axis=0, keepdims=True) for s in s_meta]
    p_meta = [jnp.concatenate([jnp.exp2(s - m).astype(BF16), pad], axis=0) for s, m in zip(s_meta, m_meta)]
    for hh in heads:
        m_ref[hh] = m_meta[hh]
        acc_ref[hh] = _dot(vmt_ref[hh], p_meta[hh])

    def produce(j, hh, penalty):
        off = pl.multiple_of(j * TK, TK)
        s = _dot_nt(k_ref[0, hh, pl.ds(off, TK), :], q_ref[0, hh])
        if penalty is not None:
            s = s + pen_ref[penalty]
        s_ref[hh % SCORE_SLOTS] = s
        return jnp.max(s, axis=0, keepdims=True)

    def consume(j, hh, col_max):
        m = m_ref[hh]
        m_new = jnp.maximum(m, col_max)
        p = jnp.exp2(s_ref[hh % SCORE_SLOTS] - m_new).astype(BF16)
        acc_ref[hh] = acc_ref[hh] * jnp.exp2(m - m_new) + _dot(vt_ref[0, hh, j], p)
        m_ref[hh] = m_new

    def diagonal_flag(j):
        return jnp.where(j < qi, 0, 1)

    def body(j, pending):
        pending = list(pending)
        for hh in heads:
            ahead = hh + LOOKAHEAD
            if ahead < HEADS_PER_STEP:
                pending.append(produce(j, ahead, None))
            else:
                pending.append(produce(j + 1, ahead - HEADS_PER_STEP, diagonal_flag(j + 1)))
            consume(j, hh, pending.pop(0))
        return tuple(pending)

    half = TK // 2

    def produce_diagonal(hh):
        slot = hh % SCORE_SLOTS
        off = pl.multiple_of(qi * TK, TK)
        top = _dot_nt(k_ref[0, hh, pl.ds(off, half), :], q_ref[0, hh]) + pen_ref[1, 0:half, :]
        low = (_dot_nt(k_ref[0, hh, pl.ds(off + half, half), :], q_ref[0, hh, half:, :])
               + pen_ref[1, half:, half:])
        s_ref[slot, 0:half, :] = top
        s_ref[slot, half:, half:] = low
        top_max = jnp.max(top, axis=0, keepdims=True)
        low_max = jnp.max(low, axis=0, keepdims=True)
        return jnp.concatenate([top_max[:, 0:half], jnp.maximum(top_max[:, half:], low_max)], axis=1)

    def consume_diagonal(hh, col_max):
        slot = hh % SCORE_SLOTS
        m = m_ref[hh]
        m_new = jnp.maximum(m, col_max)
        p_top = jnp.exp2(s_ref[slot, 0:half, :] - m_new).astype(BF16)
        p_low = jnp.exp2(s_ref[slot, half:, half:] - m_new[:, half:]).astype(BF16)
        acc = acc_ref[hh] * jnp.exp2(m - m_new) + _dot(vt_ref[0, hh, qi, :, 0:half], p_top)
        acc_ref[hh, :, 0:half] = acc[:, 0:half]
        acc_ref[hh, :, half:] = acc[:, half:] + _dot(vt_ref[0, hh, qi, :, half:], p_low)
        m_ref[hh] = m_new

    def normalized(hh):
        return acc_ref[hh, 0:HEAD_DIM, :] / acc_ref[hh, HEAD_DIM:HEAD_DIM + 1, :]

    pending = lax.fori_loop(0, qi, body, tuple(produce(0, hh, diagonal_flag(0)) for hh in range(LOOKAHEAD)))
    pending = list(pending)
    for hh in heads:
        if hh + LOOKAHEAD < HEADS_PER_STEP:
            pending.append(produce_diagonal(hh + LOOKAHEAD))
        if hh < LOOKAHEAD:
            consume(qi, hh, pending.pop(0))
        else:
            consume_diagonal(hh, pending.pop(0))
        if hh % 2:
            cols = slice((hh - 1) * HEAD_DIM, (hh + 1) * HEAD_DIM)
            pair = jnp.concatenate([normalized(hh - 1), normalized(hh)], axis=0).T
            o_ref[0, :, cols] = (pair * _silu(ga_ref[0, :, cols])).astype(BF16)


def _out_body(x_ref, ya_ref, yc_ref, wout_ref, lig_ref, lib_ref, log_ref, lob_ref, o_ref):
    tm = x_ref.shape[1]
    for r0 in range(0, tm, OUT_ROWS):
        rows = slice(r0, r0 + OUT_ROWS)
        y = _dot(ya_ref[0, rows, :], wout_ref[0:D_ATTN, :]) + _dot(yc_ref[0, rows, :], wout_ref[D_ATTN:, :])
        scaled_h = _layer_norm(x_ref[0, rows, :], lig_ref[...], lib_ref[...])
        o_ref[0, rows, :] = _layer_norm(scaled_h + y, log_ref[...], lob_ref[...])


def _placement():
    pm = np.zeros((LANES, 2 * D_ATTN), np.float32)
    for h in range(N_HEADS):
        base = (h // 2) * LANES + (HEAD_DIM if h % 2 == 0 else 0)
        for g in range(3):
            pm[g * N_HEADS + h, C_Q + base + g] = 1.0
            pm[3 * N_HEADS + h, C_Q + base + 3 + g] = 1.0
            pm[3 * N_HEADS + h, C_K + base + g] = 1.0
            pm[g * N_HEADS + h, C_K + base + 3 + g] = -1.0
    return jnp.asarray(pm, BF16)


def _full(shape):
    return pl.BlockSpec(shape, lambda *_: (0,) * len(shape))


def _nbytes(shape, dtype):
    return int(np.prod(shape)) * jnp.dtype(dtype).itemsize


def _call(body, name, operands, out_shape, grid=(), in_specs=None, out_specs=None, scratch=()):
    outs = out_shape if isinstance(out_shape, (tuple, list)) else (out_shape,)
    if in_specs is None:
        blocks = [(a.shape, a.dtype) for a in operands] + [(o.shape, o.dtype) for o in outs]
    else:
        out_blocks = out_specs if isinstance(out_specs, (tuple, list)) else (out_specs,)
        blocks = ([(s.block_shape, a.dtype) for s, a in zip(in_specs, operands)]
                  + [(s.block_shape, o.dtype) for s, o in zip(out_blocks, outs)])
    limit = (2 * sum(_nbytes(*b) for b in blocks) + sum(_nbytes(s.shape, s.dtype) for s in scratch)
             + VALUE_SPILL_BYTES)
    assert limit <= V7X_VMEM_BYTES, (name, limit)
    grid_args = {} if in_specs is None else dict(grid=grid, in_specs=in_specs, out_specs=out_specs)
    return pl.pallas_call(
        body, out_shape=out_shape, scratch_shapes=list(scratch), name=name,
        compiler_params=pltpu.CompilerParams(dimension_semantics=("arbitrary",) * len(grid),
                                             vmem_limit_bytes=limit),
        **grid_args)(*operands)


def kernel(x, meta, ln_in_g, ln_in_b, w_in, b_f, conv_w, conv_b, ln_conv_g, ln_conv_b,
           w_pw, w_out, ln_out_g, ln_out_b):
    B, S, D = x.shape
    assert D == D_MODEL and w_in.shape[0] == 1 and meta.shape == (N_META, D_MODEL)
    assert S % TM_PROJ == 0 and S % TM_OUT == 0 and S % TQ == 0 and TQ == TK == TM_PROJ

    w = w_in[0]
    scale = HEAD_DIM ** -0.5 * LOG2E
    wf = jnp.pad(jnp.tile(w[:, 3 * D_ATTN:3 * D_ATTN + N_HEADS], (1, 3)), ((0, 0), (0, LANES - 3 * N_HEADS)))
    w_all = jnp.concatenate([
        w[:, 0:D_ATTN] * scale, w[:, D_ATTN:2 * D_ATTN], wf, w[:, 3 * D_ATTN + N_HEADS:]], axis=1).astype(BF16)
    wvt = w[:, 2 * D_ATTN:3 * D_ATTN].T.astype(BF16)
    bf_row = jnp.pad(jnp.tile(b_f[0], 3), (0, LANES - 3 * N_HEADS)).reshape(1, LANES)
    pm = _placement()
    lig = ln_in_g.reshape(1, D)
    lib = ln_in_b.reshape(1, D)
    meta_pad = jnp.pad(meta, ((0, META_PAD - N_META), (0, 0)))

    k_meta, vt_meta, glu_meta = _call(
        _meta_body, "meta_proj", (meta_pad, lig, lib, w_all, wvt, bf_row, pm),
        out_shape=(jax.ShapeDtypeStruct((N_HEADS, N_META, LANES), BF16),
                   jax.ShapeDtypeStruct((N_HEADS, V_ROWS, META_PAD), BF16),
                   jax.ShapeDtypeStruct((HALO, D_CONV), F32)))

    nt = S // TM_PROJ
    head_blk = pl.BlockSpec((1, N_HEADS, TM_PROJ, LANES), lambda b, i: (b, 0, i, 0))
    row_blk = pl.BlockSpec((1, TM_PROJ, D_ATTN), lambda b, i: (b, i, 0))
    qp, kp, vtp, ga, yc = _call(
        _proj_body, "in_proj",
        (x, lig, lib, w_all, wvt, bf_row, pm, glu_meta,
         conv_w[0], conv_b[0].reshape(1, D_CONV), ln_conv_g[0].reshape(1, D_CONV), ln_conv_b[0].reshape(1, D_CONV),
         w_pw[0].astype(BF16), jnp.zeros((SUBLANES, LANES), jnp.int32)),
        grid=(B, nt),
        in_specs=[pl.BlockSpec((1, TM_PROJ, D), lambda b, i: (b, i, 0)),
                  _full((1, D)), _full((1, D)), _full((D, C_END)), _full((D_ATTN, D)), _full((1, LANES)),
                  _full((LANES, 2 * D_ATTN)), _full((HALO, D_CONV)),
                  _full((CONV_WIDTH, D_CONV)), _full((1, D_CONV)), _full((1, D_CONV)), _full((1, D_CONV)),
                  _full((D_CONV, D_CONV)), _full((SUBLANES, LANES))],
        out_specs=(head_blk, head_blk,
                   pl.BlockSpec((1, N_HEADS, 1, V_ROWS, TM_PROJ), lambda b, i: (b, 0, i, 0, 0)),
                   row_blk, row_blk),
        out_shape=(jax.ShapeDtypeStruct((B, N_HEADS, S, LANES), BF16),) * 2
                  + (jax.ShapeDtypeStruct((B, N_HEADS, nt, V_ROWS, TM_PROJ), BF16),
                     jax.ShapeDtypeStruct((B, S, D_ATTN), F32),
                     jax.ShapeDtypeStruct((B, S, D_CONV), BF16)),
        scratch=[pltpu.VMEM((1, LANES), F32), pltpu.VMEM((TM_PROJ, TM_PROJ), BF16),
                 pltpu.VMEM((HALO + TM_PROJ, D_CONV), F32), pltpu.VMEM((TM_PROJ, D_CONV), F32)])

    hp = HEADS_PER_STEP
    att = _call(
        _attn_body, "fox_attn", (qp, kp, vtp, k_meta, vt_meta, ga),
        grid=(B, N_HEADS // hp, S // TQ),
        in_specs=[pl.BlockSpec((1, hp, TQ, LANES), lambda b, h, i: (b, h, i, 0)),
                  pl.BlockSpec((1, hp, S, LANES), lambda b, h, i: (b, h, 0, 0)),
                  pl.BlockSpec((1, hp, nt, V_ROWS, TK), lambda b, h, i: (b, h, 0, 0, 0)),
                  pl.BlockSpec((hp, N_META, LANES), lambda b, h, i: (h, 0, 0)),
                  pl.BlockSpec((hp, V_ROWS, META_PAD), lambda b, h, i: (h, 0, 0)),
                  pl.BlockSpec((1, TQ, hp * HEAD_DIM), lambda b, h, i: (b, i, h))],
        out_specs=pl.BlockSpec((1, TQ, hp * HEAD_DIM), lambda b, h, i: (b, i, h)),
        out_shape=jax.ShapeDtypeStruct((B, S, D_ATTN), BF16),
        scratch=[pltpu.VMEM((SCORE_SLOTS, TK, TQ), F32), pltpu.VMEM((2, TK, TQ), F32),
                 pltpu.VMEM((hp, 1, TQ), F32), pltpu.VMEM((hp, V_ROWS, TQ), F32)])

    tile_blk = pl.BlockSpec((1, TM_OUT, D_ATTN), lambda b, i: (b, i, 0))
    return _call(
        _out_body, "out_proj",
        (x, att, yc, w_out[0].astype(BF16), ALPHA * lig, ALPHA * lib,
         ln_out_g[0].reshape(1, D), ln_out_b[0].reshape(1, D)),
        grid=(B, S // TM_OUT),
        in_specs=[pl.BlockSpec((1, TM_OUT, D), lambda b, i: (b, i, 0)), tile_blk, tile_blk,
                  _full((D, D)), _full((1, D)), _full((1, D)), _full((1, D)), _full((1, D))],
        out_specs=pl.BlockSpec((1, TM_OUT, D), lambda b, i: (b, i, 0)),
        out_shape=jax.ShapeDtypeStruct((B, S, D), F32))
```

```python
import numpy as np

import jax
import jax.numpy as jnp
from jax import lax
from jax.experimental import pallas as pl
from jax.experimental.pallas import tpu as pltpu

F32 = jnp.float32
BF16 = jnp.bfloat16

D_MODEL = 1024
N_META = 16
D_ATTN = 512
D_CONV = 512
N_HEADS = 8
HEAD_DIM = 64
CONV_WIDTH = 31
LN_EPS = 1e-5
ALPHA = 2.0 ** 0.25
V7X_VMEM_BYTES = 64 * 1024 * 1024
LANES = 128
SUBLANES = 8
BF16_ROWS = 16
META_PAD = 128
HALO = 32
MASK_BIAS = 1e30
LOG2E = 1.4426950408889634
V_ROWS = HEAD_DIM + BF16_ROWS

C_Q = 0
C_K = D_ATTN
C_F = 2 * D_ATTN
C_GA = C_F + LANES
C_U = C_GA + D_ATTN
C_UG = C_U + D_CONV
C_GC = C_UG + D_CONV
C_END = C_GC + D_CONV

TM_PROJ = 512
TQ = 512
TK = 512
HEADS_PER_STEP = 8
LOOKAHEAD = 2
SCORE_SLOTS = 4
TM_OUT = 512
OUT_ROWS = 256
CONV_ROWS = 64


def _dot(a, b):
    return jnp.dot(a, b, preferred_element_type=F32)


def _dot_nt(a, b):
    return lax.dot_general(a, b, (((1,), (1,)), ((), ())), preferred_element_type=F32)


def _layer_norm(x, g, b):
    mu = jnp.mean(x, axis=-1, keepdims=True)
    xc = x - mu
    var = jnp.mean(xc * xc, axis=-1, keepdims=True)
    return xc * lax.rsqrt(var + LN_EPS) * g + b


def _sigmoid(x):
    return 1.0 / (1.0 + jnp.exp(-x))


def _silu(x):
    return x * _sigmoid(x)


def _log_sigmoid(x):
    return jnp.minimum(x, 0.0) - jnp.log1p(jnp.exp(-jnp.abs(x)))


def _split3(c):
    hi = c.astype(BF16).astype(F32)
    r = c - hi
    mid = r.astype(BF16).astype(F32)
    lo = (r - mid).astype(BF16).astype(F32)
    return hi, mid, lo


def _bias_operand(c, lane):
    hi, mid, lo = _split3(c * LOG2E)
    one = jnp.where(lane < 4 * N_HEADS, 1.0, 0.0)
    return jnp.where(lane < N_HEADS, hi,
                     jnp.where(lane < 2 * N_HEADS, mid,
                               jnp.where(lane < 3 * N_HEADS, lo, one))).astype(BF16)


def _forget_logs(hb, w_ref, bf_ref, lane):
    f = _dot(hb, w_ref[:, C_F:C_F + LANES])
    return jnp.where(lane < 3 * N_HEADS, _log_sigmoid(f + bf_ref[...]), 0.0)


def _exact_tri_dot(tri, x):
    hi, mid, lo = _split3(x)
    return _dot(tri, hi.astype(BF16)) + _dot(tri, mid.astype(BF16)) + _dot(tri, lo.astype(BF16))


def _head_tiles(proj, bias_cols, lane):
    low_half = lane < HEAD_DIM
    tiles = []
    for p in range(proj.shape[1] // LANES):
        pair = proj[:, p * LANES:(p + 1) * LANES]
        extra = bias_cols[:, p * LANES:(p + 1) * LANES]
        tiles.append(jnp.where(low_half, pair, extra).astype(BF16))
        tiles.append(pltpu.roll(jnp.where(low_half, extra, pair), HEAD_DIM, axis=1).astype(BF16))
    return tiles


def _ones_rows(cols, valid_cols):
    r = lax.broadcasted_iota(jnp.int32, (BF16_ROWS, cols), 0)
    c = lax.broadcasted_iota(jnp.int32, (BF16_ROWS, cols), 1)
    return jnp.where((r == 0) & (c < valid_cols), 1.0, 0.0).astype(BF16)


def _meta_body(x_ref, lng_ref, lnb_ref, w_ref, wvt_ref, bf_ref, pm_ref,
               k_ref, vt_ref, glu_ref):
    rows = x_ref.shape[0]
    hb = _layer_norm(x_ref[...], lng_ref[...], lnb_ref[...]).astype(BF16)
    lane = lax.broadcasted_iota(jnp.int32, (rows, LANES), 1)
    lf = _forget_logs(hb, w_ref, bf_ref, lane)
    r = lax.broadcasted_iota(jnp.int32, (rows, rows), 0)
    c = lax.broadcasted_iota(jnp.int32, (rows, rows), 1)
    upper = jnp.where((c > r) & (c < N_META), 1.0, 0.0).astype(BF16)
    cmat = _bias_operand(-_exact_tri_dot(upper, lf), lane)
    k_tiles = _head_tiles(_dot(hb, w_ref[:, C_K:C_F]), _dot(cmat, pm_ref[:, C_K:C_F]), lane)
    vt = _dot_nt(wvt_ref[...], hb)
    real = lax.broadcasted_iota(jnp.int32, (D_ATTN, rows), 1) < N_META
    vt = jnp.where(real, vt, 0.0).astype(BF16)
    ones = _ones_rows(rows, N_META)
    for h in range(N_HEADS):
        k_ref[h] = k_tiles[h][0:N_META, :]
        vt_ref[h, 0:HEAD_DIM, :] = vt[h * HEAD_DIM:(h + 1) * HEAD_DIM, :]
        vt_ref[h, HEAD_DIM:V_ROWS, :] = ones
    u = _dot(hb, w_ref[:, C_U:C_UG])
    ug = _dot(hb, w_ref[:, C_UG:C_GC])
    glu = u * _sigmoid(ug)
    glu_ref[0:HALO - N_META, :] = jnp.zeros((HALO - N_META, D_CONV), F32)
    glu_ref[HALO - N_META:HALO, :] = glu[0:N_META, :]


def _conv_rows(buf_ref, cw_ref, cb_ref, out_ref, base, zero):
    shift = HALO - (CONV_WIDTH - 1)
    for lo in range(0, D_CONV, LANES):
        cols = slice(lo, lo + LANES)
        out = jnp.broadcast_to(cb_ref[:, cols] + zero, (CONV_ROWS, LANES))
        for r in range(SUBLANES):
            window = CONV_ROWS + (SUBLANES if r else 0)
            part = None
            for j in range(CONV_WIDTH):
                if (shift + j) % SUBLANES != r:
                    continue
                start = base + shift + j - r
                term = cw_ref[j:j + 1, cols] * buf_ref[start:start + window, cols]
                part = term if part is None else part + term
            out = out + part[r:r + CONV_ROWS, :]
        out_ref[base:base + CONV_ROWS, cols] = out


def _proj_body(x_ref, lng_ref, lnb_ref, w_ref, wvt_ref, bf_ref, pm_ref, meta_glu_ref,
               cw_ref, cb_ref, lcg_ref, lcb_ref, wpw_ref, zero_ref,
               q_ref, k_ref, vt_ref, ga_ref, yc_ref,
               carry_ref, tri_ref, buf_ref, conv_ref):
    tm = x_ref.shape[1]
    first = pl.program_id(1) == 0

    @pl.when(first)
    def _start_sequence():
        carry_ref[...] = jnp.zeros_like(carry_ref)
        r = lax.broadcasted_iota(jnp.int32, (tm, tm), 0)
        c = lax.broadcasted_iota(jnp.int32, (tm, tm), 1)
        tri_ref[...] = jnp.where(r >= c, 1.0, 0.0).astype(BF16)
        buf_ref[0:HALO, :] = meta_glu_ref[...]

    @pl.when(jnp.logical_not(first))
    def _carry_conv_context():
        buf_ref[0:HALO, :] = buf_ref[tm:tm + HALO, :]

    def zero_after(tile):
        bits = lax.bitcast_convert_type(tile, jnp.int32) & zero_ref[...]
        return lax.bitcast_convert_type(bits, F32)[0:1, :]

    hb = _layer_norm(x_ref[0], lng_ref[...], lnb_ref[...]).astype(BF16)
    u = _dot(hb, w_ref[:, C_U:C_UG])
    ug = _dot(hb, w_ref[:, C_UG:C_GC])
    buf_ref[HALO:HALO + tm, :] = u * _sigmoid(ug)

    lane = lax.broadcasted_iota(jnp.int32, (tm, LANES), 1)
    lf = _forget_logs(hb, w_ref, bf_ref, lane)
    c = _exact_tri_dot(tri_ref[...], lf) + carry_ref[...]
    carry_ref[...] = c[tm - 1:tm, :]
    cmat = _bias_operand(c, lane)
    ones = _ones_rows(tm, tm)

    def qk_heads(out_ref, col0, first_pair, n_pairs):
        cols = slice(col0 + first_pair * LANES, col0 + (first_pair + n_pairs) * LANES)
        ext = _dot(cmat, pm_ref[:, cols])
        tiles = _head_tiles(_dot(hb, w_ref[:, cols]), ext, lane)
        for h, tile in enumerate(tiles):
            out_ref[0, 2 * first_pair + h] = tile
        return zero_after(ext[tm - SUBLANES:, (n_pairs - 1) * LANES:])

    def v_heads(first_head, n_heads):
        rows = slice(first_head * HEAD_DIM, (first_head + n_heads) * HEAD_DIM)
        vt = _dot_nt(wvt_ref[rows, :], hb)
        for h in range(n_heads):
            vt_ref[0, first_head + h, 0, 0:HEAD_DIM, :] = vt[h * HEAD_DIM:(h + 1) * HEAD_DIM, :].astype(BF16)
            vt_ref[0, first_head + h, 0, HEAD_DIM:V_ROWS, :] = ones
        return zero_after(vt[n_heads * HEAD_DIM - SUBLANES:, tm - LANES:])

    def attn_gate():
        ga = _dot(hb, w_ref[:, C_GA:C_U])
        ga_ref[0] = ga
        return zero_after(ga[tm - SUBLANES:, D_ATTN - LANES:])

    half = N_HEADS // 2
    quarter = N_HEADS // 4
    mxu_groups = [lambda: v_heads(0, half), lambda: v_heads(half, half), attn_gate,
                  lambda: qk_heads(q_ref, C_Q, 0, quarter), lambda: qk_heads(q_ref, C_Q, quarter, quarter),
                  lambda: qk_heads(k_ref, C_K, 0, quarter), lambda: qk_heads(k_ref, C_K, quarter, quarter)]
    chunks = list(range(0, tm, CONV_ROWS))
    per_chunk = -(-len(mxu_groups) // len(chunks))
    after = jnp.zeros((1, LANES), F32)
    for n, base in enumerate(chunks):
        _conv_rows(buf_ref, cw_ref, cb_ref, conv_ref, base, after)
        for group in mxu_groups[n * per_chunk:(n + 1) * per_chunk]:
            after = group()

    gate = _silu(_dot(hb, w_ref[:, C_GC:C_END]))
    z = _silu(_layer_norm(conv_ref[...], lcg_ref[...], lcb_ref[...]))
    yc_ref[0] = (_dot(z.astype(BF16), wpw_ref[...]) * gate).astype(BF16)


def _attn_body(q_ref, k_ref, vt_ref, km_ref, vmt_ref, ga_ref, o_ref, s_ref, pen_ref, m_ref, acc_ref):
    qi = pl.program_id(2)
    heads = range(HEADS_PER_STEP)

    @pl.when((pl.program_id(0) == 0) & (qi == 0))
    def _causal_penalty():
        key = lax.broadcasted_iota(jnp.int32, (TK, TQ), 0)
        qry = lax.broadcasted_iota(jnp.int32, (TK, TQ), 1)
        pen_ref[0] = jnp.zeros((TK, TQ), F32)
        pen_ref[1] = jnp.where(key <= qry, 0.0, -MASK_BIAS)

    pad = jnp.zeros((META_PAD - N_META, TQ), BF16)
    s_meta = [_dot_nt(km_ref[hh], q_ref[0, hh]) for hh in heads]
    m_meta = [jnp.max(s, axis=0, keepdims=True) for s in s_meta]
    p_meta = [jnp.concatenate([jnp.exp2(s - m).astype(BF16), pad], axis=0) for s, m in zip(s_meta, m_meta)]
    for hh in heads:
        m_ref[hh] = m_meta[hh]
        acc_ref[hh] = _dot(vmt_ref[hh], p_meta[hh])

    def produce(j, hh, penalty):
        off = pl.multiple_of(j * TK, TK)
        s = _dot_nt(k_ref[0, hh, pl.ds(off, TK), :], q_ref[0, hh])
        if penalty is not None:
            s = s + pen_ref[penalty]
        s_ref[hh % SCORE_SLOTS] = s
        return jnp.max(s, axis=0, keepdims=True)

    def consume(j, hh, col_max):
        m = m_ref[hh]
        m_new = jnp.maximum(m, col_max)
        p = jnp.exp2(s_ref[hh % SCORE_SLOTS] - m_new).astype(BF16)
        acc_ref[hh] = acc_ref[hh] * jnp.exp2(m - m_new) + _dot(vt_ref[0, hh, j], p)
        m_ref[hh] = m_new

    def diagonal_flag(j):
        return jnp.where(j < qi, 0, 1)

    def body(j, pending):
        pending = list(pending)
        for hh in heads:
            ahead = hh + LOOKAHEAD
            if ahead < HEADS_PER_STEP:
                pending.append(produce(j, ahead, None))
            else:
                pending.append(produce(j + 1, ahead - HEADS_PER_STEP, diagonal_flag(j + 1)))
            consume(j, hh, pending.pop(0))
        return tuple(pending)

    half = TK // 2

    def produce_diagonal(hh):
        slot = hh % SCORE_SLOTS
        off = pl.multiple_of(qi * TK, TK)
        top = _dot_nt(k_ref[0, hh, pl.ds(off, half), :], q_ref[0, hh]) + pen_ref[1, 0:half, :]
        low = (_dot_nt(k_ref[0, hh, pl.ds(off + half, half), :], q_ref[0, hh, half:, :])
               + pen_ref[1, half:, half:])
        s_ref[slot, 0:half, :] = top
        s_ref[slot, half:, half:] = low
        top_max = jnp.max(top, axis=0, keepdims=True)
        low_max = jnp.max(low, axis=0, keepdims=True)
        return jnp.concatenate([top_max[:, 0:half], jnp.maximum(top_max[:, half:], low_max)], axis=1)

    def consume_diagonal(hh, col_max):
        slot = hh % SCORE_SLOTS
        m = m_ref[hh]
        m_new = jnp.maximum(m, col_max)
        p_top = jnp.exp2(s_ref[slot, 0:half, :] - m_new).astype(BF16)
        p_low = jnp.exp2(s_ref[slot, half:, half:] - m_new[:, half:]).astype(BF16)
        acc = acc_ref[hh] * jnp.exp2(m - m_new) + _dot(vt_ref[0, hh, qi, :, 0:half], p_top)
        acc_ref[hh, :, 0:half] = acc[:, 0:half]
        acc_ref[hh, :, half:] = acc[:, half:] + _dot(vt_ref[0, hh, qi, :, half:], p_low)
        m_ref[hh] = m_new

    def normalized(hh):
        return acc_ref[hh, 0:HEAD_DIM, :] / acc_ref[hh, HEAD_DIM:HEAD_DIM + 1, :]

    pending = lax.fori_loop(0, qi, body, tuple(produce(0, hh, diagonal_flag(0)) for hh in range(LOOKAHEAD)))
    pending = list(pending)
    for hh in heads:
        if hh + LOOKAHEAD < HEADS_PER_STEP:
            pending.append(produce_diagonal(hh + LOOKAHEAD))
        if hh < LOOKAHEAD:
            consume(qi, hh, pending.pop(0))
        else:
            consume_diagonal(hh, pending.pop(0))
        if hh % 2:
            cols = slice((hh - 1) * HEAD_DIM, (hh + 1) * HEAD_DIM)
            pair = jnp.concatenate([normalized(hh - 1), normalized(hh)], axis=0).T
            o_ref[0, :, cols] = (pair * _silu(ga_ref[0, :, cols])).astype(BF16)


def _out_body(x_ref, ya_ref, yc_ref, wout_ref, lig_ref, lib_ref, log_ref, lob_ref, o_ref):
    tm = x_ref.shape[1]
    for r0 in range(0, tm, OUT_ROWS):
        rows = slice(r0, r0 + OUT_ROWS)
        y = _dot(ya_ref[0, rows, :], wout_ref[0:D_ATTN, :]) + _dot(yc_ref[0, rows, :], wout_ref[D_ATTN:, :])
        scaled_h = _layer_norm(x_ref[0, rows, :], lig_ref[...], lib_ref[...])
        o_ref[0, rows, :] = _layer_norm(scaled_h + y, log_ref[...], lob_ref[...])


def _placement():
    pm = np.zeros((LANES, 2 * D_ATTN), np.float32)
    for h in range(N_HEADS):
        base = (h // 2) * LANES + (HEAD_DIM if h % 2 == 0 else 0)
        for g in range(3):
            pm[g * N_HEADS + h, C_Q + base + g] = 1.0
            pm[3 * N_HEADS + h, C_Q + base + 3 + g] = 1.0
            pm[3 * N_HEADS + h, C_K + base + g] = 1.0
            pm[g * N_HEADS + h, C_K + base + 3 + g] = -1.0
    return jnp.asarray(pm, BF16)


def _full(shape):
    return pl.BlockSpec(shape, lambda *_: (0,) * len(shape))


def _call(body, name, vmem_mib, operands, out_shape, grid=(), in_specs=None, out_specs=None, scratch=()):
    assert vmem_mib * 1024 * 1024 <= V7X_VMEM_BYTES
    grid_args = {} if in_specs is None else dict(grid=grid, in_specs=in_specs, out_specs=out_specs)
    return pl.pallas_call(
        body, out_shape=out_shape, scratch_shapes=list(scratch), name=name,
        compiler_params=pltpu.CompilerParams(dimension_semantics=("arbitrary",) * len(grid),
                                             vmem_limit_bytes=vmem_mib * 1024 * 1024),
        **grid_args)(*operands)


def kernel(x, meta, ln_in_g, ln_in_b, w_in, b_f, conv_w, conv_b, ln_conv_g, ln_conv_b,
           w_pw, w_out, ln_out_g, ln_out_b):
    B, S, D = x.shape
    assert D == D_MODEL and w_in.shape[0] == 1 and meta.shape == (N_META, D_MODEL)
    assert S % TM_PROJ == 0 and S % TM_OUT == 0 and S % TQ == 0 and TQ == TK == TM_PROJ

    w = w_in[0]
    scale = HEAD_DIM ** -0.5 * LOG2E
    wf = jnp.pad(jnp.tile(w[:, 3 * D_ATTN:3 * D_ATTN + N_HEADS], (1, 3)), ((0, 0), (0, LANES - 3 * N_HEADS)))
    w_all = jnp.concatenate([
        w[:, 0:D_ATTN] * scale, w[:, D_ATTN:2 * D_ATTN], wf, w[:, 3 * D_ATTN + N_HEADS:]], axis=1).astype(BF16)
    wvt = w[:, 2 * D_ATTN:3 * D_ATTN].T.astype(BF16)
    bf_row = jnp.pad(jnp.tile(b_f[0], 3), (0, LANES - 3 * N_HEADS)).reshape(1, LANES)
    pm = _placement()
    lig = ln_in_g.reshape(1, D)
    lib = ln_in_b.reshape(1, D)
    meta_pad = jnp.pad(meta, ((0, META_PAD - N_META), (0, 0)))

    k_meta, vt_meta, glu_meta = _call(
        _meta_body, "meta_proj", 48, (meta_pad, lig, lib, w_all, wvt, bf_row, pm),
        out_shape=(jax.ShapeDtypeStruct((N_HEADS, N_META, LANES), BF16),
                   jax.ShapeDtypeStruct((N_HEADS, V_ROWS, META_PAD), BF16),
                   jax.ShapeDtypeStruct((HALO, D_CONV), F32)))

    nt = S // TM_PROJ
    head_blk = pl.BlockSpec((1, N_HEADS, TM_PROJ, LANES), lambda b, i: (b, 0, i, 0))
    row_blk = pl.BlockSpec((1, TM_PROJ, D_ATTN), lambda b, i: (b, i, 0))
    qp, kp, vtp, ga, yc = _call(
        _proj_body, "in_proj", 56,
        (x, lig, lib, w_all, wvt, bf_row, pm, glu_meta,
         conv_w[0], conv_b[0].reshape(1, D_CONV), ln_conv_g[0].reshape(1, D_CONV), ln_conv_b[0].reshape(1, D_CONV),
         w_pw[0].astype(BF16), jnp.zeros((SUBLANES, LANES), jnp.int32)),
        grid=(B, nt),
        in_specs=[pl.BlockSpec((1, TM_PROJ, D), lambda b, i: (b, i, 0)),
                  _full((1, D)), _full((1, D)), _full((D, C_END)), _full((D_ATTN, D)), _full((1, LANES)),
                  _full((LANES, 2 * D_ATTN)), _full((HALO, D_CONV)),
                  _full((CONV_WIDTH, D_CONV)), _full((1, D_CONV)), _full((1, D_CONV)), _full((1, D_CONV)),
                  _full((D_CONV, D_CONV)), _full((SUBLANES, LANES))],
        out_specs=(head_blk, head_blk,
                   pl.BlockSpec((1, N_HEADS, 1, V_ROWS, TM_PROJ), lambda b, i: (b, 0, i, 0, 0)),
                   row_blk, row_blk),
        out_shape=(jax.ShapeDtypeStruct((B, N_HEADS, S, LANES), BF16),) * 2
                  + (jax.ShapeDtypeStruct((B, N_HEADS, nt, V_ROWS, TM_PROJ), BF16),
                     jax.ShapeDtypeStruct((B, S, D_ATTN), F32),
                     jax.ShapeDtypeStruct((B, S, D_CONV), BF16)),
        scratch=[pltpu.VMEM((1, LANES), F32), pltpu.VMEM((TM_PROJ, TM_PROJ), BF16),
                 pltpu.VMEM((HALO + TM_PROJ, D_CONV), F32), pltpu.VMEM((TM_PROJ, D_CONV), F32)])

    hp = HEADS_PER_STEP
    att = _call(
        _attn_body, "fox_attn", 56, (qp, kp, vtp, k_meta, vt_meta, ga),
        grid=(B, N_HEADS // hp, S // TQ),
        in_specs=[pl.BlockSpec((1, hp, TQ, LANES), lambda b, h, i: (b, h, i, 0)),
                  pl.BlockSpec((1, hp, S, LANES), lambda b, h, i: (b, h, 0, 0)),
                  pl.BlockSpec((1, hp, nt, V_ROWS, TK), lambda b, h, i: (b, h, 0, 0, 0)),
                  pl.BlockSpec((hp, N_META, LANES), lambda b, h, i: (h, 0, 0)),
                  pl.BlockSpec((hp, V_ROWS, META_PAD), lambda b, h, i: (h, 0, 0)),
                  pl.BlockSpec((1, TQ, hp * HEAD_DIM), lambda b, h, i: (b, i, h))],
        out_specs=pl.BlockSpec((1, TQ, hp * HEAD_DIM), lambda b, h, i: (b, i, h)),
        out_shape=jax.ShapeDtypeStruct((B, S, D_ATTN), BF16),
        scratch=[pltpu.VMEM((SCORE_SLOTS, TK, TQ), F32), pltpu.VMEM((2, TK, TQ), F32),
                 pltpu.VMEM((hp, 1, TQ), F32), pltpu.VMEM((hp, V_ROWS, TQ), F32)])

    tile_blk = pl.BlockSpec((1, TM_OUT, D_ATTN), lambda b, i: (b, i, 0))
    return _call(
        _out_body, "out_proj", 56,
        (x, att, yc, w_out[0].astype(BF16), ALPHA * lig, ALPHA * lib,
         ln_out_g[0].reshape(1, D), ln_out_b[0].reshape(1, D)),
        grid=(B, S // TM_OUT),
        in_specs=[pl.BlockSpec((1, TM_OUT, D), lambda b, i: (b, i, 0)), tile_blk, tile_blk,
                  _full((D, D)), _full((1, D)), _full((1, D)), _full((1, D)), _full((1, D))],
        out_specs=pl.BlockSpec((1, TM_OUT, D), lambda b, i: (b, i, 0)),
        out_shape=jax.ShapeDtypeStruct((B, S, D), F32))
```

```python
import numpy as np

import jax
import jax.numpy as jnp
from jax import lax
from jax.experimental import pallas as pl
from jax.experimental.pallas import tpu as pltpu

F32 = jnp.float32
BF16 = jnp.bfloat16

D_MODEL = 1024
N_META = 16
D_ATTN = 512
D_CONV = 512
N_HEADS = 8
HEAD_DIM = 64
CONV_WIDTH = 31
LN_EPS = 1e-5
ALPHA = 2.0 ** 0.25
V7X_VMEM_BYTES = 64 * 1024 * 1024
LANES = 128
SUBLANES = 8
BF16_ROWS = 16
META_PAD = 128
HALO = 32
MASK_BIAS = 1e30
LOG2E = 1.4426950408889634
V_ROWS = HEAD_DIM + BF16_ROWS

C_Q = 0
C_K = D_ATTN
C_F = 2 * D_ATTN
C_GA = C_F + LANES
C_U = C_GA + D_ATTN
C_UG = C_U + D_CONV
C_GC = C_UG + D_CONV
C_END = C_GC + D_CONV

TM_PROJ = 512
TQ = 512
TK = 512
HEADS_PER_STEP = 8
LOOKAHEAD = 2
SCORE_SLOTS = 4
TM_OUT = 1024
OUT_ROWS = 256
CONV_ROWS = 64


def _dot(a, b):
    return jnp.dot(a, b, preferred_element_type=F32)


def _dot_nt(a, b):
    return lax.dot_general(a, b, (((1,), (1,)), ((), ())), preferred_element_type=F32)


def _layer_norm(x, g, b):
    mu = jnp.mean(x, axis=-1, keepdims=True)
    xc = x - mu
    var = jnp.mean(xc * xc, axis=-1, keepdims=True)
    return xc * lax.rsqrt(var + LN_EPS) * g + b


def _sigmoid(x):
    return 1.0 / (1.0 + jnp.exp(-x))


def _silu(x):
    return x * _sigmoid(x)


def _log_sigmoid(x):
    return jnp.minimum(x, 0.0) - jnp.log1p(jnp.exp(-jnp.abs(x)))


def _split3(c):
    hi = c.astype(BF16).astype(F32)
    r = c - hi
    mid = r.astype(BF16).astype(F32)
    lo = (r - mid).astype(BF16).astype(F32)
    return hi, mid, lo


def _bias_operand(c, lane):
    hi, mid, lo = _split3(c * LOG2E)
    one = jnp.where(lane < 4 * N_HEADS, 1.0, 0.0)
    return jnp.where(lane < N_HEADS, hi,
                     jnp.where(lane < 2 * N_HEADS, mid,
                               jnp.where(lane < 3 * N_HEADS, lo, one))).astype(BF16)


def _forget_logs(hb, w_ref, bf_ref, lane):
    f = _dot(hb, w_ref[:, C_F:C_F + LANES])
    return jnp.where(lane < 3 * N_HEADS, _log_sigmoid(f + bf_ref[...]), 0.0)


def _exact_tri_dot(tri, x):
    hi, mid, lo = _split3(x)
    return _dot(tri, hi.astype(BF16)) + _dot(tri, mid.astype(BF16)) + _dot(tri, lo.astype(BF16))


def _head_tiles(proj, bias_cols, lane):
    low_half = lane < HEAD_DIM
    tiles = []
    for p in range(proj.shape[1] // LANES):
        pair = proj[:, p * LANES:(p + 1) * LANES]
        extra = bias_cols[:, p * LANES:(p + 1) * LANES]
        tiles.append(jnp.where(low_half, pair, extra).astype(BF16))
        tiles.append(pltpu.roll(jnp.where(low_half, extra, pair), HEAD_DIM, axis=1).astype(BF16))
    return tiles


def _ones_rows(cols, valid_cols):
    r = lax.broadcasted_iota(jnp.int32, (BF16_ROWS, cols), 0)
    c = lax.broadcasted_iota(jnp.int32, (BF16_ROWS, cols), 1)
    return jnp.where((r == 0) & (c < valid_cols), 1.0, 0.0).astype(BF16)


def _meta_body(x_ref, lng_ref, lnb_ref, w_ref, wvt_ref, bf_ref, pm_ref,
               k_ref, vt_ref, glu_ref):
    rows = x_ref.shape[0]
    hb = _layer_norm(x_ref[...], lng_ref[...], lnb_ref[...]).astype(BF16)
    lane = lax.broadcasted_iota(jnp.int32, (rows, LANES), 1)
    lf = _forget_logs(hb, w_ref, bf_ref, lane)
    r = lax.broadcasted_iota(jnp.int32, (rows, rows), 0)
    c = lax.broadcasted_iota(jnp.int32, (rows, rows), 1)
    upper = jnp.where((c > r) & (c < N_META), 1.0, 0.0).astype(BF16)
    cmat = _bias_operand(-_exact_tri_dot(upper, lf), lane)
    k_tiles = _head_tiles(_dot(hb, w_ref[:, C_K:C_F]), _dot(cmat, pm_ref[:, C_K:C_F]), lane)
    vt = _dot_nt(wvt_ref[...], hb)
    real = lax.broadcasted_iota(jnp.int32, (D_ATTN, rows), 1) < N_META
    vt = jnp.where(real, vt, 0.0).astype(BF16)
    ones = _ones_rows(rows, N_META)
    for h in range(N_HEADS):
        k_ref[h] = k_tiles[h][0:N_META, :]
        vt_ref[h, 0:HEAD_DIM, :] = vt[h * HEAD_DIM:(h + 1) * HEAD_DIM, :]
        vt_ref[h, HEAD_DIM:V_ROWS, :] = ones
    u = _dot(hb, w_ref[:, C_U:C_UG])
    ug = _dot(hb, w_ref[:, C_UG:C_GC])
    glu = u * _sigmoid(ug)
    glu_ref[0:HALO - N_META, :] = jnp.zeros((HALO - N_META, D_CONV), F32)
    glu_ref[HALO - N_META:HALO, :] = glu[0:N_META, :]


def _conv_rows(buf_ref, cw_ref, cb_ref, out_ref, base, zero):
    shift = HALO - (CONV_WIDTH - 1)
    for lo in range(0, D_CONV, LANES):
        cols = slice(lo, lo + LANES)
        out = jnp.broadcast_to(cb_ref[:, cols] + zero, (CONV_ROWS, LANES))
        for r in range(SUBLANES):
            window = CONV_ROWS + (SUBLANES if r else 0)
            part = None
            for j in range(CONV_WIDTH):
                if (shift + j) % SUBLANES != r:
                    continue
                start = base + shift + j - r
                term = cw_ref[j:j + 1, cols] * buf_ref[start:start + window, cols]
                part = term if part is None else part + term
            out = out + part[r:r + CONV_ROWS, :]
        out_ref[base:base + CONV_ROWS, cols] = out


def _proj_body(x_ref, lng_ref, lnb_ref, w_ref, wvt_ref, bf_ref, pm_ref, meta_glu_ref,
               cw_ref, cb_ref, lcg_ref, lcb_ref, wpw_ref, zero_ref,
               q_ref, k_ref, vt_ref, ga_ref, yc_ref,
               carry_ref, tri_ref, buf_ref, conv_ref):
    tm = x_ref.shape[1]
    first = pl.program_id(1) == 0

    @pl.when(first)
    def _start_sequence():
        carry_ref[...] = jnp.zeros_like(carry_ref)
        r = lax.broadcasted_iota(jnp.int32, (tm, tm), 0)
        c = lax.broadcasted_iota(jnp.int32, (tm, tm), 1)
        tri_ref[...] = jnp.where(r >= c, 1.0, 0.0).astype(BF16)
        buf_ref[0:HALO, :] = meta_glu_ref[...]

    @pl.when(jnp.logical_not(first))
    def _carry_conv_context():
        buf_ref[0:HALO, :] = buf_ref[tm:tm + HALO, :]

    def zero_after(tile):
        bits = lax.bitcast_convert_type(tile, jnp.int32) & zero_ref[...]
        return lax.bitcast_convert_type(bits, F32)[0:1, :]

    hb = _layer_norm(x_ref[0], lng_ref[...], lnb_ref[...]).astype(BF16)
    u = _dot(hb, w_ref[:, C_U:C_UG])
    ug = _dot(hb, w_ref[:, C_UG:C_GC])
    buf_ref[HALO:HALO + tm, :] = u * _sigmoid(ug)

    lane = lax.broadcasted_iota(jnp.int32, (tm, LANES), 1)
    lf = _forget_logs(hb, w_ref, bf_ref, lane)
    c = _exact_tri_dot(tri_ref[...], lf) + carry_ref[...]
    carry_ref[...] = c[tm - 1:tm, :]
    cmat = _bias_operand(c, lane)
    ones = _ones_rows(tm, tm)

    def qk_heads(out_ref, col0, first_pair, n_pairs):
        cols = slice(col0 + first_pair * LANES, col0 + (first_pair + n_pairs) * LANES)
        ext = _dot(cmat, pm_ref[:, cols])
        tiles = _head_tiles(_dot(hb, w_ref[:, cols]), ext, lane)
        for h, tile in enumerate(tiles):
            out_ref[0, 2 * first_pair + h] = tile
        return zero_after(ext[tm - SUBLANES:, (n_pairs - 1) * LANES:])

    def v_heads(first_head, n_heads):
        rows = slice(first_head * HEAD_DIM, (first_head + n_heads) * HEAD_DIM)
        vt = _dot_nt(wvt_ref[rows, :], hb)
        for h in range(n_heads):
            vt_ref[0, first_head + h, 0, 0:HEAD_DIM, :] = vt[h * HEAD_DIM:(h + 1) * HEAD_DIM, :].astype(BF16)
            vt_ref[0, first_head + h, 0, HEAD_DIM:V_ROWS, :] = ones
        return zero_after(vt[n_heads * HEAD_DIM - SUBLANES:, tm - LANES:])

    def attn_gate():
        ga = _dot(hb, w_ref[:, C_GA:C_U])
        ga_ref[0] = ga
        return zero_after(ga[tm - SUBLANES:, D_ATTN - LANES:])

    half = N_HEADS // 2
    quarter = N_HEADS // 4
    mxu_groups = [lambda: v_heads(0, half), lambda: v_heads(half, half), attn_gate,
                  lambda: qk_heads(q_ref, C_Q, 0, quarter), lambda: qk_heads(q_ref, C_Q, quarter, quarter),
                  lambda: qk_heads(k_ref, C_K, 0, quarter), lambda: qk_heads(k_ref, C_K, quarter, quarter)]
    chunks = list(range(0, tm, CONV_ROWS))
    per_chunk = -(-len(mxu_groups) // len(chunks))
    after = jnp.zeros((1, LANES), F32)
    for n, base in enumerate(chunks):
        _conv_rows(buf_ref, cw_ref, cb_ref, conv_ref, base, after)
        for group in mxu_groups[n * per_chunk:(n + 1) * per_chunk]:
            after = group()

    gate = _silu(_dot(hb, w_ref[:, C_GC:C_END]))
    z = _silu(_layer_norm(conv_ref[...], lcg_ref[...], lcb_ref[...]))
    yc_ref[0] = (_dot(z.astype(BF16), wpw_ref[...]) * gate).astype(BF16)


def _attn_body(q_ref, k_ref, vt_ref, km_ref, vmt_ref, ga_ref, o_ref, s_ref, pen_ref, m_ref, acc_ref):
    qi = pl.program_id(2)
    heads = range(HEADS_PER_STEP)

    @pl.when((pl.program_id(0) == 0) & (qi == 0))
    def _causal_penalty():
        key = lax.broadcasted_iota(jnp.int32, (TK, TQ), 0)
        qry = lax.broadcasted_iota(jnp.int32, (TK, TQ), 1)
        pen_ref[0] = jnp.zeros((TK, TQ), F32)
        pen_ref[1] = jnp.where(key <= qry, 0.0, -MASK_BIAS)

    pad = jnp.zeros((META_PAD - N_META, TQ), BF16)
    s_meta = [_dot_nt(km_ref[hh], q_ref[0, hh]) for hh in heads]
    m_meta = [jnp.max(s, axis=0, keepdims=True) for s in s_meta]
    p_meta = [jnp.concatenate([jnp.exp2(s - m).astype(BF16), pad], axis=0) for s, m in zip(s_meta, m_meta)]
    for hh in heads:
        m_ref[hh] = m_meta[hh]
        acc_ref[hh] = _dot(vmt_ref[hh], p_meta[hh])

    def produce(j, hh, penalty):
        off = pl.multiple_of(j * TK, TK)
        s = _dot_nt(k_ref[0, hh, pl.ds(off, TK), :], q_ref[0, hh])
        if penalty is not None:
            s = s + pen_ref[penalty]
        s_ref[hh % SCORE_SLOTS] = s
        return jnp.max(s, axis=0, keepdims=True)

    def consume(j, hh, col_max):
        m = m_ref[hh]
        m_new = jnp.maximum(m, col_max)
        p = jnp.exp2(s_ref[hh % SCORE_SLOTS] - m_new).astype(BF16)
        acc_ref[hh] = acc_ref[hh] * jnp.exp2(m - m_new) + _dot(vt_ref[0, hh, j], p)
        m_ref[hh] = m_new

    def diagonal_flag(j):
        return jnp.where(j < qi, 0, 1)

    def body(j, pending):
        pending = list(pending)
        for hh in heads:
            ahead = hh + LOOKAHEAD
            if ahead < HEADS_PER_STEP:
                pending.append(produce(j, ahead, None))
            else:
                pending.append(produce(j + 1, ahead - HEADS_PER_STEP, diagonal_flag(j + 1)))
            consume(j, hh, pending.pop(0))
        return tuple(pending)

    half = TK // 2

    def produce_diagonal(hh):
        slot = hh % SCORE_SLOTS
        off = pl.multiple_of(qi * TK, TK)
        top = _dot_nt(k_ref[0, hh, pl.ds(off, half), :], q_ref[0, hh]) + pen_ref[1, 0:half, :]
        low = (_dot_nt(k_ref[0, hh, pl.ds(off + half, half), :], q_ref[0, hh, half:, :])
               + pen_ref[1, half:, half:])
        s_ref[slot, 0:half, :] = top
        s_ref[slot, half:, half:] = low
        top_max = jnp.max(top, axis=0, keepdims=True)
        low_max = jnp.max(low, axis=0, keepdims=True)
        return jnp.concatenate([top_max[:, 0:half], jnp.maximum(top_max[:, half:], low_max)], axis=1)

    def consume_diagonal(hh, col_max):
        slot = hh % SCORE_SLOTS
        m = m_ref[hh]
        m_new = jnp.maximum(m, col_max)
        p_top = jnp.exp2(s_ref[slot, 0:half, :] - m_new).astype(BF16)
        p_low = jnp.exp2(s_ref[slot, half:, half:] - m_new[:, half:]).astype(BF16)
        acc = acc_ref[hh] * jnp.exp2(m - m_new) + _dot(vt_ref[0, hh, qi, :, 0:half], p_top)
        acc_ref[hh, :, 0:half] = acc[:, 0:half]
        acc_ref[hh, :, half:] = acc[:, half:] + _dot(vt_ref[0, hh, qi, :, half:], p_low)
        m_ref[hh] = m_new

    def normalized(hh):
        return acc_ref[hh, 0:HEAD_DIM, :] / acc_ref[hh, HEAD_DIM:HEAD_DIM + 1, :]

    pending = lax.fori_loop(0, qi, body, tuple(produce(0, hh, diagonal_flag(0)) for hh in range(LOOKAHEAD)))
    pending = list(pending)
    for hh in heads:
        if hh + LOOKAHEAD < HEADS_PER_STEP:
            pending.append(produce_diagonal(hh + LOOKAHEAD))
        if hh < LOOKAHEAD:
            consume(qi, hh, pending.pop(0))
        else:
            consume_diagonal(hh, pending.pop(0))
        if hh % 2:
            cols = slice((hh - 1) * HEAD_DIM, (hh + 1) * HEAD_DIM)
            pair = jnp.concatenate([normalized(hh - 1), normalized(hh)], axis=0).T
            o_ref[0, :, cols] = (pair * _silu(ga_ref[0, :, cols])).astype(BF16)


def _out_body(x_ref, ya_ref, yc_ref, wout_ref, lig_ref, lib_ref, log_ref, lob_ref, o_ref):
    tm = x_ref.shape[1]
    for r0 in range(0, tm, OUT_ROWS):
        rows = slice(r0, r0 + OUT_ROWS)
        y = _dot(ya_ref[0, rows, :], wout_ref[0:D_ATTN, :]) + _dot(yc_ref[0, rows, :], wout_ref[D_ATTN:, :])
        scaled_h = _layer_norm(x_ref[0, rows, :], lig_ref[...], lib_ref[...])
        o_ref[0, rows, :] = _layer_norm(scaled_h + y, log_ref[...], lob_ref[...])


def _placement():
    pm = np.zeros((LANES, 2 * D_ATTN), np.float32)
    for h in range(N_HEADS):
        base = (h // 2) * LANES + (HEAD_DIM if h % 2 == 0 else 0)
        for g in range(3):
            pm[g * N_HEADS + h, C_Q + base + g] = 1.0
            pm[3 * N_HEADS + h, C_Q + base + 3 + g] = 1.0
            pm[3 * N_HEADS + h, C_K + base + g] = 1.0
            pm[g * N_HEADS + h, C_K + base + 3 + g] = -1.0
    return jnp.asarray(pm, BF16)


def _full(shape):
    return pl.BlockSpec(shape, lambda *_: (0,) * len(shape))


def _call(body, name, vmem_mib, operands, out_shape, grid=(), in_specs=None, out_specs=None, scratch=()):
    assert vmem_mib * 1024 * 1024 <= V7X_VMEM_BYTES
    grid_args = {} if in_specs is None else dict(grid=grid, in_specs=in_specs, out_specs=out_specs)
    return pl.pallas_call(
        body, out_shape=out_shape, scratch_shapes=list(scratch), name=name,
        compiler_params=pltpu.CompilerParams(dimension_semantics=("arbitrary",) * len(grid),
                                             vmem_limit_bytes=vmem_mib * 1024 * 1024),
        **grid_args)(*operands)


def kernel(x, meta, ln_in_g, ln_in_b, w_in, b_f, conv_w, conv_b, ln_conv_g, ln_conv_b,
           w_pw, w_out, ln_out_g, ln_out_b):
    B, S, D = x.shape
    assert D == D_MODEL and w_in.shape[0] == 1 and meta.shape == (N_META, D_MODEL)
    assert S % TM_PROJ == 0 and S % TM_OUT == 0 and S % TQ == 0 and TQ == TK == TM_PROJ

    w = w_in[0]
    scale = HEAD_DIM ** -0.5 * LOG2E
    wf = jnp.pad(jnp.tile(w[:, 3 * D_ATTN:3 * D_ATTN + N_HEADS], (1, 3)), ((0, 0), (0, LANES - 3 * N_HEADS)))
    w_all = jnp.concatenate([
        w[:, 0:D_ATTN] * scale, w[:, D_ATTN:2 * D_ATTN], wf, w[:, 3 * D_ATTN + N_HEADS:]], axis=1).astype(BF16)
    wvt = w[:, 2 * D_ATTN:3 * D_ATTN].T.astype(BF16)
    bf_row = jnp.pad(jnp.tile(b_f[0], 3), (0, LANES - 3 * N_HEADS)).reshape(1, LANES)
    pm = _placement()
    lig = ln_in_g.reshape(1, D)
    lib = ln_in_b.reshape(1, D)
    meta_pad = jnp.pad(meta, ((0, META_PAD - N_META), (0, 0)))

    k_meta, vt_meta, glu_meta = _call(
        _meta_body, "meta_proj", 48, (meta_pad, lig, lib, w_all, wvt, bf_row, pm),
        out_shape=(jax.ShapeDtypeStruct((N_HEADS, N_META, LANES), BF16),
                   jax.ShapeDtypeStruct((N_HEADS, V_ROWS, META_PAD), BF16),
                   jax.ShapeDtypeStruct((HALO, D_CONV), F32)))

    nt = S // TM_PROJ
    head_blk = pl.BlockSpec((1, N_HEADS, TM_PROJ, LANES), lambda b, i: (b, 0, i, 0))
    row_blk = pl.BlockSpec((1, TM_PROJ, D_ATTN), lambda b, i: (b, i, 0))
    qp, kp, vtp, ga, yc = _call(
        _proj_body, "in_proj", 56,
        (x, lig, lib, w_all, wvt, bf_row, pm, glu_meta,
         conv_w[0], conv_b[0].reshape(1, D_CONV), ln_conv_g[0].reshape(1, D_CONV), ln_conv_b[0].reshape(1, D_CONV),
         w_pw[0].astype(BF16), jnp.zeros((SUBLANES, LANES), jnp.int32)),
        grid=(B, nt),
        in_specs=[pl.BlockSpec((1, TM_PROJ, D), lambda b, i: (b, i, 0)),
                  _full((1, D)), _full((1, D)), _full((D, C_END)), _full((D_ATTN, D)), _full((1, LANES)),
                  _full((LANES, 2 * D_ATTN)), _full((HALO, D_CONV)),
                  _full((CONV_WIDTH, D_CONV)), _full((1, D_CONV)), _full((1, D_CONV)), _full((1, D_CONV)),
                  _full((D_CONV, D_CONV)), _full((SUBLANES, LANES))],
        out_specs=(head_blk, head_blk,
                   pl.BlockSpec((1, N_HEADS, 1, V_ROWS, TM_PROJ), lambda b, i: (b, 0, i, 0, 0)),
                   row_blk, row_blk),
        out_shape=(jax.ShapeDtypeStruct((B, N_HEADS, S, LANES), BF16),) * 2
                  + (jax.ShapeDtypeStruct((B, N_HEADS, nt, V_ROWS, TM_PROJ), BF16),
                     jax.ShapeDtypeStruct((B, S, D_ATTN), F32),
                     jax.ShapeDtypeStruct((B, S, D_CONV), BF16)),
        scratch=[pltpu.VMEM((1, LANES), F32), pltpu.VMEM((TM_PROJ, TM_PROJ), BF16),
                 pltpu.VMEM((HALO + TM_PROJ, D_CONV), F32), pltpu.VMEM((TM_PROJ, D_CONV), F32)])

    hp = HEADS_PER_STEP
    att = _call(
        _attn_body, "fox_attn", 56, (qp, kp, vtp, k_meta, vt_meta, ga),
        grid=(B, N_HEADS // hp, S // TQ),
        in_specs=[pl.BlockSpec((1, hp, TQ, LANES), lambda b, h, i: (b, h, i, 0)),
                  pl.BlockSpec((1, hp, S, LANES), lambda b, h, i: (b, h, 0, 0)),
                  pl.BlockSpec((1, hp, nt, V_ROWS, TK), lambda b, h, i: (b, h, 0, 0, 0)),
                  pl.BlockSpec((hp, N_META, LANES), lambda b, h, i: (h, 0, 0)),
                  pl.BlockSpec((hp, V_ROWS, META_PAD), lambda b, h, i: (h, 0, 0)),
                  pl.BlockSpec((1, TQ, hp * HEAD_DIM), lambda b, h, i: (b, i, h))],
        out_specs=pl.BlockSpec((1, TQ, hp * HEAD_DIM), lambda b, h, i: (b, i, h)),
        out_shape=jax.ShapeDtypeStruct((B, S, D_ATTN), BF16),
        scratch=[pltpu.VMEM((SCORE_SLOTS, TK, TQ), F32), pltpu.VMEM((2, TK, TQ), F32),
                 pltpu.VMEM((hp, 1, TQ), F32), pltpu.VMEM((hp, V_ROWS, TQ), F32)])

    tile_blk = pl.BlockSpec((1, TM_OUT, D_ATTN), lambda b, i: (b, i, 0))
    return _call(
        _out_body, "out_proj", 56,
        (x, att, yc, w_out[0].astype(BF16), ALPHA * lig, ALPHA * lib,
         ln_out_g[0].reshape(1, D), ln_out_b[0].reshape(1, D)),
        grid=(B, S // TM_OUT),
        in_specs=[pl.BlockSpec((1, TM_OUT, D), lambda b, i: (b, i, 0)), tile_blk, tile_blk,
                  _full((D, D)), _full((1, D)), _full((1, D)), _full((1, D)), _full((1, D))],
        out_specs=pl.BlockSpec((1, TM_OUT, D), lambda b, i: (b, i, 0)),
        out_shape=jax.ShapeDtypeStruct((B, S, D), F32))
```

```python
import numpy as np

import jax
import jax.numpy as jnp
from jax import lax
from jax.experimental import pallas as pl
from jax.experimental.pallas import tpu as pltpu

F32 = jnp.float32
BF16 = jnp.bfloat16

D_MODEL = 1024
N_META = 16
D_ATTN = 512
D_CONV = 512
N_HEADS = 8
HEAD_DIM = 64
CONV_WIDTH = 31
LN_EPS = 1e-5
ALPHA = 2.0 ** 0.25
V7X_VMEM_BYTES = 64 * 1024 * 1024
LANES = 128
SUBLANES = 8
BF16_ROWS = 16
META_PAD = 128
HALO = 32
MASK_BIAS = 1e30
LOG2E = 1.4426950408889634
V_ROWS = HEAD_DIM + BF16_ROWS

C_Q = 0
C_K = D_ATTN
C_F = 2 * D_ATTN
C_GA = C_F + LANES
C_U = C_GA + D_ATTN
C_UG = C_U + D_CONV
C_GC = C_UG + D_CONV
C_END = C_GC + D_CONV

TM_PROJ = 512
TQ = 512
TK = 512
HEADS_PER_STEP = 8
LOOKAHEAD = 2
SCORE_SLOTS = 4
TM_OUT = 1024
OUT_ROWS = 256
CONV_ROWS = 64


def _dot(a, b):
    return jnp.dot(a, b, preferred_element_type=F32)


def _dot_nt(a, b):
    return lax.dot_general(a, b, (((1,), (1,)), ((), ())), preferred_element_type=F32)


def _layer_norm(x, g, b):
    mu = jnp.mean(x, axis=-1, keepdims=True)
    xc = x - mu
    var = jnp.mean(xc * xc, axis=-1, keepdims=True)
    return xc * lax.rsqrt(var + LN_EPS) * g + b


def _sigmoid(x):
    return 1.0 / (1.0 + jnp.exp(-x))


def _silu(x):
    return x * _sigmoid(x)


def _log_sigmoid(x):
    return jnp.minimum(x, 0.0) - jnp.log1p(jnp.exp(-jnp.abs(x)))


def _split3(c):
    hi = c.astype(BF16).astype(F32)
    r = c - hi
    mid = r.astype(BF16).astype(F32)
    lo = (r - mid).astype(BF16).astype(F32)
    return hi, mid, lo


def _bias_operand(c, lane):
    hi, mid, lo = _split3(c * LOG2E)
    one = jnp.where(lane < 4 * N_HEADS, 1.0, 0.0)
    return jnp.where(lane < N_HEADS, hi,
                     jnp.where(lane < 2 * N_HEADS, mid,
                               jnp.where(lane < 3 * N_HEADS, lo, one))).astype(BF16)


def _forget_logs(hb, w_ref, bf_ref, lane):
    f = _dot(hb, w_ref[:, C_F:C_F + LANES])
    return jnp.where(lane < 3 * N_HEADS, _log_sigmoid(f + bf_ref[...]), 0.0)


def _exact_tri_dot(tri, x):
    hi, mid, lo = _split3(x)
    return _dot(tri, hi.astype(BF16)) + _dot(tri, mid.astype(BF16)) + _dot(tri, lo.astype(BF16))


def _head_tiles(proj, bias_cols, lane):
    low_half = lane < HEAD_DIM
    tiles = []
    for p in range(proj.shape[1] // LANES):
        pair = proj[:, p * LANES:(p + 1) * LANES]
        extra = bias_cols[:, p * LANES:(p + 1) * LANES]
        tiles.append(jnp.where(low_half, pair, extra).astype(BF16))
        tiles.append(pltpu.roll(jnp.where(low_half, extra, pair), HEAD_DIM, axis=1).astype(BF16))
    return tiles


def _ones_rows(cols, valid_cols):
    r = lax.broadcasted_iota(jnp.int32, (BF16_ROWS, cols), 0)
    c = lax.broadcasted_iota(jnp.int32, (BF16_ROWS, cols), 1)
    return jnp.where((r == 0) & (c < valid_cols), 1.0, 0.0).astype(BF16)


def _meta_body(x_ref, lng_ref, lnb_ref, w_ref, wvt_ref, bf_ref, pm_ref,
               k_ref, vt_ref, glu_ref):
    rows = x_ref.shape[0]
    hb = _layer_norm(x_ref[...], lng_ref[...], lnb_ref[...]).astype(BF16)
    lane = lax.broadcasted_iota(jnp.int32, (rows, LANES), 1)
    lf = _forget_logs(hb, w_ref, bf_ref, lane)
    r = lax.broadcasted_iota(jnp.int32, (rows, rows), 0)
    c = lax.broadcasted_iota(jnp.int32, (rows, rows), 1)
    upper = jnp.where((c > r) & (c < N_META), 1.0, 0.0).astype(BF16)
    cmat = _bias_operand(-_exact_tri_dot(upper, lf), lane)
    k_tiles = _head_tiles(_dot(hb, w_ref[:, C_K:C_F]), _dot(cmat, pm_ref[:, C_K:C_F]), lane)
    vt = _dot_nt(wvt_ref[...], hb)
    real = lax.broadcasted_iota(jnp.int32, (D_ATTN, rows), 1) < N_META
    vt = jnp.where(real, vt, 0.0).astype(BF16)
    ones = _ones_rows(rows, N_META)
    for h in range(N_HEADS):
        k_ref[h] = k_tiles[h][0:N_META, :]
        vt_ref[h, 0:HEAD_DIM, :] = vt[h * HEAD_DIM:(h + 1) * HEAD_DIM, :]
        vt_ref[h, HEAD_DIM:V_ROWS, :] = ones
    u = _dot(hb, w_ref[:, C_U:C_UG])
    ug = _dot(hb, w_ref[:, C_UG:C_GC])
    glu = u * _sigmoid(ug)
    glu_ref[0:HALO - N_META, :] = jnp.zeros((HALO - N_META, D_CONV), F32)
    glu_ref[HALO - N_META:HALO, :] = glu[0:N_META, :]


def _conv_rows(buf_ref, cw_ref, cb_ref, out_ref, base, zero):
    shift = HALO - (CONV_WIDTH - 1)
    for lo in range(0, D_CONV, LANES):
        cols = slice(lo, lo + LANES)
        out = jnp.broadcast_to(cb_ref[:, cols] + zero, (CONV_ROWS, LANES))
        for r in range(SUBLANES):
            window = CONV_ROWS + (SUBLANES if r else 0)
            part = None
            for j in range(CONV_WIDTH):
                if (shift + j) % SUBLANES != r:
                    continue
                start = base + shift + j - r
                term = cw_ref[j:j + 1, cols] * buf_ref[start:start + window, cols]
                part = term if part is None else part + term
            out = out + part[r:r + CONV_ROWS, :]
        out_ref[base:base + CONV_ROWS, cols] = out


def _proj_body(x_ref, lng_ref, lnb_ref, w_ref, wvt_ref, bf_ref, pm_ref, meta_glu_ref,
               cw_ref, cb_ref, lcg_ref, lcb_ref, wpw_ref, zero_ref,
               q_ref, k_ref, vt_ref, ga_ref, yc_ref,
               carry_ref, tri_ref, buf_ref, conv_ref):
    tm = x_ref.shape[1]
    first = pl.program_id(1) == 0

    @pl.when(first)
    def _start_sequence():
        carry_ref[...] = jnp.zeros_like(carry_ref)
        r = lax.broadcasted_iota(jnp.int32, (tm, tm), 0)
        c = lax.broadcasted_iota(jnp.int32, (tm, tm), 1)
        tri_ref[...] = jnp.where(r >= c, 1.0, 0.0).astype(BF16)
        buf_ref[0:HALO, :] = meta_glu_ref[...]

    @pl.when(jnp.logical_not(first))
    def _carry_conv_context():
        buf_ref[0:HALO, :] = buf_ref[tm:tm + HALO, :]

    def zero_after(tile):
        bits = lax.bitcast_convert_type(tile, jnp.int32) & zero_ref[...]
        return lax.bitcast_convert_type(bits, F32)[0:1, :]

    hb = _layer_norm(x_ref[0], lng_ref[...], lnb_ref[...]).astype(BF16)
    u = _dot(hb, w_ref[:, C_U:C_UG])
    ug = _dot(hb, w_ref[:, C_UG:C_GC])
    buf_ref[HALO:HALO + tm, :] = u * _sigmoid(ug)

    lane = lax.broadcasted_iota(jnp.int32, (tm, LANES), 1)
    lf = _forget_logs(hb, w_ref, bf_ref, lane)
    c = _exact_tri_dot(tri_ref[...], lf) + carry_ref[...]
    carry_ref[...] = c[tm - 1:tm, :]
    cmat = _bias_operand(c, lane)
    ones = _ones_rows(tm, tm)

    def qk_heads(out_ref, col0, first_pair, n_pairs):
        cols = slice(col0 + first_pair * LANES, col0 + (first_pair + n_pairs) * LANES)
        ext = _dot(cmat, pm_ref[:, cols])
        tiles = _head_tiles(_dot(hb, w_ref[:, cols]), ext, lane)
        for h, tile in enumerate(tiles):
            out_ref[0, 2 * first_pair + h] = tile
        return zero_after(ext[tm - SUBLANES:, (n_pairs - 1) * LANES:])

    def v_heads(first_head, n_heads):
        rows = slice(first_head * HEAD_DIM, (first_head + n_heads) * HEAD_DIM)
        vt = _dot_nt(wvt_ref[rows, :], hb)
        for h in range(n_heads):
            vt_ref[0, first_head + h, 0, 0:HEAD_DIM, :] = vt[h * HEAD_DIM:(h + 1) * HEAD_DIM, :].astype(BF16)
            vt_ref[0, first_head + h, 0, HEAD_DIM:V_ROWS, :] = ones
        return zero_after(vt[n_heads * HEAD_DIM - SUBLANES:, tm - LANES:])

    def attn_gate():
        ga = _dot(hb, w_ref[:, C_GA:C_U])
        ga_ref[0] = ga
        return zero_after(ga[tm - SUBLANES:, D_ATTN - LANES:])

    half = N_HEADS // 2
    quarter = N_HEADS // 4
    gates = []

    def conv_gate():
        gates.append(_silu(_dot(hb, w_ref[:, C_GC:C_END])))
        return zero_after(gates[0][tm - SUBLANES:, D_CONV - LANES:])

    mxu_groups = [conv_gate, attn_gate, lambda: v_heads(0, half), lambda: v_heads(half, half),
                  lambda: qk_heads(q_ref, C_Q, 0, quarter), lambda: qk_heads(q_ref, C_Q, quarter, quarter),
                  lambda: qk_heads(k_ref, C_K, 0, quarter), lambda: qk_heads(k_ref, C_K, quarter, quarter)]
    chunks = list(range(0, tm, CONV_ROWS))
    per_chunk = -(-len(mxu_groups) // len(chunks))
    after = jnp.zeros((1, LANES), F32)
    for n, base in enumerate(chunks):
        _conv_rows(buf_ref, cw_ref, cb_ref, conv_ref, base, after)
        for group in mxu_groups[n * per_chunk:(n + 1) * per_chunk]:
            after = group()

    z = _silu(_layer_norm(conv_ref[...], lcg_ref[...], lcb_ref[...]))
    yc_ref[0] = (_dot(z.astype(BF16), wpw_ref[...]) * gates[0]).astype(BF16)


def _attn_body(q_ref, k_ref, vt_ref, km_ref, vmt_ref, ga_ref, o_ref, s_ref, pen_ref, m_ref, acc_ref):
    qi = pl.program_id(2)
    heads = range(HEADS_PER_STEP)

    @pl.when((pl.program_id(0) == 0) & (qi == 0))
    def _causal_penalty():
        key = lax.broadcasted_iota(jnp.int32, (TK, TQ), 0)
        qry = lax.broadcasted_iota(jnp.int32, (TK, TQ), 1)
        pen_ref[0] = jnp.zeros((TK, TQ), F32)
        pen_ref[1] = jnp.where(key <= qry, 0.0, -MASK_BIAS)

    pad = jnp.zeros((META_PAD - N_META, TQ), BF16)
    s_meta = [_dot_nt(km_ref[hh], q_ref[0, hh]) for hh in heads]
    m_meta = [jnp.max(s, axis=0, keepdims=True) for s in s_meta]
    p_meta = [jnp.concatenate([jnp.exp2(s - m).astype(BF16), pad], axis=0) for s, m in zip(s_meta, m_meta)]
    for hh in heads:
        m_ref[hh] = m_meta[hh]
        acc_ref[hh] = _dot(vmt_ref[hh], p_meta[hh])

    def produce(j, hh, penalty):
        off = pl.multiple_of(j * TK, TK)
        s = _dot_nt(k_ref[0, hh, pl.ds(off, TK), :], q_ref[0, hh])
        if penalty is not None:
            s = s + pen_ref[penalty]
        s_ref[hh % SCORE_SLOTS] = s
        return jnp.max(s, axis=0, keepdims=True)

    def consume(j, hh, col_max):
        m = m_ref[hh]
        m_new = jnp.maximum(m, col_max)
        p = jnp.exp2(s_ref[hh % SCORE_SLOTS] - m_new).astype(BF16)
        acc_ref[hh] = acc_ref[hh] * jnp.exp2(m - m_new) + _dot(vt_ref[0, hh, j], p)
        m_ref[hh] = m_new

    def diagonal_flag(j):
        return jnp.where(j < qi, 0, 1)

    def body(j, pending):
        pending = list(pending)
        for hh in heads:
            ahead = hh + LOOKAHEAD
            if ahead < HEADS_PER_STEP:
                pending.append(produce(j, ahead, None))
            else:
                pending.append(produce(j + 1, ahead - HEADS_PER_STEP, diagonal_flag(j + 1)))
            consume(j, hh, pending.pop(0))
        return tuple(pending)

    half = TK // 2

    def produce_diagonal(hh):
        slot = hh % SCORE_SLOTS
        off = pl.multiple_of(qi * TK, TK)
        top = _dot_nt(k_ref[0, hh, pl.ds(off, half), :], q_ref[0, hh]) + pen_ref[1, 0:half, :]
        low = (_dot_nt(k_ref[0, hh, pl.ds(off + half, half), :], q_ref[0, hh, half:, :])
               + pen_ref[1, half:, half:])
        s_ref[slot, 0:half, :] = top
        s_ref[slot, half:, half:] = low
        top_max = jnp.max(top, axis=0, keepdims=True)
        low_max = jnp.max(low, axis=0, keepdims=True)
        return jnp.concatenate([top_max[:, 0:half], jnp.maximum(top_max[:, half:], low_max)], axis=1)

    def consume_diagonal(hh, col_max):
        slot = hh % SCORE_SLOTS
        m = m_ref[hh]
        m_new = jnp.maximum(m, col_max)
        p_top = jnp.exp2(s_ref[slot, 0:half, :] - m_new).astype(BF16)
        p_low = jnp.exp2(s_ref[slot, half:, half:] - m_new[:, half:]).astype(BF16)
        acc = acc_ref[hh] * jnp.exp2(m - m_new) + _dot(vt_ref[0, hh, qi, :, 0:half], p_top)
        acc_ref[hh, :, 0:half] = acc[:, 0:half]
        acc_ref[hh, :, half:] = acc[:, half:] + _dot(vt_ref[0, hh, qi, :, half:], p_low)
        m_ref[hh] = m_new

    def normalized(hh):
        return acc_ref[hh, 0:HEAD_DIM, :] / acc_ref[hh, HEAD_DIM:HEAD_DIM + 1, :]

    pending = lax.fori_loop(0, qi, body, tuple(produce(0, hh, diagonal_flag(0)) for hh in range(LOOKAHEAD)))
    pending = list(pending)
    for hh in heads:
        if hh + LOOKAHEAD < HEADS_PER_STEP:
            pending.append(produce_diagonal(hh + LOOKAHEAD))
        if hh < LOOKAHEAD:
            consume(qi, hh, pending.pop(0))
        else:
            consume_diagonal(hh, pending.pop(0))
        if hh % 2:
            cols = slice((hh - 1) * HEAD_DIM, (hh + 1) * HEAD_DIM)
            pair = jnp.concatenate([normalized(hh - 1), normalized(hh)], axis=0).T
            o_ref[0, :, cols] = (pair * _silu(ga_ref[0, :, cols])).astype(BF16)


def _out_body(x_ref, ya_ref, yc_ref, wout_ref, lig_ref, lib_ref, log_ref, lob_ref, o_ref):
    tm = x_ref.shape[1]
    for r0 in range(0, tm, OUT_ROWS):
        rows = slice(r0, r0 + OUT_ROWS)
        y = _dot(ya_ref[0, rows, :], wout_ref[0:D_ATTN, :]) + _dot(yc_ref[0, rows, :], wout_ref[D_ATTN:, :])
        scaled_h = _layer_norm(x_ref[0, rows, :], lig_ref[...], lib_ref[...])
        o_ref[0, rows, :] = _layer_norm(scaled_h + y, log_ref[...], lob_ref[...])


def _placement():
    pm = np.zeros((LANES, 2 * D_ATTN), np.float32)
    for h in range(N_HEADS):
        base = (h // 2) * LANES + (HEAD_DIM if h % 2 == 0 else 0)
        for g in range(3):
            pm[g * N_HEADS + h, C_Q + base + g] = 1.0
            pm[3 * N_HEADS + h, C_Q + base + 3 + g] = 1.0
            pm[3 * N_HEADS + h, C_K + base + g] = 1.0
            pm[g * N_HEADS + h, C_K + base + 3 + g] = -1.0
    return jnp.asarray(pm, BF16)


def _full(shape):
    return pl.BlockSpec(shape, lambda *_: (0,) * len(shape))


def _call(body, name, vmem_mib, operands, out_shape, grid=(), in_specs=None, out_specs=None, scratch=()):
    assert vmem_mib * 1024 * 1024 <= V7X_VMEM_BYTES
    grid_args = {} if in_specs is None else dict(grid=grid, in_specs=in_specs, out_specs=out_specs)
    return pl.pallas_call(
        body, out_shape=out_shape, scratch_shapes=list(scratch), name=name,
        compiler_params=pltpu.CompilerParams(dimension_semantics=("arbitrary",) * len(grid),
                                             vmem_limit_bytes=vmem_mib * 1024 * 1024),
        **grid_args)(*operands)


def kernel(x, meta, ln_in_g, ln_in_b, w_in, b_f, conv_w, conv_b, ln_conv_g, ln_conv_b,
           w_pw, w_out, ln_out_g, ln_out_b):
    B, S, D = x.shape
    assert D == D_MODEL and w_in.shape[0] == 1 and meta.shape == (N_META, D_MODEL)
    assert S % TM_PROJ == 0 and S % TM_OUT == 0 and S % TQ == 0 and TQ == TK == TM_PROJ

    w = w_in[0]
    scale = HEAD_DIM ** -0.5 * LOG2E
    wf = jnp.pad(jnp.tile(w[:, 3 * D_ATTN:3 * D_ATTN + N_HEADS], (1, 3)), ((0, 0), (0, LANES - 3 * N_HEADS)))
    w_all = jnp.concatenate([
        w[:, 0:D_ATTN] * scale, w[:, D_ATTN:2 * D_ATTN], wf, w[:, 3 * D_ATTN + N_HEADS:]], axis=1).astype(BF16)
    wvt = w[:, 2 * D_ATTN:3 * D_ATTN].T.astype(BF16)
    bf_row = jnp.pad(jnp.tile(b_f[0], 3), (0, LANES - 3 * N_HEADS)).reshape(1, LANES)
    pm = _placement()
    lig = ln_in_g.reshape(1, D)
    lib = ln_in_b.reshape(1, D)
    meta_pad = jnp.pad(meta, ((0, META_PAD - N_META), (0, 0)))

    k_meta, vt_meta, glu_meta = _call(
        _meta_body, "meta_proj", 48, (meta_pad, lig, lib, w_all, wvt, bf_row, pm),
        out_shape=(jax.ShapeDtypeStruct((N_HEADS, N_META, LANES), BF16),
                   jax.ShapeDtypeStruct((N_HEADS, V_ROWS, META_PAD), BF16),
                   jax.ShapeDtypeStruct((HALO, D_CONV), F32)))

    nt = S // TM_PROJ
    head_blk = pl.BlockSpec((1, N_HEADS, TM_PROJ, LANES), lambda b, i: (b, 0, i, 0))
    row_blk = pl.BlockSpec((1, TM_PROJ, D_ATTN), lambda b, i: (b, i, 0))
    qp, kp, vtp, ga, yc = _call(
        _proj_body, "in_proj", 56,
        (x, lig, lib, w_all, wvt, bf_row, pm, glu_meta,
         conv_w[0], conv_b[0].reshape(1, D_CONV), ln_conv_g[0].reshape(1, D_CONV), ln_conv_b[0].reshape(1, D_CONV),
         w_pw[0].astype(BF16), jnp.zeros((SUBLANES, LANES), jnp.int32)),
        grid=(B, nt),
        in_specs=[pl.BlockSpec((1, TM_PROJ, D), lambda b, i: (b, i, 0)),
                  _full((1, D)), _full((1, D)), _full((D, C_END)), _full((D_ATTN, D)), _full((1, LANES)),
                  _full((LANES, 2 * D_ATTN)), _full((HALO, D_CONV)),
                  _full((CONV_WIDTH, D_CONV)), _full((1, D_CONV)), _full((1, D_CONV)), _full((1, D_CONV)),
                  _full((D_CONV, D_CONV)), _full((SUBLANES, LANES))],
        out_specs=(head_blk, head_blk,
                   pl.BlockSpec((1, N_HEADS, 1, V_ROWS, TM_PROJ), lambda b, i: (b, 0, i, 0, 0)),
                   row_blk, row_blk),
        out_shape=(jax.ShapeDtypeStruct((B, N_HEADS, S, LANES), BF16),) * 2
                  + (jax.ShapeDtypeStruct((B, N_HEADS, nt, V_ROWS, TM_PROJ), BF16),
                     jax.ShapeDtypeStruct((B, S, D_ATTN), F32),
                     jax.ShapeDtypeStruct((B, S, D_CONV), BF16)),
        scratch=[pltpu.VMEM((1, LANES), F32), pltpu.VMEM((TM_PROJ, TM_PROJ), BF16),
                 pltpu.VMEM((HALO + TM_PROJ, D_CONV), F32), pltpu.VMEM((TM_PROJ, D_CONV), F32)])

    hp = HEADS_PER_STEP
    att = _call(
        _attn_body, "fox_attn", 56, (qp, kp, vtp, k_meta, vt_meta, ga),
        grid=(B, N_HEADS // hp, S // TQ),
        in_specs=[pl.BlockSpec((1, hp, TQ, LANES), lambda b, h, i: (b, h, i, 0)),
                  pl.BlockSpec((1, hp, S, LANES), lambda b, h, i: (b, h, 0, 0)),
                  pl.BlockSpec((1, hp, nt, V_ROWS, TK), lambda b, h, i: (b, h, 0, 0, 0)),
                  pl.BlockSpec((hp, N_META, LANES), lambda b, h, i: (h, 0, 0)),
                  pl.BlockSpec((hp, V_ROWS, META_PAD), lambda b, h, i: (h, 0, 0)),
                  pl.BlockSpec((1, TQ, hp * HEAD_DIM), lambda b, h, i: (b, i, h))],
        out_specs=pl.BlockSpec((1, TQ, hp * HEAD_DIM), lambda b, h, i: (b, i, h)),
        out_shape=jax.ShapeDtypeStruct((B, S, D_ATTN), BF16),
        scratch=[pltpu.VMEM((SCORE_SLOTS, TK, TQ), F32), pltpu.VMEM((2, TK, TQ), F32),
                 pltpu.VMEM((hp, 1, TQ), F32), pltpu.VMEM((hp, V_ROWS, TQ), F32)])

    tile_blk = pl.BlockSpec((1, TM_OUT, D_ATTN), lambda b, i: (b, i, 0))
    return _call(
        _out_body, "out_proj", 56,
        (x, att, yc, w_out[0].astype(BF16), ALPHA * lig, ALPHA * lib,
         ln_out_g[0].reshape(1, D), ln_out_b[0].reshape(1, D)),
        grid=(B, S // TM_OUT),
        in_specs=[pl.BlockSpec((1, TM_OUT, D), lambda b, i: (b, i, 0)), tile_blk, tile_blk,
                  _full((D, D)), _full((1, D)), _full((1, D)), _full((1, D)), _full((1, D))],
        out_specs=pl.BlockSpec((1, TM_OUT, D), lambda b, i: (b, i, 0)),
        out_shape=jax.ShapeDtypeStruct((B, S, D), F32))
```

```python
import numpy as np

import jax
import jax.numpy as jnp
from jax import lax
from jax.experimental import pallas as pl
from jax.experimental.pallas import tpu as pltpu

F32 = jnp.float32
BF16 = jnp.bfloat16

D_MODEL = 1024
N_META = 16
D_ATTN = 512
D_CONV = 512
N_HEADS = 8
HEAD_DIM = 64
CONV_WIDTH = 31
LN_EPS = 1e-5
ALPHA = 2.0 ** 0.25
V7X_VMEM_BYTES = 64 * 1024 * 1024
LANES = 128
SUBLANES = 8
BF16_ROWS = 16
META_PAD = 128
HALO = 32
MASK_BIAS = 1e30
LOG2E = 1.4426950408889634
V_ROWS = HEAD_DIM + BF16_ROWS

C_Q = 0
C_K = D_ATTN
C_F = 2 * D_ATTN
C_GA = C_F + LANES
C_U = C_GA + D_ATTN
C_UG = C_U + D_CONV
C_GC = C_UG + D_CONV
C_END = C_GC + D_CONV

TM_PROJ = 512
TQ = 512
TK = 512
HEADS_PER_STEP = 8
LOOKAHEAD = 2
SCORE_SLOTS = 4
TM_OUT = 1024
OUT_ROWS = 256
CONV_ROWS = 64


def _dot(a, b):
    return jnp.dot(a, b, preferred_element_type=F32)


def _dot_nt(a, b):
    return lax.dot_general(a, b, (((1,), (1,)), ((), ())), preferred_element_type=F32)


def _layer_norm(x, g, b):
    mu = jnp.mean(x, axis=-1, keepdims=True)
    xc = x - mu
    var = jnp.mean(xc * xc, axis=-1, keepdims=True)
    return xc * lax.rsqrt(var + LN_EPS) * g + b


def _sigmoid(x):
    return 1.0 / (1.0 + jnp.exp(-x))


def _silu(x):
    return x * _sigmoid(x)


def _log_sigmoid(x):
    return jnp.minimum(x, 0.0) - jnp.log1p(jnp.exp(-jnp.abs(x)))


def _split3(c):
    hi = c.astype(BF16).astype(F32)
    r = c - hi
    mid = r.astype(BF16).astype(F32)
    lo = (r - mid).astype(BF16).astype(F32)
    return hi, mid, lo


def _bias_operand(c, lane):
    hi, mid, lo = _split3(c * LOG2E)
    one = jnp.where(lane < 4 * N_HEADS, 1.0, 0.0)
    return jnp.where(lane < N_HEADS, hi,
                     jnp.where(lane < 2 * N_HEADS, mid,
                               jnp.where(lane < 3 * N_HEADS, lo, one))).astype(BF16)


def _forget_logs(hb, w_ref, bf_ref, lane):
    f = _dot(hb, w_ref[:, C_F:C_F + LANES])
    return jnp.where(lane < 3 * N_HEADS, _log_sigmoid(f + bf_ref[...]), 0.0)


def _exact_tri_dot(tri, x):
    hi, mid, lo = _split3(x)
    return _dot(tri, hi.astype(BF16)) + _dot(tri, mid.astype(BF16)) + _dot(tri, lo.astype(BF16))


def _head_tiles(proj, bias_cols, lane):
    low_half = lane < HEAD_DIM
    tiles = []
    for p in range(proj.shape[1] // LANES):
        pair = proj[:, p * LANES:(p + 1) * LANES]
        extra = bias_cols[:, p * LANES:(p + 1) * LANES]
        tiles.append(jnp.where(low_half, pair, extra).astype(BF16))
        tiles.append(pltpu.roll(jnp.where(low_half, extra, pair), HEAD_DIM, axis=1).astype(BF16))
    return tiles


def _ones_rows(cols, valid_cols):
    r = lax.broadcasted_iota(jnp.int32, (BF16_ROWS, cols), 0)
    c = lax.broadcasted_iota(jnp.int32, (BF16_ROWS, cols), 1)
    return jnp.where((r == 0) & (c < valid_cols), 1.0, 0.0).astype(BF16)


def _meta_body(x_ref, lng_ref, lnb_ref, w_ref, wvt_ref, bf_ref, pm_ref,
               k_ref, vt_ref, glu_ref):
    rows = x_ref.shape[0]
    hb = _layer_norm(x_ref[...], lng_ref[...], lnb_ref[...]).astype(BF16)
    lane = lax.broadcasted_iota(jnp.int32, (rows, LANES), 1)
    lf = _forget_logs(hb, w_ref, bf_ref, lane)
    r = lax.broadcasted_iota(jnp.int32, (rows, rows), 0)
    c = lax.broadcasted_iota(jnp.int32, (rows, rows), 1)
    upper = jnp.where((c > r) & (c < N_META), 1.0, 0.0).astype(BF16)
    cmat = _bias_operand(-_exact_tri_dot(upper, lf), lane)
    k_tiles = _head_tiles(_dot(hb, w_ref[:, C_K:C_F]), _dot(cmat, pm_ref[:, C_K:C_F]), lane)
    vt = _dot_nt(wvt_ref[...], hb)
    real = lax.broadcasted_iota(jnp.int32, (D_ATTN, rows), 1) < N_META
    vt = jnp.where(real, vt, 0.0).astype(BF16)
    ones = _ones_rows(rows, N_META)
    for h in range(N_HEADS):
        k_ref[h] = k_tiles[h][0:N_META, :]
        vt_ref[h, 0:HEAD_DIM, :] = vt[h * HEAD_DIM:(h + 1) * HEAD_DIM, :]
        vt_ref[h, HEAD_DIM:V_ROWS, :] = ones
    u = _dot(hb, w_ref[:, C_U:C_UG])
    ug = _dot(hb, w_ref[:, C_UG:C_GC])
    glu = u * _sigmoid(ug)
    glu_ref[0:HALO - N_META, :] = jnp.zeros((HALO - N_META, D_CONV), F32)
    glu_ref[HALO - N_META:HALO, :] = glu[0:N_META, :]


def _conv_rows(buf_ref, cw_ref, cb_ref, out_ref, base, zero):
    shift = HALO - (CONV_WIDTH - 1)
    for lo in range(0, D_CONV, LANES):
        cols = slice(lo, lo + LANES)
        out = jnp.broadcast_to(cb_ref[:, cols] + zero, (CONV_ROWS, LANES))
        for r in range(SUBLANES):
            window = CONV_ROWS + (SUBLANES if r else 0)
            part = None
            for j in range(CONV_WIDTH):
                if (shift + j) % SUBLANES != r:
                    continue
                start = base + shift + j - r
                term = cw_ref[j:j + 1, cols] * buf_ref[start:start + window, cols]
                part = term if part is None else part + term
            out = out + part[r:r + CONV_ROWS, :]
        out_ref[base:base + CONV_ROWS, cols] = out


def _proj_body(x_ref, lng_ref, lnb_ref, w_ref, wvt_ref, bf_ref, pm_ref, meta_glu_ref,
               cw_ref, cb_ref, lcg_ref, lcb_ref, wpw_ref, zero_ref,
               q_ref, k_ref, vt_ref, ga_ref, yc_ref,
               carry_ref, tri_ref, buf_ref, conv_ref):
    tm = x_ref.shape[1]
    first = pl.program_id(1) == 0

    @pl.when(first)
    def _start_sequence():
        carry_ref[...] = jnp.zeros_like(carry_ref)
        r = lax.broadcasted_iota(jnp.int32, (tm, tm), 0)
        c = lax.broadcasted_iota(jnp.int32, (tm, tm), 1)
        tri_ref[...] = jnp.where(r >= c, 1.0, 0.0).astype(BF16)
        buf_ref[0:HALO, :] = meta_glu_ref[...]

    @pl.when(jnp.logical_not(first))
    def _carry_conv_context():
        buf_ref[0:HALO, :] = buf_ref[tm:tm + HALO, :]

    def zero_after(tile):
        bits = lax.bitcast_convert_type(tile, jnp.int32) & zero_ref[...]
        return lax.bitcast_convert_type(bits, F32)[0:1, :]

    hb = _layer_norm(x_ref[0], lng_ref[...], lnb_ref[...]).astype(BF16)
    for lo in range(0, D_CONV, D_CONV // 2):
        u = _dot(hb, w_ref[:, C_U + lo:C_U + lo + D_CONV // 2])
        ug = _dot(hb, w_ref[:, C_UG + lo:C_UG + lo + D_CONV // 2])
        buf_ref[HALO:HALO + tm, lo:lo + D_CONV // 2] = u * _sigmoid(ug)

    lane = lax.broadcasted_iota(jnp.int32, (tm, LANES), 1)
    lf = _forget_logs(hb, w_ref, bf_ref, lane)
    c = _exact_tri_dot(tri_ref[...], lf) + carry_ref[...]
    carry_ref[...] = c[tm - 1:tm, :]
    cmat = _bias_operand(c, lane)
    ones = _ones_rows(tm, tm)

    def qk_heads(out_ref, col0, first_pair, n_pairs):
        cols = slice(col0 + first_pair * LANES, col0 + (first_pair + n_pairs) * LANES)
        ext = _dot(cmat, pm_ref[:, cols])
        tiles = _head_tiles(_dot(hb, w_ref[:, cols]), ext, lane)
        for h, tile in enumerate(tiles):
            out_ref[0, 2 * first_pair + h] = tile
        return zero_after(ext[tm - SUBLANES:, (n_pairs - 1) * LANES:])

    def v_heads(first_head, n_heads):
        rows = slice(first_head * HEAD_DIM, (first_head + n_heads) * HEAD_DIM)
        vt = _dot_nt(wvt_ref[rows, :], hb)
        for h in range(n_heads):
            vt_ref[0, first_head + h, 0, 0:HEAD_DIM, :] = vt[h * HEAD_DIM:(h + 1) * HEAD_DIM, :].astype(BF16)
            vt_ref[0, first_head + h, 0, HEAD_DIM:V_ROWS, :] = ones
        return zero_after(vt[n_heads * HEAD_DIM - SUBLANES:, tm - LANES:])

    def attn_gate():
        ga = _dot(hb, w_ref[:, C_GA:C_U])
        ga_ref[0] = ga
        return zero_after(ga[tm - SUBLANES:, D_ATTN - LANES:])

    half = N_HEADS // 2
    quarter = N_HEADS // 4
    gates = []

    def conv_gate():
        gates.append(_silu(_dot(hb, w_ref[:, C_GC:C_END])))
        return zero_after(gates[0][tm - SUBLANES:, D_CONV - LANES:])

    mxu_groups = [conv_gate, attn_gate, lambda: v_heads(0, half), lambda: v_heads(half, half),
                  lambda: qk_heads(q_ref, C_Q, 0, quarter), lambda: qk_heads(q_ref, C_Q, quarter, quarter),
                  lambda: qk_heads(k_ref, C_K, 0, quarter), lambda: qk_heads(k_ref, C_K, quarter, quarter)]
    chunks = list(range(0, tm, CONV_ROWS))
    per_chunk = -(-len(mxu_groups) // len(chunks))
    after = jnp.zeros((1, LANES), F32)
    for n, base in enumerate(chunks):
        _conv_rows(buf_ref, cw_ref, cb_ref, conv_ref, base, after)
        for group in mxu_groups[n * per_chunk:(n + 1) * per_chunk]:
            after = group()

    z = _silu(_layer_norm(conv_ref[...], lcg_ref[...], lcb_ref[...]))
    yc_ref[0] = (_dot(z.astype(BF16), wpw_ref[...]) * gates[0]).astype(BF16)


def _attn_body(q_ref, k_ref, vt_ref, km_ref, vmt_ref, ga_ref, o_ref, s_ref, pen_ref, m_ref, acc_ref):
    qi = pl.program_id(2)
    heads = range(HEADS_PER_STEP)

    @pl.when((pl.program_id(0) == 0) & (qi == 0))
    def _causal_penalty():
        key = lax.broadcasted_iota(jnp.int32, (TK, TQ), 0)
        qry = lax.broadcasted_iota(jnp.int32, (TK, TQ), 1)
        pen_ref[0] = jnp.zeros((TK, TQ), F32)
        pen_ref[1] = jnp.where(key <= qry, 0.0, -MASK_BIAS)

    pad = jnp.zeros((META_PAD - N_META, TQ), BF16)
    s_meta = [_dot_nt(km_ref[hh], q_ref[0, hh]) for hh in heads]
    m_meta = [jnp.max(s, axis=0, keepdims=True) for s in s_meta]
    p_meta = [jnp.concatenate([jnp.exp2(s - m).astype(BF16), pad], axis=0) for s, m in zip(s_meta, m_meta)]
    for hh in heads:
        m_ref[hh] = m_meta[hh]
        acc_ref[hh] = _dot(vmt_ref[hh], p_meta[hh])

    def produce(j, hh, penalty):
        off = pl.multiple_of(j * TK, TK)
        s = _dot_nt(k_ref[0, hh, pl.ds(off, TK), :], q_ref[0, hh])
        if penalty is not None:
            s = s + pen_ref[penalty]
        s_ref[hh % SCORE_SLOTS] = s
        return jnp.max(s, axis=0, keepdims=True)

    def consume(j, hh, col_max):
        m = m_ref[hh]
        m_new = jnp.maximum(m, col_max)
        p = jnp.exp2(s_ref[hh % SCORE_SLOTS] - m_new).astype(BF16)
        acc_ref[hh] = acc_ref[hh] * jnp.exp2(m - m_new) + _dot(vt_ref[0, hh, j], p)
        m_ref[hh] = m_new

    def diagonal_flag(j):
        return jnp.where(j < qi, 0, 1)

    def body(j, pending):
        pending = list(pending)
        for hh in heads:
            ahead = hh + LOOKAHEAD
            if ahead < HEADS_PER_STEP:
                pending.append(produce(j, ahead, None))
            else:
                pending.append(produce(j + 1, ahead - HEADS_PER_STEP, diagonal_flag(j + 1)))
            consume(j, hh, pending.pop(0))
        return tuple(pending)

    half = TK // 2

    def produce_diagonal(hh):
        slot = hh % SCORE_SLOTS
        off = pl.multiple_of(qi * TK, TK)
        top = _dot_nt(k_ref[0, hh, pl.ds(off, half), :], q_ref[0, hh]) + pen_ref[1, 0:half, :]
        low = (_dot_nt(k_ref[0, hh, pl.ds(off + half, half), :], q_ref[0, hh, half:, :])
               + pen_ref[1, half:, half:])
        s_ref[slot, 0:half, :] = top
        s_ref[slot, half:, half:] = low
        top_max = jnp.max(top, axis=0, keepdims=True)
        low_max = jnp.max(low, axis=0, keepdims=True)
        return jnp.concatenate([top_max[:, 0:half], jnp.maximum(top_max[:, half:], low_max)], axis=1)

    def consume_diagonal(hh, col_max):
        slot = hh % SCORE_SLOTS
        m = m_ref[hh]
        m_new = jnp.maximum(m, col_max)
        p_top = jnp.exp2(s_ref[slot, 0:half, :] - m_new).astype(BF16)
        p_low = jnp.exp2(s_ref[slot, half:, half:] - m_new[:, half:]).astype(BF16)
        acc = acc_ref[hh] * jnp.exp2(m - m_new) + _dot(vt_ref[0, hh, qi, :, 0:half], p_top)
        acc_ref[hh, :, 0:half] = acc[:, 0:half]
        acc_ref[hh, :, half:] = acc[:, half:] + _dot(vt_ref[0, hh, qi, :, half:], p_low)
        m_ref[hh] = m_new

    def normalized(hh):
        return acc_ref[hh, 0:HEAD_DIM, :] / acc_ref[hh, HEAD_DIM:HEAD_DIM + 1, :]

    pending = lax.fori_loop(0, qi, body, tuple(produce(0, hh, diagonal_flag(0)) for hh in range(LOOKAHEAD)))
    pending = list(pending)
    for hh in heads:
        if hh + LOOKAHEAD < HEADS_PER_STEP:
            pending.append(produce_diagonal(hh + LOOKAHEAD))
        if hh < LOOKAHEAD:
            consume(qi, hh, pending.pop(0))
        else:
            consume_diagonal(hh, pending.pop(0))
        if hh % 2:
            cols = slice((hh - 1) * HEAD_DIM, (hh + 1) * HEAD_DIM)
            pair = jnp.concatenate([normalized(hh - 1), normalized(hh)], axis=0).T
            o_ref[0, :, cols] = (pair * _silu(ga_ref[0, :, cols])).astype(BF16)


def _out_body(x_ref, ya_ref, yc_ref, wout_ref, lig_ref, lib_ref, log_ref, lob_ref, o_ref):
    tm = x_ref.shape[1]
    for r0 in range(0, tm, OUT_ROWS):
        rows = slice(r0, r0 + OUT_ROWS)
        y = _dot(ya_ref[0, rows, :], wout_ref[0:D_ATTN, :]) + _dot(yc_ref[0, rows, :], wout_ref[D_ATTN:, :])
        scaled_h = _layer_norm(x_ref[0, rows, :], lig_ref[...], lib_ref[...])
        o_ref[0, rows, :] = _layer_norm(scaled_h + y, log_ref[...], lob_ref[...])


def _placement():
    pm = np.zeros((LANES, 2 * D_ATTN), np.float32)
    for h in range(N_HEADS):
        base = (h // 2) * LANES + (HEAD_DIM if h % 2 == 0 else 0)
        for g in range(3):
            pm[g * N_HEADS + h, C_Q + base + g] = 1.0
            pm[3 * N_HEADS + h, C_Q + base + 3 + g] = 1.0
            pm[3 * N_HEADS + h, C_K + base + g] = 1.0
            pm[g * N_HEADS + h, C_K + base + 3 + g] = -1.0
    return jnp.asarray(pm, BF16)


def _full(shape):
    return pl.BlockSpec(shape, lambda *_: (0,) * len(shape))


def _call(body, name, vmem_mib, operands, out_shape, grid=(), in_specs=None, out_specs=None, scratch=()):
    assert vmem_mib * 1024 * 1024 <= V7X_VMEM_BYTES
    grid_args = {} if in_specs is None else dict(grid=grid, in_specs=in_specs, out_specs=out_specs)
    return pl.pallas_call(
        body, out_shape=out_shape, scratch_shapes=list(scratch), name=name,
        compiler_params=pltpu.CompilerParams(dimension_semantics=("arbitrary",) * len(grid),
                                             vmem_limit_bytes=vmem_mib * 1024 * 1024),
        **grid_args)(*operands)


def kernel(x, meta, ln_in_g, ln_in_b, w_in, b_f, conv_w, conv_b, ln_conv_g, ln_conv_b,
           w_pw, w_out, ln_out_g, ln_out_b):
    B, S, D = x.shape
    assert D == D_MODEL and w_in.shape[0] == 1 and meta.shape == (N_META, D_MODEL)
    assert S % TM_PROJ == 0 and S % TM_OUT == 0 and S % TQ == 0 and TQ == TK == TM_PROJ

    w = w_in[0]
    scale = HEAD_DIM ** -0.5 * LOG2E
    wf = jnp.pad(jnp.tile(w[:, 3 * D_ATTN:3 * D_ATTN + N_HEADS], (1, 3)), ((0, 0), (0, LANES - 3 * N_HEADS)))
    w_all = jnp.concatenate([
        w[:, 0:D_ATTN] * scale, w[:, D_ATTN:2 * D_ATTN], wf, w[:, 3 * D_ATTN + N_HEADS:]], axis=1).astype(BF16)
    wvt = w[:, 2 * D_ATTN:3 * D_ATTN].T.astype(BF16)
    bf_row = jnp.pad(jnp.tile(b_f[0], 3), (0, LANES - 3 * N_HEADS)).reshape(1, LANES)
    pm = _placement()
    lig = ln_in_g.reshape(1, D)
    lib = ln_in_b.reshape(1, D)
    meta_pad = jnp.pad(meta, ((0, META_PAD - N_META), (0, 0)))

    k_meta, vt_meta, glu_meta = _call(
        _meta_body, "meta_proj", 48, (meta_pad, lig, lib, w_all, wvt, bf_row, pm),
        out_shape=(jax.ShapeDtypeStruct((N_HEADS, N_META, LANES), BF16),
                   jax.ShapeDtypeStruct((N_HEADS, V_ROWS, META_PAD), BF16),
                   jax.ShapeDtypeStruct((HALO, D_CONV), F32)))

    nt = S // TM_PROJ
    head_blk = pl.BlockSpec((1, N_HEADS, TM_PROJ, LANES), lambda b, i: (b, 0, i, 0))
    row_blk = pl.BlockSpec((1, TM_PROJ, D_ATTN), lambda b, i: (b, i, 0))
    qp, kp, vtp, ga, yc = _call(
        _proj_body, "in_proj", 56,
        (x, lig, lib, w_all, wvt, bf_row, pm, glu_meta,
         conv_w[0], conv_b[0].reshape(1, D_CONV), ln_conv_g[0].reshape(1, D_CONV), ln_conv_b[0].reshape(1, D_CONV),
         w_pw[0].astype(BF16), jnp.zeros((SUBLANES, LANES), jnp.int32)),
        grid=(B, nt),
        in_specs=[pl.BlockSpec((1, TM_PROJ, D), lambda b, i: (b, i, 0)),
                  _full((1, D)), _full((1, D)), _full((D, C_END)), _full((D_ATTN, D)), _full((1, LANES)),
                  _full((LANES, 2 * D_ATTN)), _full((HALO, D_CONV)),
                  _full((CONV_WIDTH, D_CONV)), _full((1, D_CONV)), _full((1, D_CONV)), _full((1, D_CONV)),
                  _full((D_CONV, D_CONV)), _full((SUBLANES, LANES))],
        out_specs=(head_blk, head_blk,
                   pl.BlockSpec((1, N_HEADS, 1, V_ROWS, TM_PROJ), lambda b, i: (b, 0, i, 0, 0)),
                   row_blk, row_blk),
        out_shape=(jax.ShapeDtypeStruct((B, N_HEADS, S, LANES), BF16),) * 2
                  + (jax.ShapeDtypeStruct((B, N_HEADS, nt, V_ROWS, TM_PROJ), BF16),
                     jax.ShapeDtypeStruct((B, S, D_ATTN), F32),
                     jax.ShapeDtypeStruct((B, S, D_CONV), BF16)),
        scratch=[pltpu.VMEM((1, LANES), F32), pltpu.VMEM((TM_PROJ, TM_PROJ), BF16),
                 pltpu.VMEM((HALO + TM_PROJ, D_CONV), F32), pltpu.VMEM((TM_PROJ, D_CONV), F32)])

    hp = HEADS_PER_STEP
    att = _call(
        _attn_body, "fox_attn", 56, (qp, kp, vtp, k_meta, vt_meta, ga),
        grid=(B, N_HEADS // hp, S // TQ),
        in_specs=[pl.BlockSpec((1, hp, TQ, LANES), lambda b, h, i: (b, h, i, 0)),
                  pl.BlockSpec((1, hp, S, LANES), lambda b, h, i: (b, h, 0, 0)),
                  pl.BlockSpec((1, hp, nt, V_ROWS, TK), lambda b, h, i: (b, h, 0, 0, 0)),
                  pl.BlockSpec((hp, N_META, LANES), lambda b, h, i: (h, 0, 0)),
                  pl.BlockSpec((hp, V_ROWS, META_PAD), lambda b, h, i: (h, 0, 0)),
                  pl.BlockSpec((1, TQ, hp * HEAD_DIM), lambda b, h, i: (b, i, h))],
        out_specs=pl.BlockSpec((1, TQ, hp * HEAD_DIM), lambda b, h, i: (b, i, h)),
        out_shape=jax.ShapeDtypeStruct((B, S, D_ATTN), BF16),
        scratch=[pltpu.VMEM((SCORE_SLOTS, TK, TQ), F32), pltpu.VMEM((2, TK, TQ), F32),
                 pltpu.VMEM((hp, 1, TQ), F32), pltpu.VMEM((hp, V_ROWS, TQ), F32)])

    tile_blk = pl.BlockSpec((1, TM_OUT, D_ATTN), lambda b, i: (b, i, 0))
    return _call(
        _out_body, "out_proj", 56,
        (x, att, yc, w_out[0].astype(BF16), ALPHA * lig, ALPHA * lib,
         ln_out_g[0].reshape(1, D), ln_out_b[0].reshape(1, D)),
        grid=(B, S // TM_OUT),
        in_specs=[pl.BlockSpec((1, TM_OUT, D), lambda b, i: (b, i, 0)), tile_blk, tile_blk,
                  _full((D, D)), _full((1, D)), _full((1, D)), _full((1, D)), _full((1, D))],
        out_specs=pl.BlockSpec((1, TM_OUT, D), lambda b, i: (b, i, 0)),
        out_shape=jax.ShapeDtypeStruct((B, S, D), F32))
```

```python
import numpy as np

import jax
import jax.numpy as jnp
from jax import lax
from jax.experimental import pallas as pl
from jax.experimental.pallas import tpu as pltpu

F32 = jnp.float32
BF16 = jnp.bfloat16

D_MODEL = 1024
N_META = 16
D_ATTN = 512
D_CONV = 512
N_HEADS = 8
HEAD_DIM = 64
CONV_WIDTH = 31
LN_EPS = 1e-5
ALPHA = 2.0 ** 0.25
V7X_VMEM_BYTES = 64 * 1024 * 1024
LANES = 128
SUBLANES = 8
BF16_ROWS = 16
META_PAD = 128
HALO = 32
MASK_BIAS = 1e30
LOG2E = 1.4426950408889634
V_ROWS = HEAD_DIM + BF16_ROWS

C_Q = 0
C_K = D_ATTN
C_F = 2 * D_ATTN
C_GA = C_F + LANES
C_U = C_GA + D_ATTN
C_UG = C_U + D_CONV
C_GC = C_UG + D_CONV
C_END = C_GC + D_CONV

TM_PROJ = 512
TQ = 512
TK = 512
HEADS_PER_STEP = 8
LOOKAHEAD = 2
SCORE_SLOTS = 4
TM_OUT = 1024
OUT_ROWS = 256
CONV_ROWS = 64


def _dot(a, b):
    return jnp.dot(a, b, preferred_element_type=F32)


def _dot_nt(a, b):
    return lax.dot_general(a, b, (((1,), (1,)), ((), ())), preferred_element_type=F32)


def _layer_norm(x, g, b):
    mu = jnp.mean(x, axis=-1, keepdims=True)
    xc = x - mu
    var = jnp.mean(xc * xc, axis=-1, keepdims=True)
    return xc * lax.rsqrt(var + LN_EPS) * g + b


def _sigmoid(x):
    return 1.0 / (1.0 + jnp.exp(-x))


def _silu(x):
    return x * _sigmoid(x)


def _log_sigmoid(x):
    return jnp.minimum(x, 0.0) - jnp.log1p(jnp.exp(-jnp.abs(x)))


def _split3(c):
    hi = c.astype(BF16).astype(F32)
    r = c - hi
    mid = r.astype(BF16).astype(F32)
    lo = (r - mid).astype(BF16).astype(F32)
    return hi, mid, lo


def _bias_operand(c, lane):
    hi, mid, lo = _split3(c * LOG2E)
    one = jnp.where(lane < 4 * N_HEADS, 1.0, 0.0)
    return jnp.where(lane < N_HEADS, hi,
                     jnp.where(lane < 2 * N_HEADS, mid,
                               jnp.where(lane < 3 * N_HEADS, lo, one))).astype(BF16)


def _forget_logs(hb, w_ref, bf_ref, lane):
    f = _dot(hb, w_ref[:, C_F:C_F + LANES])
    return jnp.where(lane < 3 * N_HEADS, _log_sigmoid(f + bf_ref[...]), 0.0)


def _exact_tri_dot(tri, x):
    hi, mid, lo = _split3(x)
    return _dot(tri, hi.astype(BF16)) + _dot(tri, mid.astype(BF16)) + _dot(tri, lo.astype(BF16))


def _head_tiles(proj, bias_cols, lane):
    low_half = lane < HEAD_DIM
    tiles = []
    for p in range(proj.shape[1] // LANES):
        pair = proj[:, p * LANES:(p + 1) * LANES]
        extra = bias_cols[:, p * LANES:(p + 1) * LANES]
        tiles.append(jnp.where(low_half, pair, extra).astype(BF16))
        tiles.append(pltpu.roll(jnp.where(low_half, extra, pair), HEAD_DIM, axis=1).astype(BF16))
    return tiles


def _ones_rows(cols, valid_cols):
    r = lax.broadcasted_iota(jnp.int32, (BF16_ROWS, cols), 0)
    c = lax.broadcasted_iota(jnp.int32, (BF16_ROWS, cols), 1)
    return jnp.where((r == 0) & (c < valid_cols), 1.0, 0.0).astype(BF16)


def _meta_body(x_ref, lng_ref, lnb_ref, w_ref, wvt_ref, bf_ref, pm_ref,
               k_ref, vt_ref, glu_ref):
    rows = x_ref.shape[0]
    hb = _layer_norm(x_ref[...], lng_ref[...], lnb_ref[...]).astype(BF16)
    lane = lax.broadcasted_iota(jnp.int32, (rows, LANES), 1)
    lf = _forget_logs(hb, w_ref, bf_ref, lane)
    r = lax.broadcasted_iota(jnp.int32, (rows, rows), 0)
    c = lax.broadcasted_iota(jnp.int32, (rows, rows), 1)
    upper = jnp.where((c > r) & (c < N_META), 1.0, 0.0).astype(BF16)
    cmat = _bias_operand(-_exact_tri_dot(upper, lf), lane)
    k_tiles = _head_tiles(_dot(hb, w_ref[:, C_K:C_F]), _dot(cmat, pm_ref[:, C_K:C_F]), lane)
    vt = _dot_nt(wvt_ref[...], hb)
    real = lax.broadcasted_iota(jnp.int32, (D_ATTN, rows), 1) < N_META
    vt = jnp.where(real, vt, 0.0).astype(BF16)
    ones = _ones_rows(rows, N_META)
    for h in range(N_HEADS):
        k_ref[h] = k_tiles[h][0:N_META, :]
        vt_ref[h, 0:HEAD_DIM, :] = vt[h * HEAD_DIM:(h + 1) * HEAD_DIM, :]
        vt_ref[h, HEAD_DIM:V_ROWS, :] = ones
    u = _dot(hb, w_ref[:, C_U:C_UG])
    ug = _dot(hb, w_ref[:, C_UG:C_GC])
    glu = u * _sigmoid(ug)
    glu_ref[0:HALO - N_META, :] = jnp.zeros((HALO - N_META, D_CONV), F32)
    glu_ref[HALO - N_META:HALO, :] = glu[0:N_META, :]


def _conv_rows(buf_ref, cw_ref, cb_ref, out_ref, base, zero):
    shift = HALO - (CONV_WIDTH - 1)
    for lo in range(0, D_CONV, LANES):
        cols = slice(lo, lo + LANES)
        out = jnp.broadcast_to(cb_ref[:, cols] + zero, (CONV_ROWS, LANES))
        for r in range(SUBLANES):
            window = CONV_ROWS + (SUBLANES if r else 0)
            part = None
            for j in range(CONV_WIDTH):
                if (shift + j) % SUBLANES != r:
                    continue
                start = base + shift + j - r
                term = cw_ref[j:j + 1, cols] * buf_ref[start:start + window, cols]
                part = term if part is None else part + term
            out = out + part[r:r + CONV_ROWS, :]
        out_ref[base:base + CONV_ROWS, cols] = out


def _proj_body(x_ref, lng_ref, lnb_ref, w_ref, wvt_ref, bf_ref, pm_ref, meta_glu_ref,
               cw_ref, cb_ref, lcg_ref, lcb_ref, wpw_ref, zero_ref,
               q_ref, k_ref, vt_ref, ga_ref, yc_ref,
               carry_ref, tri_ref, buf_ref, conv_ref):
    tm = x_ref.shape[1]
    first = pl.program_id(1) == 0

    @pl.when(first)
    def _start_sequence():
        carry_ref[...] = jnp.zeros_like(carry_ref)
        r = lax.broadcasted_iota(jnp.int32, (tm, tm), 0)
        c = lax.broadcasted_iota(jnp.int32, (tm, tm), 1)
        tri_ref[...] = jnp.where(r >= c, 1.0, 0.0).astype(BF16)
        buf_ref[0:HALO, :] = meta_glu_ref[...]

    @pl.when(jnp.logical_not(first))
    def _carry_conv_context():
        buf_ref[0:HALO, :] = buf_ref[tm:tm + HALO, :]

    def zero_after(tile):
        bits = lax.bitcast_convert_type(tile, jnp.int32) & zero_ref[...]
        return lax.bitcast_convert_type(bits, F32)[0:1, :]

    hb = _layer_norm(x_ref[0], lng_ref[...], lnb_ref[...]).astype(BF16)
    for lo in range(0, D_CONV, D_CONV // 2):
        u = _dot(hb, w_ref[:, C_U + lo:C_U + lo + D_CONV // 2])
        ug = _dot(hb, w_ref[:, C_UG + lo:C_UG + lo + D_CONV // 2])
        buf_ref[HALO:HALO + tm, lo:lo + D_CONV // 2] = u * _sigmoid(ug)

    lane = lax.broadcasted_iota(jnp.int32, (tm, LANES), 1)
    lf = _forget_logs(hb, w_ref, bf_ref, lane)
    c = _exact_tri_dot(tri_ref[...], lf) + carry_ref[...]
    carry_ref[...] = c[tm - 1:tm, :]
    cmat = _bias_operand(c, lane)
    ones = _ones_rows(tm, tm)

    def qk_heads(out_ref, col0, first_pair, n_pairs):
        cols = slice(col0 + first_pair * LANES, col0 + (first_pair + n_pairs) * LANES)
        ext = _dot(cmat, pm_ref[:, cols])
        tiles = _head_tiles(_dot(hb, w_ref[:, cols]), ext, lane)
        for h, tile in enumerate(tiles):
            out_ref[0, 2 * first_pair + h] = tile
        return zero_after(ext[tm - SUBLANES:, (n_pairs - 1) * LANES:])

    def v_heads(first_head, n_heads):
        rows = slice(first_head * HEAD_DIM, (first_head + n_heads) * HEAD_DIM)
        vt = _dot_nt(wvt_ref[rows, :], hb)
        for h in range(n_heads):
            vt_ref[0, first_head + h, 0, 0:HEAD_DIM, :] = vt[h * HEAD_DIM:(h + 1) * HEAD_DIM, :].astype(BF16)
            vt_ref[0, first_head + h, 0, HEAD_DIM:V_ROWS, :] = ones
        return zero_after(vt[n_heads * HEAD_DIM - SUBLANES:, tm - LANES:])

    def attn_gate():
        ga = _dot(hb, w_ref[:, C_GA:C_U])
        ga_ref[0] = ga
        return zero_after(ga[tm - SUBLANES:, D_ATTN - LANES:])

    half = N_HEADS // 2
    quarter = N_HEADS // 4
    gates = []

    def conv_gate():
        gates.append(_silu(_dot(hb, w_ref[:, C_GC:C_END])))
        return zero_after(gates[0][tm - SUBLANES:, D_CONV - LANES:])

    mxu_groups = [conv_gate, attn_gate, lambda: v_heads(0, half), lambda: v_heads(half, half),
                  lambda: qk_heads(q_ref, C_Q, 0, quarter), lambda: qk_heads(q_ref, C_Q, quarter, quarter),
                  lambda: qk_heads(k_ref, C_K, 0, quarter), lambda: qk_heads(k_ref, C_K, quarter, quarter)]
    chunks = list(range(0, tm, CONV_ROWS))
    per_chunk = -(-len(mxu_groups) // len(chunks))
    after = jnp.zeros((1, LANES), F32)
    for n, base in enumerate(chunks):
        _conv_rows(buf_ref, cw_ref, cb_ref, conv_ref, base, after)
        for group in mxu_groups[n * per_chunk:(n + 1) * per_chunk]:
            after = group()

    z = _silu(_layer_norm(conv_ref[...], lcg_ref[...], lcb_ref[...]))
    yc_ref[0] = (_dot(z.astype(BF16), wpw_ref[...]) * gates[0]).astype(BF16)


def _attn_body(q_ref, k_ref, vt_ref, km_ref, vmt_ref, ga_ref, o_ref, s_ref, pen_ref, m_ref, acc_ref):
    qi = pl.program_id(2)
    heads = range(HEADS_PER_STEP)

    @pl.when((pl.program_id(0) == 0) & (qi == 0))
    def _causal_penalty():
        key = lax.broadcasted_iota(jnp.int32, (TK, TQ), 0)
        qry = lax.broadcasted_iota(jnp.int32, (TK, TQ), 1)
        pen_ref[0] = jnp.zeros((TK, TQ), F32)
        pen_ref[1] = jnp.where(key <= qry, 0.0, -MASK_BIAS)

    pad = jnp.zeros((META_PAD - N_META, TQ), BF16)
    s_meta = [_dot_nt(km_ref[hh], q_ref[0, hh]) for hh in heads]
    m_meta = [jnp.max(s, axis=0, keepdims=True) for s in s_meta]
    p_meta = [jnp.concatenate([jnp.exp2(s - m).astype(BF16), pad], axis=0) for s, m in zip(s_meta, m_meta)]
    for hh in heads:
        m_ref[hh] = m_meta[hh]
        acc_ref[hh] = _dot(vmt_ref[hh], p_meta[hh])

    def produce(j, hh, penalty):
        off = pl.multiple_of(j * TK, TK)
        s = _dot_nt(k_ref[0, hh, pl.ds(off, TK), :], q_ref[0, hh])
        if penalty is not None:
            s = s + pen_ref[penalty]
        s_ref[hh % SCORE_SLOTS] = s
        return jnp.max(s, axis=0, keepdims=True)

    def consume(j, hh, col_max):
        m = m_ref[hh]
        m_new = jnp.maximum(m, col_max)
        p = jnp.exp2(s_ref[hh % SCORE_SLOTS] - m_new).astype(BF16)
        acc_ref[hh] = acc_ref[hh] * jnp.exp2(m - m_new) + _dot(vt_ref[0, hh, j], p)
        m_ref[hh] = m_new

    def diagonal_flag(j):
        return jnp.where(j < qi, 0, 1)

    def body(j, pending):
        pending = list(pending)
        for hh in heads:
            ahead = hh + LOOKAHEAD
            if ahead < HEADS_PER_STEP:
                pending.append(produce(j, ahead, None))
            else:
                pending.append(produce(j + 1, ahead - HEADS_PER_STEP, diagonal_flag(j + 1)))
            consume(j, hh, pending.pop(0))
        return tuple(pending)

    half = TK // 2

    def produce_diagonal(hh):
        slot = hh % SCORE_SLOTS
        off = pl.multiple_of(qi * TK, TK)
        top = _dot_nt(k_ref[0, hh, pl.ds(off, half), :], q_ref[0, hh]) + pen_ref[1, 0:half, :]
        low = (_dot_nt(k_ref[0, hh, pl.ds(off + half, half), :], q_ref[0, hh, half:, :])
               + pen_ref[1, half:, half:])
        s_ref[slot, 0:half, :] = top
        s_ref[slot, half:, half:] = low
        top_max = jnp.max(top, axis=0, keepdims=True)
        low_max = jnp.max(low, axis=0, keepdims=True)
        return jnp.concatenate([top_max[:, 0:half], jnp.maximum(top_max[:, half:], low_max)], axis=1)

    def consume_diagonal(hh, col_max):
        slot = hh % SCORE_SLOTS
        m = m_ref[hh]
        m_new = jnp.maximum(m, col_max)
        p_top = jnp.exp2(s_ref[slot, 0:half, :] - m_new).astype(BF16)
        p_low = jnp.exp2(s_ref[slot, half:, half:] - m_new[:, half:]).astype(BF16)
        acc = acc_ref[hh] * jnp.exp2(m - m_new) + _dot(vt_ref[0, hh, qi, :, 0:half], p_top)
        acc_ref[hh, :, 0:half] = acc[:, 0:half]
        acc_ref[hh, :, half:] = acc[:, half:] + _dot(vt_ref[0, hh, qi, :, half:], p_low)
        m_ref[hh] = m_new

    def normalized(hh):
        return acc_ref[hh, 0:HEAD_DIM, :] / acc_ref[hh, HEAD_DIM:HEAD_DIM + 1, :]

    pending = lax.fori_loop(0, qi, body, tuple(produce(0, hh, diagonal_flag(0)) for hh in range(LOOKAHEAD)))
    pending = list(pending)
    for hh in heads:
        if hh + LOOKAHEAD < HEADS_PER_STEP:
            pending.append(produce_diagonal(hh + LOOKAHEAD))
        if hh < LOOKAHEAD:
            consume(qi, hh, pending.pop(0))
        else:
            consume_diagonal(hh, pending.pop(0))
        if hh % 2:
            cols = slice((hh - 1) * HEAD_DIM, (hh + 1) * HEAD_DIM)
            pair = jnp.concatenate([normalized(hh - 1), normalized(hh)], axis=0).T
            o_ref[0, :, cols] = (pair * _silu(ga_ref[0, :, cols])).astype(BF16)


def _out_body(x_ref, ya_ref, yc_ref, wout_ref, lig_ref, lib_ref, log_ref, lob_ref, o_ref):
    tm = x_ref.shape[1]
    for r0 in range(0, tm, OUT_ROWS):
        rows = slice(r0, r0 + OUT_ROWS)
        y = _dot(jnp.concatenate([ya_ref[0, rows, :], yc_ref[0, rows, :]], axis=1), wout_ref[...])
        scaled_h = _layer_norm(x_ref[0, rows, :], lig_ref[...], lib_ref[...])
        o_ref[0, rows, :] = _layer_norm(scaled_h + y, log_ref[...], lob_ref[...])


def _placement():
    pm = np.zeros((LANES, 2 * D_ATTN), np.float32)
    for h in range(N_HEADS):
        base = (h // 2) * LANES + (HEAD_DIM if h % 2 == 0 else 0)
        for g in range(3):
            pm[g * N_HEADS + h, C_Q + base + g] = 1.0
            pm[3 * N_HEADS + h, C_Q + base + 3 + g] = 1.0
            pm[3 * N_HEADS + h, C_K + base + g] = 1.0
            pm[g * N_HEADS + h, C_K + base + 3 + g] = -1.0
    return jnp.asarray(pm, BF16)


def _full(shape):
    return pl.BlockSpec(shape, lambda *_: (0,) * len(shape))


def _call(body, name, vmem_mib, operands, out_shape, grid=(), in_specs=None, out_specs=None, scratch=()):
    assert vmem_mib * 1024 * 1024 <= V7X_VMEM_BYTES
    grid_args = {} if in_specs is None else dict(grid=grid, in_specs=in_specs, out_specs=out_specs)
    return pl.pallas_call(
        body, out_shape=out_shape, scratch_shapes=list(scratch), name=name,
        compiler_params=pltpu.CompilerParams(dimension_semantics=("arbitrary",) * len(grid),
                                             vmem_limit_bytes=vmem_mib * 1024 * 1024),
        **grid_args)(*operands)


def kernel(x, meta, ln_in_g, ln_in_b, w_in, b_f, conv_w, conv_b, ln_conv_g, ln_conv_b,
           w_pw, w_out, ln_out_g, ln_out_b):
    B, S, D = x.shape
    assert D == D_MODEL and w_in.shape[0] == 1 and meta.shape == (N_META, D_MODEL)
    assert S % TM_PROJ == 0 and S % TM_OUT == 0 and S % TQ == 0 and TQ == TK == TM_PROJ
    assert TM_OUT % OUT_ROWS == 0 and TM_PROJ % CONV_ROWS == 0 and N_HEADS % HEADS_PER_STEP == 0
    assert HEADS_PER_STEP % SCORE_SLOTS == 0 and SCORE_SLOTS > LOOKAHEAD

    w = w_in[0]
    scale = HEAD_DIM ** -0.5 * LOG2E
    wf = jnp.pad(jnp.tile(w[:, 3 * D_ATTN:3 * D_ATTN + N_HEADS], (1, 3)), ((0, 0), (0, LANES - 3 * N_HEADS)))
    w_all = jnp.concatenate([
        w[:, 0:D_ATTN] * scale, w[:, D_ATTN:2 * D_ATTN], wf, w[:, 3 * D_ATTN + N_HEADS:]], axis=1).astype(BF16)
    wvt = w[:, 2 * D_ATTN:3 * D_ATTN].T.astype(BF16)
    bf_row = jnp.pad(jnp.tile(b_f[0], 3), (0, LANES - 3 * N_HEADS)).reshape(1, LANES)
    pm = _placement()
    lig = ln_in_g.reshape(1, D)
    lib = ln_in_b.reshape(1, D)
    meta_pad = jnp.pad(meta, ((0, META_PAD - N_META), (0, 0)))

    k_meta, vt_meta, glu_meta = _call(
        _meta_body, "meta_proj", 48, (meta_pad, lig, lib, w_all, wvt, bf_row, pm),
        out_shape=(jax.ShapeDtypeStruct((N_HEADS, N_META, LANES), BF16),
                   jax.ShapeDtypeStruct((N_HEADS, V_ROWS, META_PAD), BF16),
                   jax.ShapeDtypeStruct((HALO, D_CONV), F32)))

    nt = S // TM_PROJ
    head_blk = pl.BlockSpec((1, N_HEADS, TM_PROJ, LANES), lambda b, i: (b, 0, i, 0))
    row_blk = pl.BlockSpec((1, TM_PROJ, D_ATTN), lambda b, i: (b, i, 0))
    qp, kp, vtp, ga, yc = _call(
        _proj_body, "in_proj", 56,
        (x, lig, lib, w_all, wvt, bf_row, pm, glu_meta,
         conv_w[0], conv_b[0].reshape(1, D_CONV), ln_conv_g[0].reshape(1, D_CONV), ln_conv_b[0].reshape(1, D_CONV),
         w_pw[0].astype(BF16), jnp.zeros((SUBLANES, LANES), jnp.int32)),
        grid=(B, nt),
        in_specs=[pl.BlockSpec((1, TM_PROJ, D), lambda b, i: (b, i, 0)),
                  _full((1, D)), _full((1, D)), _full((D, C_END)), _full((D_ATTN, D)), _full((1, LANES)),
                  _full((LANES, 2 * D_ATTN)), _full((HALO, D_CONV)),
                  _full((CONV_WIDTH, D_CONV)), _full((1, D_CONV)), _full((1, D_CONV)), _full((1, D_CONV)),
                  _full((D_CONV, D_CONV)), _full((SUBLANES, LANES))],
        out_specs=(head_blk, head_blk,
                   pl.BlockSpec((1, N_HEADS, 1, V_ROWS, TM_PROJ), lambda b, i: (b, 0, i, 0, 0)),
                   row_blk, row_blk),
        out_shape=(jax.ShapeDtypeStruct((B, N_HEADS, S, LANES), BF16),) * 2
                  + (jax.ShapeDtypeStruct((B, N_HEADS, nt, V_ROWS, TM_PROJ), BF16),
                     jax.ShapeDtypeStruct((B, S, D_ATTN), F32),
                     jax.ShapeDtypeStruct((B, S, D_CONV), BF16)),
        scratch=[pltpu.VMEM((1, LANES), F32), pltpu.VMEM((TM_PROJ, TM_PROJ), BF16),
                 pltpu.VMEM((HALO + TM_PROJ, D_CONV), F32), pltpu.VMEM((TM_PROJ, D_CONV), F32)])

    hp = HEADS_PER_STEP
    att = _call(
        _attn_body, "fox_attn", 56, (qp, kp, vtp, k_meta, vt_meta, ga),
        grid=(B, N_HEADS // hp, S // TQ),
        in_specs=[pl.BlockSpec((1, hp, TQ, LANES), lambda b, h, i: (b, h, i, 0)),
                  pl.BlockSpec((1, hp, S, LANES), lambda b, h, i: (b, h, 0, 0)),
                  pl.BlockSpec((1, hp, nt, V_ROWS, TK), lambda b, h, i: (b, h, 0, 0, 0)),
                  pl.BlockSpec((hp, N_META, LANES), lambda b, h, i: (h, 0, 0)),
                  pl.BlockSpec((hp, V_ROWS, META_PAD), lambda b, h, i: (h, 0, 0)),
                  pl.BlockSpec((1, TQ, hp * HEAD_DIM), lambda b, h, i: (b, i, h))],
        out_specs=pl.BlockSpec((1, TQ, hp * HEAD_DIM), lambda b, h, i: (b, i, h)),
        out_shape=jax.ShapeDtypeStruct((B, S, D_ATTN), BF16),
        scratch=[pltpu.VMEM((SCORE_SLOTS, TK, TQ), F32), pltpu.VMEM((2, TK, TQ), F32),
                 pltpu.VMEM((hp, 1, TQ), F32), pltpu.VMEM((hp, V_ROWS, TQ), F32)])

    tile_blk = pl.BlockSpec((1, TM_OUT, D_ATTN), lambda b, i: (b, i, 0))
    return _call(
        _out_body, "out_proj", 56,
        (x, att, yc, w_out[0].astype(BF16), ALPHA * lig, ALPHA * lib,
         ln_out_g[0].reshape(1, D), ln_out_b[0].reshape(1, D)),
        grid=(B, S // TM_OUT),
        in_specs=[pl.BlockSpec((1, TM_OUT, D), lambda b, i: (b, i, 0)), tile_blk, tile_blk,
                  _full((D, D)), _full((1, D)), _full((1, D)), _full((1, D)), _full((1, D))],
        out_specs=pl.BlockSpec((1, TM_OUT, D), lambda b, i: (b, i, 0)),
        out_shape=jax.ShapeDtypeStruct((B, S, D), F32))
```

```python
import numpy as np

import jax
import jax.numpy as jnp
from jax import lax
from jax.experimental import pallas as pl
from jax.experimental.pallas import tpu as pltpu

F32 = jnp.float32
BF16 = jnp.bfloat16

D_MODEL = 1024
N_META = 16
D_ATTN = 512
D_CONV = 512
N_HEADS = 8
HEAD_DIM = 64
CONV_WIDTH = 31
LN_EPS = 1e-5
ALPHA = 2.0 ** 0.25
V7X_VMEM_BYTES = 64 * 1024 * 1024
LANES = 128
SUBLANES = 8
BF16_ROWS = 16
META_PAD = 128
HALO = 32
MASK_BIAS = 1e30
LOG2E = 1.4426950408889634
V_ROWS = HEAD_DIM + BF16_ROWS

C_Q = 0
C_K = D_ATTN
C_F = 2 * D_ATTN
C_GA = C_F + LANES
C_U = C_GA + D_ATTN
C_UG = C_U + D_CONV
C_GC = C_UG + D_CONV
C_END = C_GC + D_CONV

TM_PROJ = 512
TQ = 512
TK = 512
HEADS_PER_STEP = 8
LOOKAHEAD = 2
SCORE_SLOTS = 4
TM_OUT = 1024
OUT_ROWS = 256
CONV_ROWS = 64


def _dot(a, b):
    return jnp.dot(a, b, preferred_element_type=F32)


def _dot_nt(a, b):
    return lax.dot_general(a, b, (((1,), (1,)), ((), ())), preferred_element_type=F32)


def _layer_norm(x, g, b):
    mu = jnp.mean(x, axis=-1, keepdims=True)
    xc = x - mu
    var = jnp.mean(xc * xc, axis=-1, keepdims=True)
    return xc * lax.rsqrt(var + LN_EPS) * g + b


def _sigmoid(x):
    return 1.0 / (1.0 + jnp.exp(-x))


def _silu(x):
    return x * _sigmoid(x)


def _log_sigmoid(x):
    return jnp.minimum(x, 0.0) - jnp.log1p(jnp.exp(-jnp.abs(x)))


def _split3(c):
    hi = c.astype(BF16).astype(F32)
    r = c - hi
    mid = r.astype(BF16).astype(F32)
    lo = (r - mid).astype(BF16).astype(F32)
    return hi, mid, lo


def _bias_operand(c, lane):
    hi, mid, lo = _split3(c * LOG2E)
    one = jnp.where(lane < 4 * N_HEADS, 1.0, 0.0)
    return jnp.where(lane < N_HEADS, hi,
                     jnp.where(lane < 2 * N_HEADS, mid,
                               jnp.where(lane < 3 * N_HEADS, lo, one))).astype(BF16)


def _forget_logs(hb, w_ref, bf_ref, lane):
    f = _dot(hb, w_ref[:, C_F:C_F + LANES])
    return jnp.where(lane < 3 * N_HEADS, _log_sigmoid(f + bf_ref[...]), 0.0)


def _exact_tri_dot(tri, x):
    hi, mid, lo = _split3(x)
    return _dot(tri, hi.astype(BF16)) + _dot(tri, mid.astype(BF16)) + _dot(tri, lo.astype(BF16))


def _head_tiles(proj, bias_cols, lane):
    low_half = lane < HEAD_DIM
    tiles = []
    for p in range(proj.shape[1] // LANES):
        pair = proj[:, p * LANES:(p + 1) * LANES]
        extra = bias_cols[:, p * LANES:(p + 1) * LANES]
        tiles.append(jnp.where(low_half, pair, extra).astype(BF16))
        tiles.append(pltpu.roll(jnp.where(low_half, extra, pair), HEAD_DIM, axis=1).astype(BF16))
    return tiles


def _ones_rows(cols, valid_cols):
    r = lax.broadcasted_iota(jnp.int32, (BF16_ROWS, cols), 0)
    c = lax.broadcasted_iota(jnp.int32, (BF16_ROWS, cols), 1)
    return jnp.where((r == 0) & (c < valid_cols), 1.0, 0.0).astype(BF16)


def _meta_body(x_ref, lng_ref, lnb_ref, w_ref, wvt_ref, bf_ref, pm_ref,
               k_ref, vt_ref, glu_ref):
    rows = x_ref.shape[0]
    hb = _layer_norm(x_ref[...], lng_ref[...], lnb_ref[...]).astype(BF16)
    lane = lax.broadcasted_iota(jnp.int32, (rows, LANES), 1)
    lf = _forget_logs(hb, w_ref, bf_ref, lane)
    r = lax.broadcasted_iota(jnp.int32, (rows, rows), 0)
    c = lax.broadcasted_iota(jnp.int32, (rows, rows), 1)
    upper = jnp.where((c > r) & (c < N_META), 1.0, 0.0).astype(BF16)
    cmat = _bias_operand(-_exact_tri_dot(upper, lf), lane)
    k_tiles = _head_tiles(_dot(hb, w_ref[:, C_K:C_F]), _dot(cmat, pm_ref[:, C_K:C_F]), lane)
    vt = _dot_nt(wvt_ref[...], hb)
    real = lax.broadcasted_iota(jnp.int32, (D_ATTN, rows), 1) < N_META
    vt = jnp.where(real, vt, 0.0).astype(BF16)
    ones = _ones_rows(rows, N_META)
    for h in range(N_HEADS):
        k_ref[h] = k_tiles[h][0:N_META, :]
        vt_ref[h, 0:HEAD_DIM, :] = vt[h * HEAD_DIM:(h + 1) * HEAD_DIM, :]
        vt_ref[h, HEAD_DIM:V_ROWS, :] = ones
    u = _dot(hb, w_ref[:, C_U:C_UG])
    ug = _dot(hb, w_ref[:, C_UG:C_GC])
    glu = u * _sigmoid(ug)
    glu_ref[0:HALO - N_META, :] = jnp.zeros((HALO - N_META, D_CONV), F32)
    glu_ref[HALO - N_META:HALO, :] = glu[0:N_META, :]


def _conv_rows(buf_ref, cw_ref, cb_ref, out_ref, base, zero):
    shift = HALO - (CONV_WIDTH - 1)
    for lo in range(0, D_CONV, LANES):
        cols = slice(lo, lo + LANES)
        out = jnp.broadcast_to(cb_ref[:, cols] + zero, (CONV_ROWS, LANES))
        for r in range(SUBLANES):
            window = CONV_ROWS + (SUBLANES if r else 0)
            part = None
            for j in range(CONV_WIDTH):
                if (shift + j) % SUBLANES != r:
                    continue
                start = base + shift + j - r
                term = cw_ref[j:j + 1, cols] * buf_ref[start:start + window, cols]
                part = term if part is None else part + term
            out = out + part[r:r + CONV_ROWS, :]
        out_ref[base:base + CONV_ROWS, cols] = out


def _proj_body(x_ref, lng_ref, lnb_ref, w_ref, wvt_ref, bf_ref, pm_ref, meta_glu_ref,
               cw_ref, cb_ref, lcg_ref, lcb_ref, wpw_ref, zero_ref,
               q_ref, k_ref, vt_ref, ga_ref, yc_ref,
               carry_ref, tri_ref, buf_ref, conv_ref):
    tm = x_ref.shape[1]
    first = pl.program_id(1) == 0

    @pl.when(first)
    def _start_sequence():
        carry_ref[...] = jnp.zeros_like(carry_ref)
        r = lax.broadcasted_iota(jnp.int32, (tm, tm), 0)
        c = lax.broadcasted_iota(jnp.int32, (tm, tm), 1)
        tri_ref[...] = jnp.where(r >= c, 1.0, 0.0).astype(BF16)
        buf_ref[0:HALO, :] = meta_glu_ref[...]

    @pl.when(jnp.logical_not(first))
    def _carry_conv_context():
        buf_ref[0:HALO, :] = buf_ref[tm:tm + HALO, :]

    def zero_after(tile):
        bits = lax.bitcast_convert_type(tile, jnp.int32) & zero_ref[...]
        return lax.bitcast_convert_type(bits, F32)[0:1, :]

    hb = _layer_norm(x_ref[0], lng_ref[...], lnb_ref[...]).astype(BF16)
    for lo in range(0, D_CONV, D_CONV // 2):
        u = _dot(hb, w_ref[:, C_U + lo:C_U + lo + D_CONV // 2])
        ug = _dot(hb, w_ref[:, C_UG + lo:C_UG + lo + D_CONV // 2])
        buf_ref[HALO:HALO + tm, lo:lo + D_CONV // 2] = u * _sigmoid(ug)

    lane = lax.broadcasted_iota(jnp.int32, (tm, LANES), 1)
    lf = _forget_logs(hb, w_ref, bf_ref, lane)
    c = _exact_tri_dot(tri_ref[...], lf) + carry_ref[...]
    carry_ref[...] = c[tm - 1:tm, :]
    cmat = _bias_operand(c, lane)
    ones = _ones_rows(tm, tm)

    def qk_heads(out_ref, col0, first_pair, n_pairs):
        cols = slice(col0 + first_pair * LANES, col0 + (first_pair + n_pairs) * LANES)
        ext = _dot(cmat, pm_ref[:, cols])
        tiles = _head_tiles(_dot(hb, w_ref[:, cols]), ext, lane)
        for h, tile in enumerate(tiles):
            out_ref[0, 2 * first_pair + h] = tile
        return zero_after(ext[tm - SUBLANES:, (n_pairs - 1) * LANES:])

    def v_heads(first_head, n_heads):
        rows = slice(first_head * HEAD_DIM, (first_head + n_heads) * HEAD_DIM)
        vt = _dot_nt(wvt_ref[rows, :], hb)
        for h in range(n_heads):
            vt_ref[0, first_head + h, 0, 0:HEAD_DIM, :] = vt[h * HEAD_DIM:(h + 1) * HEAD_DIM, :].astype(BF16)
            vt_ref[0, first_head + h, 0, HEAD_DIM:V_ROWS, :] = ones
        return zero_after(vt[n_heads * HEAD_DIM - SUBLANES:, tm - LANES:])

    def attn_gate():
        ga = _dot(hb, w_ref[:, C_GA:C_U])
        ga_ref[0] = ga
        return zero_after(ga[tm - SUBLANES:, D_ATTN - LANES:])

    half = N_HEADS // 2
    quarter = N_HEADS // 4
    gates = []

    def conv_gate():
        gates.append(_silu(_dot(hb, w_ref[:, C_GC:C_END])))
        return zero_after(gates[0][tm - SUBLANES:, D_CONV - LANES:])

    mxu_groups = [conv_gate, attn_gate, lambda: v_heads(0, half), lambda: v_heads(half, half),
                  lambda: qk_heads(q_ref, C_Q, 0, quarter), lambda: qk_heads(q_ref, C_Q, quarter, quarter),
                  lambda: qk_heads(k_ref, C_K, 0, quarter), lambda: qk_heads(k_ref, C_K, quarter, quarter)]
    chunks = list(range(0, tm, CONV_ROWS))
    per_chunk = -(-len(mxu_groups) // len(chunks))
    after = jnp.zeros((1, LANES), F32)
    for n, base in enumerate(chunks):
        _conv_rows(buf_ref, cw_ref, cb_ref, conv_ref, base, after)
        for group in mxu_groups[n * per_chunk:(n + 1) * per_chunk]:
            after = group()

    z = _silu(_layer_norm(conv_ref[...], lcg_ref[...], lcb_ref[...]))
    yc_ref[0] = (_dot(z.astype(BF16), wpw_ref[...]) * gates[0]).astype(BF16)


def _attn_body(q_ref, k_ref, vt_ref, km_ref, vmt_ref, ga_ref, o_ref, s_ref, pen_ref, m_ref, acc_ref):
    qi = pl.program_id(2)
    heads = range(HEADS_PER_STEP)

    @pl.when((pl.program_id(0) == 0) & (qi == 0))
    def _causal_penalty():
        key = lax.broadcasted_iota(jnp.int32, (TK, TQ), 0)
        qry = lax.broadcasted_iota(jnp.int32, (TK, TQ), 1)
        pen_ref[0] = jnp.zeros((TK, TQ), F32)
        pen_ref[1] = jnp.where(key <= qry, 0.0, -MASK_BIAS)

    pad = jnp.zeros((META_PAD - N_META, TQ), BF16)

    def meta_weights(s, m):
        return jnp.concatenate([jnp.exp2(s - m).astype(BF16), pad], axis=0)

    early = range(LOOKAHEAD)
    s_meta = [_dot_nt(km_ref[hh], q_ref[0, hh]) for hh in early]
    m_meta = [jnp.max(s, axis=0, keepdims=True) for s in s_meta]
    for hh in heads:
        if hh < LOOKAHEAD:
            m_ref[hh] = m_meta[hh]
            acc_ref[hh] = _dot(vmt_ref[hh], meta_weights(s_meta[hh], m_meta[hh]))
        else:
            m_ref[hh] = jnp.full((1, TQ), -MASK_BIAS, F32)
            acc_ref[hh] = jnp.zeros((V_ROWS, TQ), F32)

    def produce(j, hh, penalty):
        off = pl.multiple_of(j * TK, TK)
        s = _dot_nt(k_ref[0, hh, pl.ds(off, TK), :], q_ref[0, hh])
        if penalty is not None:
            s = s + pen_ref[penalty]
        s_ref[hh % SCORE_SLOTS] = s
        return jnp.max(s, axis=0, keepdims=True)

    def consume(j, hh, col_max):
        m = m_ref[hh]
        m_new = jnp.maximum(m, col_max)
        p = jnp.exp2(s_ref[hh % SCORE_SLOTS] - m_new).astype(BF16)
        acc_ref[hh] = acc_ref[hh] * jnp.exp2(m - m_new) + _dot(vt_ref[0, hh, j], p)
        m_ref[hh] = m_new

    def diagonal_flag(j):
        return jnp.where(j < qi, 0, 1)

    def body(j, pending):
        pending = list(pending)
        for hh in heads:
            ahead = hh + LOOKAHEAD
            if ahead < HEADS_PER_STEP:
                pending.append(produce(j, ahead, None))
            else:
                pending.append(produce(j + 1, ahead - HEADS_PER_STEP, diagonal_flag(j + 1)))
            consume(j, hh, pending.pop(0))
        return tuple(pending)

    half = TK // 2

    def produce_diagonal(hh):
        slot = hh % SCORE_SLOTS
        off = pl.multiple_of(qi * TK, TK)
        keys = jnp.concatenate([k_ref[0, hh, pl.ds(off, half), :], km_ref[hh]], axis=0)
        scores = _dot_nt(keys, q_ref[0, hh])
        top = scores[0:half, :] + pen_ref[1, 0:half, :]
        meta = scores[half:, :]
        low = (_dot_nt(k_ref[0, hh, pl.ds(off + half, half), :], q_ref[0, hh, half:, :])
               + pen_ref[1, half:, half:])
        s_ref[slot, 0:half, :] = top
        s_ref[slot, half:, half:] = low
        top_max = jnp.maximum(jnp.max(top, axis=0, keepdims=True), jnp.max(meta, axis=0, keepdims=True))
        low_max = jnp.max(low, axis=0, keepdims=True)
        col_max = jnp.concatenate([top_max[:, 0:half], jnp.maximum(top_max[:, half:], low_max)], axis=1)
        return col_max, meta

    def consume_diagonal(hh, produced):
        col_max, meta = produced
        slot = hh % SCORE_SLOTS
        m = m_ref[hh]
        m_new = jnp.maximum(m, col_max)
        p_top = jnp.exp2(s_ref[slot, 0:half, :] - m_new).astype(BF16)
        p_low = jnp.exp2(s_ref[slot, half:, half:] - m_new[:, half:]).astype(BF16)
        acc = (acc_ref[hh] * jnp.exp2(m - m_new) + _dot(vt_ref[0, hh, qi, :, 0:half], p_top)
               + _dot(vmt_ref[hh], meta_weights(meta, m_new)))
        acc_ref[hh, :, 0:half] = acc[:, 0:half]
        acc_ref[hh, :, half:] = acc[:, half:] + _dot(vt_ref[0, hh, qi, :, half:], p_low)
        m_ref[hh] = m_new

    def normalized(hh):
        return acc_ref[hh, 0:HEAD_DIM, :] / acc_ref[hh, HEAD_DIM:HEAD_DIM + 1, :]

    pending = lax.fori_loop(0, qi, body, tuple(produce(0, hh, diagonal_flag(0)) for hh in range(LOOKAHEAD)))
    pending = list(pending)
    for hh in heads:
        if hh + LOOKAHEAD < HEADS_PER_STEP:
            pending.append(produce_diagonal(hh + LOOKAHEAD))
        if hh < LOOKAHEAD:
            consume(qi, hh, pending.pop(0))
        else:
            consume_diagonal(hh, pending.pop(0))
        if hh % 2:
            cols = slice((hh - 1) * HEAD_DIM, (hh + 1) * HEAD_DIM)
            pair = jnp.concatenate([normalized(hh - 1), normalized(hh)], axis=0).T
            o_ref[0, :, cols] = (pair * _silu(ga_ref[0, :, cols])).astype(BF16)


def _out_body(x_ref, ya_ref, yc_ref, wout_ref, lig_ref, lib_ref, log_ref, lob_ref, o_ref):
    tm = x_ref.shape[1]
    for r0 in range(0, tm, OUT_ROWS):
        rows = slice(r0, r0 + OUT_ROWS)
        y = _dot(jnp.concatenate([ya_ref[0, rows, :], yc_ref[0, rows, :]], axis=1), wout_ref[...])
        scaled_h = _layer_norm(x_ref[0, rows, :], lig_ref[...], lib_ref[...])
        o_ref[0, rows, :] = _layer_norm(scaled_h + y, log_ref[...], lob_ref[...])


def _placement():
    pm = np.zeros((LANES, 2 * D_ATTN), np.float32)
    for h in range(N_HEADS):
        base = (h // 2) * LANES + (HEAD_DIM if h % 2 == 0 else 0)
        for g in range(3):
            pm[g * N_HEADS + h, C_Q + base + g] = 1.0
            pm[3 * N_HEADS + h, C_Q + base + 3 + g] = 1.0
            pm[3 * N_HEADS + h, C_K + base + g] = 1.0
            pm[g * N_HEADS + h, C_K + base + 3 + g] = -1.0
    return jnp.asarray(pm, BF16)


def _full(shape):
    return pl.BlockSpec(shape, lambda *_: (0,) * len(shape))


def _call(body, name, vmem_mib, operands, out_shape, grid=(), in_specs=None, out_specs=None, scratch=()):
    assert vmem_mib * 1024 * 1024 <= V7X_VMEM_BYTES
    grid_args = {} if in_specs is None else dict(grid=grid, in_specs=in_specs, out_specs=out_specs)
    return pl.pallas_call(
        body, out_shape=out_shape, scratch_shapes=list(scratch), name=name,
        compiler_params=pltpu.CompilerParams(dimension_semantics=("arbitrary",) * len(grid),
                                             vmem_limit_bytes=vmem_mib * 1024 * 1024),
        **grid_args)(*operands)


def kernel(x, meta, ln_in_g, ln_in_b, w_in, b_f, conv_w, conv_b, ln_conv_g, ln_conv_b,
           w_pw, w_out, ln_out_g, ln_out_b):
    B, S, D = x.shape
    assert D == D_MODEL and w_in.shape[0] == 1 and meta.shape == (N_META, D_MODEL)
    assert S % TM_PROJ == 0 and S % TM_OUT == 0 and S % TQ == 0 and TQ == TK == TM_PROJ
    assert TM_OUT % OUT_ROWS == 0 and TM_PROJ % CONV_ROWS == 0 and N_HEADS % HEADS_PER_STEP == 0
    assert HEADS_PER_STEP % SCORE_SLOTS == 0 and SCORE_SLOTS > LOOKAHEAD

    w = w_in[0]
    scale = HEAD_DIM ** -0.5 * LOG2E
    wf = jnp.pad(jnp.tile(w[:, 3 * D_ATTN:3 * D_ATTN + N_HEADS], (1, 3)), ((0, 0), (0, LANES - 3 * N_HEADS)))
    w_all = jnp.concatenate([
        w[:, 0:D_ATTN] * scale, w[:, D_ATTN:2 * D_ATTN], wf, w[:, 3 * D_ATTN + N_HEADS:]], axis=1).astype(BF16)
    wvt = w[:, 2 * D_ATTN:3 * D_ATTN].T.astype(BF16)
    bf_row = jnp.pad(jnp.tile(b_f[0], 3), (0, LANES - 3 * N_HEADS)).reshape(1, LANES)
    pm = _placement()
    lig = ln_in_g.reshape(1, D)
    lib = ln_in_b.reshape(1, D)
    meta_pad = jnp.pad(meta, ((0, META_PAD - N_META), (0, 0)))

    k_meta, vt_meta, glu_meta = _call(
        _meta_body, "meta_proj", 48, (meta_pad, lig, lib, w_all, wvt, bf_row, pm),
        out_shape=(jax.ShapeDtypeStruct((N_HEADS, N_META, LANES), BF16),
                   jax.ShapeDtypeStruct((N_HEADS, V_ROWS, META_PAD), BF16),
                   jax.ShapeDtypeStruct((HALO, D_CONV), F32)))

    nt = S // TM_PROJ
    head_blk = pl.BlockSpec((1, N_HEADS, TM_PROJ, LANES), lambda b, i: (b, 0, i, 0))
    row_blk = pl.BlockSpec((1, TM_PROJ, D_ATTN), lambda b, i: (b, i, 0))
    qp, kp, vtp, ga, yc = _call(
        _proj_body, "in_proj", 56,
        (x, lig, lib, w_all, wvt, bf_row, pm, glu_meta,
         conv_w[0], conv_b[0].reshape(1, D_CONV), ln_conv_g[0].reshape(1, D_CONV), ln_conv_b[0].reshape(1, D_CONV),
         w_pw[0].astype(BF16), jnp.zeros((SUBLANES, LANES), jnp.int32)),
        grid=(B, nt),
        in_specs=[pl.BlockSpec((1, TM_PROJ, D), lambda b, i: (b, i, 0)),
                  _full((1, D)), _full((1, D)), _full((D, C_END)), _full((D_ATTN, D)), _full((1, LANES)),
                  _full((LANES, 2 * D_ATTN)), _full((HALO, D_CONV)),
                  _full((CONV_WIDTH, D_CONV)), _full((1, D_CONV)), _full((1, D_CONV)), _full((1, D_CONV)),
                  _full((D_CONV, D_CONV)), _full((SUBLANES, LANES))],
        out_specs=(head_blk, head_blk,
                   pl.BlockSpec((1, N_HEADS, 1, V_ROWS, TM_PROJ), lambda b, i: (b, 0, i, 0, 0)),
                   row_blk, row_blk),
        out_shape=(jax.ShapeDtypeStruct((B, N_HEADS, S, LANES), BF16),) * 2
                  + (jax.ShapeDtypeStruct((B, N_HEADS, nt, V_ROWS, TM_PROJ), BF16),
                     jax.ShapeDtypeStruct((B, S, D_ATTN), F32),
                     jax.ShapeDtypeStruct((B, S, D_CONV), BF16)),
        scratch=[pltpu.VMEM((1, LANES), F32), pltpu.VMEM((TM_PROJ, TM_PROJ), BF16),
                 pltpu.VMEM((HALO + TM_PROJ, D_CONV), F32), pltpu.VMEM((TM_PROJ, D_CONV), F32)])

    hp = HEADS_PER_STEP
    att = _call(
        _attn_body, "fox_attn", 56, (qp, kp, vtp, k_meta, vt_meta, ga),
        grid=(B, N_HEADS // hp, S // TQ),
        in_specs=[pl.BlockSpec((1, hp, TQ, LANES), lambda b, h, i: (b, h, i, 0)),
                  pl.BlockSpec((1, hp, S, LANES), lambda b, h, i: (b, h, 0, 0)),
                  pl.BlockSpec((1, hp, nt, V_ROWS, TK), lambda b, h, i: (b, h, 0, 0, 0)),
                  pl.BlockSpec((hp, N_META, LANES), lambda b, h, i: (h, 0, 0)),
                  pl.BlockSpec((hp, V_ROWS, META_PAD), lambda b, h, i: (h, 0, 0)),
                  pl.BlockSpec((1, TQ, hp * HEAD_DIM), lambda b, h, i: (b, i, h))],
        out_specs=pl.BlockSpec((1, TQ, hp * HEAD_DIM), lambda b, h, i: (b, i, h)),
        out_shape=jax.ShapeDtypeStruct((B, S, D_ATTN), BF16),
        scratch=[pltpu.VMEM((SCORE_SLOTS, TK, TQ), F32), pltpu.VMEM((2, TK, TQ), F32),
                 pltpu.VMEM((hp, 1, TQ), F32), pltpu.VMEM((hp, V_ROWS, TQ), F32)])

    tile_blk = pl.BlockSpec((1, TM_OUT, D_ATTN), lambda b, i: (b, i, 0))
    return _call(
        _out_body, "out_proj", 56,
        (x, att, yc, w_out[0].astype(BF16), ALPHA * lig, ALPHA * lib,
         ln_out_g[0].reshape(1, D), ln_out_b[0].reshape(1, D)),
        grid=(B, S // TM_OUT),
        in_specs=[pl.BlockSpec((1, TM_OUT, D), lambda b, i: (b, i, 0)), tile_blk, tile_blk,
                  _full((D, D)), _full((1, D)), _full((1, D)), _full((1, D)), _full((1, D))],
        out_specs=pl.BlockSpec((1, TM_OUT, D), lambda b, i: (b, i, 0)),
        out_shape=jax.ShapeDtypeStruct((B, S, D), F32))
```

```python
import numpy as np

import jax
import jax.numpy as jnp
from jax import lax
from jax.experimental import pallas as pl
from jax.experimental.pallas import tpu as pltpu

F32 = jnp.float32
BF16 = jnp.bfloat16

D_MODEL = 1024
N_META = 16
D_ATTN = 512
D_CONV = 512
N_HEADS = 8
HEAD_DIM = 64
CONV_WIDTH = 31
LN_EPS = 1e-5
ALPHA = 2.0 ** 0.25
V7X_VMEM_BYTES = 64 * 1024 * 1024
LANES = 128
SUBLANES = 8
BF16_ROWS = 16
META_PAD = 128
HALO = 32
MASK_BIAS = 1e30
LOG2E = 1.4426950408889634
V_ROWS = HEAD_DIM + BF16_ROWS

C_Q = 0
C_K = D_ATTN
C_F = 2 * D_ATTN
C_GA = C_F + LANES
C_U = C_GA + D_ATTN
C_UG = C_U + D_CONV
C_GC = C_UG + D_CONV
C_END = C_GC + D_CONV

TM_PROJ = 512
TQ = 512
TK = 512
HEADS_PER_STEP = 8
LOOKAHEAD = 2
SCORE_SLOTS = 4
TM_OUT = 1024
OUT_ROWS = 256
CONV_ROWS = 64


def _dot(a, b):
    return jnp.dot(a, b, preferred_element_type=F32)


def _dot_nt(a, b):
    return lax.dot_general(a, b, (((1,), (1,)), ((), ())), preferred_element_type=F32)


def _layer_norm(x, g, b):
    mu = jnp.mean(x, axis=-1, keepdims=True)
    xc = x - mu
    var = jnp.mean(xc * xc, axis=-1, keepdims=True)
    return xc * lax.rsqrt(var + LN_EPS) * g + b


def _sigmoid(x):
    return 0.5 * jnp.tanh(0.5 * x) + 0.5


def _silu(x):
    return x * _sigmoid(x)


def _log_sigmoid(x):
    return jnp.minimum(x, 0.0) - jnp.log1p(jnp.exp(-jnp.abs(x)))


def _split3(c):
    hi = c.astype(BF16).astype(F32)
    r = c - hi
    mid = r.astype(BF16).astype(F32)
    lo = (r - mid).astype(BF16).astype(F32)
    return hi, mid, lo


def _bias_operand(c, lane):
    hi, mid, lo = _split3(c * LOG2E)
    one = jnp.where(lane < 4 * N_HEADS, 1.0, 0.0)
    return jnp.where(lane < N_HEADS, hi,
                     jnp.where(lane < 2 * N_HEADS, mid,
                               jnp.where(lane < 3 * N_HEADS, lo, one))).astype(BF16)


def _forget_logs(hb, w_ref, bf_ref, lane):
    f = _dot(hb, w_ref[:, C_F:C_F + LANES])
    return jnp.where(lane < 3 * N_HEADS, _log_sigmoid(f + bf_ref[...]), 0.0)


def _exact_tri_dot(tri, x):
    hi, mid, lo = _split3(x)
    return _dot(tri, hi.astype(BF16)) + _dot(tri, mid.astype(BF16)) + _dot(tri, lo.astype(BF16))


def _head_tiles(proj, bias_cols, lane):
    low_half = lane < HEAD_DIM
    tiles = []
    for p in range(proj.shape[1] // LANES):
        pair = proj[:, p * LANES:(p + 1) * LANES]
        extra = bias_cols[:, p * LANES:(p + 1) * LANES]
        tiles.append(jnp.where(low_half, pair, extra).astype(BF16))
        tiles.append(pltpu.roll(jnp.where(low_half, extra, pair), HEAD_DIM, axis=1).astype(BF16))
    return tiles


def _ones_rows(cols, valid_cols):
    r = lax.broadcasted_iota(jnp.int32, (BF16_ROWS, cols), 0)
    c = lax.broadcasted_iota(jnp.int32, (BF16_ROWS, cols), 1)
    return jnp.where((r == 0) & (c < valid_cols), 1.0, 0.0).astype(BF16)


def _meta_body(x_ref, lng_ref, lnb_ref, w_ref, wvt_ref, bf_ref, pm_ref,
               k_ref, vt_ref, glu_ref):
    rows = x_ref.shape[0]
    hb = _layer_norm(x_ref[...], lng_ref[...], lnb_ref[...]).astype(BF16)
    lane = lax.broadcasted_iota(jnp.int32, (rows, LANES), 1)
    lf = _forget_logs(hb, w_ref, bf_ref, lane)
    r = lax.broadcasted_iota(jnp.int32, (rows, rows), 0)
    c = lax.broadcasted_iota(jnp.int32, (rows, rows), 1)
    upper = jnp.where((c > r) & (c < N_META), 1.0, 0.0).astype(BF16)
    cmat = _bias_operand(-_exact_tri_dot(upper, lf), lane)
    k_tiles = _head_tiles(_dot(hb, w_ref[:, C_K:C_F]), _dot(cmat, pm_ref[:, C_K:C_F]), lane)
    vt = _dot_nt(wvt_ref[...], hb)
    real = lax.broadcasted_iota(jnp.int32, (D_ATTN, rows), 1) < N_META
    vt = jnp.where(real, vt, 0.0).astype(BF16)
    ones = _ones_rows(rows, N_META)
    for h in range(N_HEADS):
        k_ref[h] = k_tiles[h][0:N_META, :]
        vt_ref[h, 0:HEAD_DIM, :] = vt[h * HEAD_DIM:(h + 1) * HEAD_DIM, :]
        vt_ref[h, HEAD_DIM:V_ROWS, :] = ones
    u = _dot(hb, w_ref[:, C_U:C_UG])
    ug = _dot(hb, w_ref[:, C_UG:C_GC])
    glu = u * _sigmoid(ug)
    glu_ref[0:HALO - N_META, :] = jnp.zeros((HALO - N_META, D_CONV), F32)
    glu_ref[HALO - N_META:HALO, :] = glu[0:N_META, :]


def _conv_rows(buf_ref, cw_ref, cb_ref, out_ref, base, zero):
    shift = HALO - (CONV_WIDTH - 1)
    for lo in range(0, D_CONV, LANES):
        cols = slice(lo, lo + LANES)
        out = jnp.broadcast_to(cb_ref[:, cols] + zero, (CONV_ROWS, LANES))
        for r in range(SUBLANES):
            window = CONV_ROWS + (SUBLANES if r else 0)
            part = None
            for j in range(CONV_WIDTH):
                if (shift + j) % SUBLANES != r:
                    continue
                start = base + shift + j - r
                term = cw_ref[j:j + 1, cols] * buf_ref[start:start + window, cols]
                part = term if part is None else part + term
            out = out + part[r:r + CONV_ROWS, :]
        out_ref[base:base + CONV_ROWS, cols] = out


def _proj_body(x_ref, lng_ref, lnb_ref, w_ref, wvt_ref, bf_ref, pm_ref, meta_glu_ref,
               cw_ref, cb_ref, lcg_ref, lcb_ref, wpw_ref, zero_ref,
               q_ref, k_ref, vt_ref, ga_ref, yc_ref,
               carry_ref, tri_ref, buf_ref, conv_ref):
    tm = x_ref.shape[1]
    first = pl.program_id(1) == 0

    @pl.when(first)
    def _start_sequence():
        carry_ref[...] = jnp.zeros_like(carry_ref)
        r = lax.broadcasted_iota(jnp.int32, (tm, tm), 0)
        c = lax.broadcasted_iota(jnp.int32, (tm, tm), 1)
        tri_ref[...] = jnp.where(r >= c, 1.0, 0.0).astype(BF16)
        buf_ref[0:HALO, :] = meta_glu_ref[...]

    @pl.when(jnp.logical_not(first))
    def _carry_conv_context():
        buf_ref[0:HALO, :] = buf_ref[tm:tm + HALO, :]

    def zero_after(tile):
        bits = lax.bitcast_convert_type(tile, jnp.int32) & zero_ref[...]
        return lax.bitcast_convert_type(bits, F32)[0:1, :]

    hb = _layer_norm(x_ref[0], lng_ref[...], lnb_ref[...]).astype(BF16)
    for lo in range(0, D_CONV, D_CONV // 2):
        u = _dot(hb, w_ref[:, C_U + lo:C_U + lo + D_CONV // 2])
        ug = _dot(hb, w_ref[:, C_UG + lo:C_UG + lo + D_CONV // 2])
        buf_ref[HALO:HALO + tm, lo:lo + D_CONV // 2] = u * _sigmoid(ug)

    lane = lax.broadcasted_iota(jnp.int32, (tm, LANES), 1)
    lf = _forget_logs(hb, w_ref, bf_ref, lane)
    c = _exact_tri_dot(tri_ref[...], lf) + carry_ref[...]
    carry_ref[...] = c[tm - 1:tm, :]
    cmat = _bias_operand(c, lane)
    ones = _ones_rows(tm, tm)

    def qk_heads(out_ref, col0, first_pair, n_pairs):
        cols = slice(col0 + first_pair * LANES, col0 + (first_pair + n_pairs) * LANES)
        ext = _dot(cmat, pm_ref[:, cols])
        tiles = _head_tiles(_dot(hb, w_ref[:, cols]), ext, lane)
        for h, tile in enumerate(tiles):
            out_ref[0, 2 * first_pair + h] = tile
        return zero_after(ext[tm - SUBLANES:, (n_pairs - 1) * LANES:])

    def v_heads(first_head, n_heads):
        rows = slice(first_head * HEAD_DIM, (first_head + n_heads) * HEAD_DIM)
        vt = _dot_nt(wvt_ref[rows, :], hb)
        for h in range(n_heads):
            vt_ref[0, first_head + h, 0, 0:HEAD_DIM, :] = vt[h * HEAD_DIM:(h + 1) * HEAD_DIM, :].astype(BF16)
            vt_ref[0, first_head + h, 0, HEAD_DIM:V_ROWS, :] = ones
        return zero_after(vt[n_heads * HEAD_DIM - SUBLANES:, tm - LANES:])

    def attn_gate():
        ga = _dot(hb, w_ref[:, C_GA:C_U])
        ga_ref[0] = ga
        return zero_after(ga[tm - SUBLANES:, D_ATTN - LANES:])

    half = N_HEADS // 2
    quarter = N_HEADS // 4
    gates = []

    def conv_gate():
        gates.append(_silu(_dot(hb, w_ref[:, C_GC:C_END])))
        return zero_after(gates[0][tm - SUBLANES:, D_CONV - LANES:])

    mxu_groups = [conv_gate, attn_gate, lambda: v_heads(0, half), lambda: v_heads(half, half),
                  lambda: qk_heads(q_ref, C_Q, 0, quarter), lambda: qk_heads(q_ref, C_Q, quarter, quarter),
                  lambda: qk_heads(k_ref, C_K, 0, quarter), lambda: qk_heads(k_ref, C_K, quarter, quarter)]
    chunks = list(range(0, tm, CONV_ROWS))
    per_chunk = -(-len(mxu_groups) // len(chunks))
    after = jnp.zeros((1, LANES), F32)
    for n, base in enumerate(chunks):
        _conv_rows(buf_ref, cw_ref, cb_ref, conv_ref, base, after)
        for group in mxu_groups[n * per_chunk:(n + 1) * per_chunk]:
            after = group()

    z = _silu(_layer_norm(conv_ref[...], lcg_ref[...], lcb_ref[...]))
    yc_ref[0] = (_dot(z.astype(BF16), wpw_ref[...]) * gates[0]).astype(BF16)


def _attn_body(q_ref, k_ref, vt_ref, km_ref, vmt_ref, ga_ref, o_ref, s_ref, pen_ref, m_ref, acc_ref):
    qi = pl.program_id(2)
    heads = range(HEADS_PER_STEP)

    @pl.when((pl.program_id(0) == 0) & (qi == 0))
    def _causal_penalty():
        key = lax.broadcasted_iota(jnp.int32, (TK, TQ), 0)
        qry = lax.broadcasted_iota(jnp.int32, (TK, TQ), 1)
        pen_ref[0] = jnp.zeros((TK, TQ), F32)
        pen_ref[1] = jnp.where(key <= qry, 0.0, -MASK_BIAS)

    pad = jnp.zeros((META_PAD - N_META, TQ), BF16)

    def meta_weights(s, m):
        return jnp.concatenate([jnp.exp2(s - m).astype(BF16), pad], axis=0)

    early = range(LOOKAHEAD)
    s_meta = [_dot_nt(km_ref[hh], q_ref[0, hh]) for hh in early]
    m_meta = [jnp.max(s, axis=0, keepdims=True) for s in s_meta]
    for hh in heads:
        if hh < LOOKAHEAD:
            m_ref[hh] = m_meta[hh]
            acc_ref[hh] = _dot(vmt_ref[hh], meta_weights(s_meta[hh], m_meta[hh]))
        else:
            m_ref[hh] = jnp.full((1, TQ), -MASK_BIAS, F32)
            acc_ref[hh] = jnp.zeros((V_ROWS, TQ), F32)

    def produce(j, hh, penalty):
        off = pl.multiple_of(j * TK, TK)
        s = _dot_nt(k_ref[0, hh, pl.ds(off, TK), :], q_ref[0, hh])
        if penalty is not None:
            s = s + pen_ref[penalty]
        s_ref[hh % SCORE_SLOTS] = s
        return jnp.max(s, axis=0, keepdims=True)

    def consume(j, hh, col_max):
        m = m_ref[hh]
        m_new = jnp.maximum(m, col_max)
        p = jnp.exp2(s_ref[hh % SCORE_SLOTS] - m_new).astype(BF16)
        acc_ref[hh] = acc_ref[hh] * jnp.exp2(m - m_new) + _dot(vt_ref[0, hh, j], p)
        m_ref[hh] = m_new

    def diagonal_flag(j):
        return jnp.where(j < qi, 0, 1)

    def body(j, pending):
        pending = list(pending)
        for hh in heads:
            ahead = hh + LOOKAHEAD
            if ahead < HEADS_PER_STEP:
                pending.append(produce(j, ahead, None))
            else:
                pending.append(produce(j + 1, ahead - HEADS_PER_STEP, diagonal_flag(j + 1)))
            consume(j, hh, pending.pop(0))
        return tuple(pending)

    half = TK // 2

    def produce_diagonal(hh):
        slot = hh % SCORE_SLOTS
        off = pl.multiple_of(qi * TK, TK)
        keys = jnp.concatenate([k_ref[0, hh, pl.ds(off, half), :], km_ref[hh]], axis=0)
        scores = _dot_nt(keys, q_ref[0, hh])
        top = scores[0:half, :] + pen_ref[1, 0:half, :]
        meta = scores[half:, :]
        low = (_dot_nt(k_ref[0, hh, pl.ds(off + half, half), :], q_ref[0, hh, half:, :])
               + pen_ref[1, half:, half:])
        s_ref[slot, 0:half, :] = top
        s_ref[slot, half:, half:] = low
        top_max = jnp.maximum(jnp.max(top, axis=0, keepdims=True), jnp.max(meta, axis=0, keepdims=True))
        low_max = jnp.max(low, axis=0, keepdims=True)
        col_max = jnp.concatenate([top_max[:, 0:half], jnp.maximum(top_max[:, half:], low_max)], axis=1)
        return col_max, meta

    def consume_diagonal(hh, produced):
        col_max, meta = produced
        slot = hh % SCORE_SLOTS
        m = m_ref[hh]
        m_new = jnp.maximum(m, col_max)
        p_top = jnp.exp2(s_ref[slot, 0:half, :] - m_new).astype(BF16)
        p_low = jnp.exp2(s_ref[slot, half:, half:] - m_new[:, half:]).astype(BF16)
        acc = (acc_ref[hh] * jnp.exp2(m - m_new) + _dot(vt_ref[0, hh, qi, :, 0:half], p_top)
               + _dot(vmt_ref[hh], meta_weights(meta, m_new)))
        acc_ref[hh, :, 0:half] = acc[:, 0:half]
        acc_ref[hh, :, half:] = acc[:, half:] + _dot(vt_ref[0, hh, qi, :, half:], p_low)
        m_ref[hh] = m_new

    def normalized(hh):
        return acc_ref[hh, 0:HEAD_DIM, :] / acc_ref[hh, HEAD_DIM:HEAD_DIM + 1, :]

    pending = lax.fori_loop(0, qi, body, tuple(produce(0, hh, diagonal_flag(0)) for hh in range(LOOKAHEAD)))
    pending = list(pending)
    for hh in heads:
        if hh + LOOKAHEAD < HEADS_PER_STEP:
            pending.append(produce_diagonal(hh + LOOKAHEAD))
        if hh < LOOKAHEAD:
            consume(qi, hh, pending.pop(0))
        else:
            consume_diagonal(hh, pending.pop(0))
        if hh % 2:
            cols = slice((hh - 1) * HEAD_DIM, (hh + 1) * HEAD_DIM)
            pair = jnp.concatenate([normalized(hh - 1), normalized(hh)], axis=0).T
            o_ref[0, :, cols] = (pair * _silu(ga_ref[0, :, cols])).astype(BF16)


def _out_body(x_ref, ya_ref, yc_ref, wout_ref, lig_ref, lib_ref, log_ref, lob_ref, o_ref):
    tm = x_ref.shape[1]
    for r0 in range(0, tm, OUT_ROWS):
        rows = slice(r0, r0 + OUT_ROWS)
        y = _dot(jnp.concatenate([ya_ref[0, rows, :], yc_ref[0, rows, :]], axis=1), wout_ref[...])
        scaled_h = _layer_norm(x_ref[0, rows, :], lig_ref[...], lib_ref[...])
        o_ref[0, rows, :] = _layer_norm(scaled_h + y, log_ref[...], lob_ref[...])


def _placement():
    pm = np.zeros((LANES, 2 * D_ATTN), np.float32)
    for h in range(N_HEADS):
        base = (h // 2) * LANES + (HEAD_DIM if h % 2 == 0 else 0)
        for g in range(3):
            pm[g * N_HEADS + h, C_Q + base + g] = 1.0
            pm[3 * N_HEADS + h, C_Q + base + 3 + g] = 1.0
            pm[3 * N_HEADS + h, C_K + base + g] = 1.0
            pm[g * N_HEADS + h, C_K + base + 3 + g] = -1.0
    return jnp.asarray(pm, BF16)


def _full(shape):
    return pl.BlockSpec(shape, lambda *_: (0,) * len(shape))


def _call(body, name, vmem_mib, operands, out_shape, grid=(), in_specs=None, out_specs=None, scratch=()):
    assert vmem_mib * 1024 * 1024 <= V7X_VMEM_BYTES
    grid_args = {} if in_specs is None else dict(grid=grid, in_specs=in_specs, out_specs=out_specs)
    return pl.pallas_call(
        body, out_shape=out_shape, scratch_shapes=list(scratch), name=name,
        compiler_params=pltpu.CompilerParams(dimension_semantics=("arbitrary",) * len(grid),
                                             vmem_limit_bytes=vmem_mib * 1024 * 1024),
        **grid_args)(*operands)


def kernel(x, meta, ln_in_g, ln_in_b, w_in, b_f, conv_w, conv_b, ln_conv_g, ln_conv_b,
           w_pw, w_out, ln_out_g, ln_out_b):
    B, S, D = x.shape
    assert D == D_MODEL and w_in.shape[0] == 1 and meta.shape == (N_META, D_MODEL)
    assert S % TM_PROJ == 0 and S % TM_OUT == 0 and S % TQ == 0 and TQ == TK == TM_PROJ
    assert TM_OUT % OUT_ROWS == 0 and TM_PROJ % CONV_ROWS == 0 and N_HEADS % HEADS_PER_STEP == 0
    assert HEADS_PER_STEP % SCORE_SLOTS == 0 and SCORE_SLOTS > LOOKAHEAD

    w = w_in[0]
    scale = HEAD_DIM ** -0.5 * LOG2E
    wf = jnp.pad(jnp.tile(w[:, 3 * D_ATTN:3 * D_ATTN + N_HEADS], (1, 3)), ((0, 0), (0, LANES - 3 * N_HEADS)))
    w_all = jnp.concatenate([
        w[:, 0:D_ATTN] * scale, w[:, D_ATTN:2 * D_ATTN], wf, w[:, 3 * D_ATTN + N_HEADS:]], axis=1).astype(BF16)
    wvt = w[:, 2 * D_ATTN:3 * D_ATTN].T.astype(BF16)
    bf_row = jnp.pad(jnp.tile(b_f[0], 3), (0, LANES - 3 * N_HEADS)).reshape(1, LANES)
    pm = _placement()
    lig = ln_in_g.reshape(1, D)
    lib = ln_in_b.reshape(1, D)
    meta_pad = jnp.pad(meta, ((0, META_PAD - N_META), (0, 0)))

    k_meta, vt_meta, glu_meta = _call(
        _meta_body, "meta_proj", 48, (meta_pad, lig, lib, w_all, wvt, bf_row, pm),
        out_shape=(jax.ShapeDtypeStruct((N_HEADS, N_META, LANES), BF16),
                   jax.ShapeDtypeStruct((N_HEADS, V_ROWS, META_PAD), BF16),
                   jax.ShapeDtypeStruct((HALO, D_CONV), F32)))

    nt = S // TM_PROJ
    head_blk = pl.BlockSpec((1, N_HEADS, TM_PROJ, LANES), lambda b, i: (b, 0, i, 0))
    row_blk = pl.BlockSpec((1, TM_PROJ, D_ATTN), lambda b, i: (b, i, 0))
    qp, kp, vtp, ga, yc = _call(
        _proj_body, "in_proj", 56,
        (x, lig, lib, w_all, wvt, bf_row, pm, glu_meta,
         conv_w[0], conv_b[0].reshape(1, D_CONV), ln_conv_g[0].reshape(1, D_CONV), ln_conv_b[0].reshape(1, D_CONV),
         w_pw[0].astype(BF16), jnp.zeros((SUBLANES, LANES), jnp.int32)),
        grid=(B, nt),
        in_specs=[pl.BlockSpec((1, TM_PROJ, D), lambda b, i: (b, i, 0)),
                  _full((1, D)), _full((1, D)), _full((D, C_END)), _full((D_ATTN, D)), _full((1, LANES)),
                  _full((LANES, 2 * D_ATTN)), _full((HALO, D_CONV)),
                  _full((CONV_WIDTH, D_CONV)), _full((1, D_CONV)), _full((1, D_CONV)), _full((1, D_CONV)),
                  _full((D_CONV, D_CONV)), _full((SUBLANES, LANES))],
        out_specs=(head_blk, head_blk,
                   pl.BlockSpec((1, N_HEADS, 1, V_ROWS, TM_PROJ), lambda b, i: (b, 0, i, 0, 0)),
                   row_blk, row_blk),
        out_shape=(jax.ShapeDtypeStruct((B, N_HEADS, S, LANES), BF16),) * 2
                  + (jax.ShapeDtypeStruct((B, N_HEADS, nt, V_ROWS, TM_PROJ), BF16),
                     jax.ShapeDtypeStruct((B, S, D_ATTN), F32),
                     jax.ShapeDtypeStruct((B, S, D_CONV), BF16)),
        scratch=[pltpu.VMEM((1, LANES), F32), pltpu.VMEM((TM_PROJ, TM_PROJ), BF16),
                 pltpu.VMEM((HALO + TM_PROJ, D_CONV), F32), pltpu.VMEM((TM_PROJ, D_CONV), F32)])

    hp = HEADS_PER_STEP
    att = _call(
        _attn_body, "fox_attn", 56, (qp, kp, vtp, k_meta, vt_meta, ga),
        grid=(B, N_HEADS // hp, S // TQ),
        in_specs=[pl.BlockSpec((1, hp, TQ, LANES), lambda b, h, i: (b, h, i, 0)),
                  pl.BlockSpec((1, hp, S, LANES), lambda b, h, i: (b, h, 0, 0)),
                  pl.BlockSpec((1, hp, nt, V_ROWS, TK), lambda b, h, i: (b, h, 0, 0, 0)),
                  pl.BlockSpec((hp, N_META, LANES), lambda b, h, i: (h, 0, 0)),
                  pl.BlockSpec((hp, V_ROWS, META_PAD), lambda b, h, i: (h, 0, 0)),
                  pl.BlockSpec((1, TQ, hp * HEAD_DIM), lambda b, h, i: (b, i, h))],
        out_specs=pl.BlockSpec((1, TQ, hp * HEAD_DIM), lambda b, h, i: (b, i, h)),
        out_shape=jax.ShapeDtypeStruct((B, S, D_ATTN), BF16),
        scratch=[pltpu.VMEM((SCORE_SLOTS, TK, TQ), F32), pltpu.VMEM((2, TK, TQ), F32),
                 pltpu.VMEM((hp, 1, TQ), F32), pltpu.VMEM((hp, V_ROWS, TQ), F32)])

    tile_blk = pl.BlockSpec((1, TM_OUT, D_ATTN), lambda b, i: (b, i, 0))
    return _call(
        _out_body, "out_proj", 56,
        (x, att, yc, w_out[0].astype(BF16), ALPHA * lig, ALPHA * lib,
         ln_out_g[0].reshape(1, D), ln_out_b[0].reshape(1, D)),
        grid=(B, S // TM_OUT),
        in_specs=[pl.BlockSpec((1, TM_OUT, D), lambda b, i: (b, i, 0)), tile_blk, tile_blk,
                  _full((D, D)), _full((1, D)), _full((1, D)), _full((1, D)), _full((1, D))],
        out_specs=pl.BlockSpec((1, TM_OUT, D), lambda b, i: (b, i, 0)),
        out_shape=jax.ShapeDtypeStruct((B, S, D), F32))
```

```python
import numpy as np

import jax
import jax.numpy as jnp
from jax import lax
from jax.experimental import pallas as pl
from jax.experimental.pallas import tpu as pltpu

F32 = jnp.float32
BF16 = jnp.bfloat16

D_MODEL = 1024
N_META = 16
D_ATTN = 512
D_CONV = 512
N_HEADS = 8
HEAD_DIM = 64
CONV_WIDTH = 31
LN_EPS = 1e-5
ALPHA = 2.0 ** 0.25
V7X_VMEM_BYTES = 64 * 1024 * 1024
LANES = 128
SUBLANES = 8
BF16_ROWS = 16
META_PAD = 128
HALO = 32
MASK_BIAS = 1e30
LOG2E = 1.4426950408889634
V_ROWS = HEAD_DIM + BF16_ROWS

C_Q = 0
C_K = D_ATTN
C_F = 2 * D_ATTN
C_GA = C_F + LANES
C_U = C_GA + D_ATTN
C_UG = C_U + D_CONV
C_GC = C_UG + D_CONV
C_END = C_GC + D_CONV

TM_PROJ = 512
TQ = 512
TK = 512
HEADS_PER_STEP = 8
LOOKAHEAD = 2
SCORE_SLOTS = 4
TM_OUT = 1024
OUT_ROWS = 256
CONV_ROWS = 64


def _dot(a, b):
    return jnp.dot(a, b, preferred_element_type=F32)


def _dot_nt(a, b):
    return lax.dot_general(a, b, (((1,), (1,)), ((), ())), preferred_element_type=F32)


def _layer_norm(x, g, b):
    mu = jnp.mean(x, axis=-1, keepdims=True)
    xc = x - mu
    var = jnp.mean(xc * xc, axis=-1, keepdims=True)
    return xc * lax.rsqrt(var + LN_EPS) * g + b


def _sigmoid(x):
    return 0.5 * jnp.tanh(0.5 * x) + 0.5


def _silu(x):
    t = 0.5 * x
    return t * jnp.tanh(t) + t


def _log_sigmoid(x):
    return jnp.minimum(x, 0.0) - jnp.log1p(jnp.exp(-jnp.abs(x)))


def _split3(c):
    hi = c.astype(BF16).astype(F32)
    r = c - hi
    mid = r.astype(BF16).astype(F32)
    lo = (r - mid).astype(BF16).astype(F32)
    return hi, mid, lo


def _bias_operand(c, lane):
    hi, mid, lo = _split3(c * LOG2E)
    one = jnp.where(lane < 4 * N_HEADS, 1.0, 0.0)
    return jnp.where(lane < N_HEADS, hi,
                     jnp.where(lane < 2 * N_HEADS, mid,
                               jnp.where(lane < 3 * N_HEADS, lo, one))).astype(BF16)


def _forget_logs(hb, w_ref, bf_ref, lane):
    f = _dot(hb, w_ref[:, C_F:C_F + LANES])
    return jnp.where(lane < 3 * N_HEADS, _log_sigmoid(f + bf_ref[...]), 0.0)


def _exact_tri_dot(tri, x):
    hi, mid, lo = _split3(x)
    return _dot(tri, hi.astype(BF16)) + _dot(tri, mid.astype(BF16)) + _dot(tri, lo.astype(BF16))


def _head_tiles(proj, bias_cols, lane):
    low_half = lane < HEAD_DIM
    tiles = []
    for p in range(proj.shape[1] // LANES):
        pair = proj[:, p * LANES:(p + 1) * LANES]
        extra = bias_cols[:, p * LANES:(p + 1) * LANES]
        tiles.append(jnp.where(low_half, pair, extra).astype(BF16))
        tiles.append(pltpu.roll(jnp.where(low_half, extra, pair), HEAD_DIM, axis=1).astype(BF16))
    return tiles


def _ones_rows(cols, valid_cols):
    r = lax.broadcasted_iota(jnp.int32, (BF16_ROWS, cols), 0)
    c = lax.broadcasted_iota(jnp.int32, (BF16_ROWS, cols), 1)
    return jnp.where((r == 0) & (c < valid_cols), 1.0, 0.0).astype(BF16)


def _meta_body(x_ref, lng_ref, lnb_ref, w_ref, wvt_ref, bf_ref, pm_ref,
               k_ref, vt_ref, glu_ref):
    rows = x_ref.shape[0]
    hb = _layer_norm(x_ref[...], lng_ref[...], lnb_ref[...]).astype(BF16)
    lane = lax.broadcasted_iota(jnp.int32, (rows, LANES), 1)
    lf = _forget_logs(hb, w_ref, bf_ref, lane)
    r = lax.broadcasted_iota(jnp.int32, (rows, rows), 0)
    c = lax.broadcasted_iota(jnp.int32, (rows, rows), 1)
    upper = jnp.where((c > r) & (c < N_META), 1.0, 0.0).astype(BF16)
    cmat = _bias_operand(-_exact_tri_dot(upper, lf), lane)
    k_tiles = _head_tiles(_dot(hb, w_ref[:, C_K:C_F]), _dot(cmat, pm_ref[:, C_K:C_F]), lane)
    vt = _dot_nt(wvt_ref[...], hb)
    real = lax.broadcasted_iota(jnp.int32, (D_ATTN, rows), 1) < N_META
    vt = jnp.where(real, vt, 0.0).astype(BF16)
    ones = _ones_rows(rows, N_META)
    for h in range(N_HEADS):
        k_ref[h] = k_tiles[h][0:N_META, :]
        vt_ref[h, 0:HEAD_DIM, :] = vt[h * HEAD_DIM:(h + 1) * HEAD_DIM, :]
        vt_ref[h, HEAD_DIM:V_ROWS, :] = ones
    u = _dot(hb, w_ref[:, C_U:C_UG])
    ug = _dot(hb, w_ref[:, C_UG:C_GC])
    glu = u * _sigmoid(ug)
    glu_ref[0:HALO - N_META, :] = jnp.zeros((HALO - N_META, D_CONV), F32)
    glu_ref[HALO - N_META:HALO, :] = glu[0:N_META, :]


def _shift_rows(part, r):
    if r == 0:
        return part[0:CONV_ROWS, :]
    from_next = lax.broadcasted_iota(jnp.int32, (SUBLANES, part.shape[1]), 0) < r
    slabs = [part[k:k + SUBLANES, :] for k in range(0, part.shape[0], SUBLANES)]
    return jnp.concatenate([pltpu.roll(jnp.where(from_next, nxt, cur), SUBLANES - r, axis=0)
                            for cur, nxt in zip(slabs[:-1], slabs[1:])], axis=0)


def _conv_rows(buf_ref, cw_ref, cb_ref, out_ref, base, zero):
    shift = HALO - (CONV_WIDTH - 1)
    for lo in range(0, D_CONV, LANES):
        cols = slice(lo, lo + LANES)
        out = jnp.broadcast_to(cb_ref[:, cols] + zero, (CONV_ROWS, LANES))
        for r in range(SUBLANES):
            window = CONV_ROWS + (SUBLANES if r else 0)
            part = None
            for j in range(CONV_WIDTH):
                if (shift + j) % SUBLANES != r:
                    continue
                start = base + shift + j - r
                term = cw_ref[j:j + 1, cols] * buf_ref[start:start + window, cols]
                part = term if part is None else part + term
            out = out + _shift_rows(part, r)
        out_ref[base:base + CONV_ROWS, cols] = out


def _proj_body(x_ref, lng_ref, lnb_ref, w_ref, wvt_ref, bf_ref, pm_ref, meta_glu_ref,
               cw_ref, cb_ref, lcg_ref, lcb_ref, wpw_ref, zero_ref,
               q_ref, k_ref, vt_ref, ga_ref, yc_ref,
               carry_ref, tri_ref, buf_ref, conv_ref):
    tm = x_ref.shape[1]
    first = pl.program_id(1) == 0

    @pl.when(first)
    def _start_sequence():
        carry_ref[...] = jnp.zeros_like(carry_ref)
        r = lax.broadcasted_iota(jnp.int32, (tm, tm), 0)
        c = lax.broadcasted_iota(jnp.int32, (tm, tm), 1)
        tri_ref[...] = jnp.where(r >= c, 1.0, 0.0).astype(BF16)
        buf_ref[0:HALO, :] = meta_glu_ref[...]

    @pl.when(jnp.logical_not(first))
    def _carry_conv_context():
        buf_ref[0:HALO, :] = buf_ref[tm:tm + HALO, :]

    def zero_after(tile):
        bits = lax.bitcast_convert_type(tile, jnp.int32) & zero_ref[...]
        return lax.bitcast_convert_type(bits, F32)[0:1, :]

    hb = _layer_norm(x_ref[0], lng_ref[...], lnb_ref[...]).astype(BF16)
    for lo in range(0, D_CONV, D_CONV // 2):
        u = _dot(hb, w_ref[:, C_U + lo:C_U + lo + D_CONV // 2])
        ug = _dot(hb, w_ref[:, C_UG + lo:C_UG + lo + D_CONV // 2])
        buf_ref[HALO:HALO + tm, lo:lo + D_CONV // 2] = u * _sigmoid(ug)

    lane = lax.broadcasted_iota(jnp.int32, (tm, LANES), 1)
    lf = _forget_logs(hb, w_ref, bf_ref, lane)
    c = _exact_tri_dot(tri_ref[...], lf) + carry_ref[...]
    carry_ref[...] = c[tm - 1:tm, :]
    cmat = _bias_operand(c, lane)
    ones = _ones_rows(tm, tm)

    def qk_heads(out_ref, col0, first_pair, n_pairs):
        cols = slice(col0 + first_pair * LANES, col0 + (first_pair + n_pairs) * LANES)
        ext = _dot(cmat, pm_ref[:, cols])
        tiles = _head_tiles(_dot(hb, w_ref[:, cols]), ext, lane)
        for h, tile in enumerate(tiles):
            out_ref[0, 2 * first_pair + h] = tile
        return zero_after(ext[tm - SUBLANES:, (n_pairs - 1) * LANES:])

    def v_heads(first_head, n_heads):
        rows = slice(first_head * HEAD_DIM, (first_head + n_heads) * HEAD_DIM)
        vt = _dot_nt(wvt_ref[rows, :], hb)
        for h in range(n_heads):
            vt_ref[0, first_head + h, 0, 0:HEAD_DIM, :] = vt[h * HEAD_DIM:(h + 1) * HEAD_DIM, :].astype(BF16)
            vt_ref[0, first_head + h, 0, HEAD_DIM:V_ROWS, :] = ones
        return zero_after(vt[n_heads * HEAD_DIM - SUBLANES:, tm - LANES:])

    def attn_gate():
        ga = _dot(hb, w_ref[:, C_GA:C_U])
        ga_ref[0] = ga
        return zero_after(ga[tm - SUBLANES:, D_ATTN - LANES:])

    half = N_HEADS // 2
    quarter = N_HEADS // 4
    gates = []

    def conv_gate():
        gates.append(_silu(_dot(hb, w_ref[:, C_GC:C_END])))
        return zero_after(gates[0][tm - SUBLANES:, D_CONV - LANES:])

    mxu_groups = [conv_gate, attn_gate, lambda: v_heads(0, half), lambda: v_heads(half, half),
                  lambda: qk_heads(q_ref, C_Q, 0, quarter), lambda: qk_heads(q_ref, C_Q, quarter, quarter),
                  lambda: qk_heads(k_ref, C_K, 0, quarter), lambda: qk_heads(k_ref, C_K, quarter, quarter)]
    chunks = list(range(0, tm, CONV_ROWS))
    per_chunk = -(-len(mxu_groups) // len(chunks))
    after = jnp.zeros((1, LANES), F32)
    for n, base in enumerate(chunks):
        _conv_rows(buf_ref, cw_ref, cb_ref, conv_ref, base, after)
        for group in mxu_groups[n * per_chunk:(n + 1) * per_chunk]:
            after = group()

    z = _silu(_layer_norm(conv_ref[...], lcg_ref[...], lcb_ref[...]))
    yc_ref[0] = (_dot(z.astype(BF16), wpw_ref[...]) * gates[0]).astype(BF16)


def _attn_body(q_ref, k_ref, vt_ref, km_ref, vmt_ref, ga_ref, o_ref, s_ref, pen_ref, m_ref, acc_ref):
    qi = pl.program_id(2)
    heads = range(HEADS_PER_STEP)

    @pl.when((pl.program_id(0) == 0) & (qi == 0))
    def _causal_penalty():
        key = lax.broadcasted_iota(jnp.int32, (TK, TQ), 0)
        qry = lax.broadcasted_iota(jnp.int32, (TK, TQ), 1)
        pen_ref[0] = jnp.zeros((TK, TQ), F32)
        pen_ref[1] = jnp.where(key <= qry, 0.0, -MASK_BIAS)

    pad = jnp.zeros((META_PAD - N_META, TQ), BF16)

    def meta_weights(s, m):
        return jnp.concatenate([jnp.exp2(s - m).astype(BF16), pad], axis=0)

    early = range(LOOKAHEAD)
    s_meta = [_dot_nt(km_ref[hh], q_ref[0, hh]) for hh in early]
    m_meta = [jnp.max(s, axis=0, keepdims=True) for s in s_meta]
    for hh in heads:
        if hh < LOOKAHEAD:
            m_ref[hh] = m_meta[hh]
            acc_ref[hh] = _dot(vmt_ref[hh], meta_weights(s_meta[hh], m_meta[hh]))
        else:
            m_ref[hh] = jnp.full((1, TQ), -MASK_BIAS, F32)
            acc_ref[hh] = jnp.zeros((V_ROWS, TQ), F32)

    def produce(j, hh, penalty):
        off = pl.multiple_of(j * TK, TK)
        s = _dot_nt(k_ref[0, hh, pl.ds(off, TK), :], q_ref[0, hh])
        if penalty is not None:
            s = s + pen_ref[penalty]
        s_ref[hh % SCORE_SLOTS] = s
        return jnp.max(s, axis=0, keepdims=True)

    def consume(j, hh, col_max):
        m = m_ref[hh]
        m_new = jnp.maximum(m, col_max)
        p = jnp.exp2(s_ref[hh % SCORE_SLOTS] - m_new).astype(BF16)
        acc_ref[hh] = acc_ref[hh] * jnp.exp2(m - m_new) + _dot(vt_ref[0, hh, j], p)
        m_ref[hh] = m_new

    def diagonal_flag(j):
        return jnp.where(j < qi, 0, 1)

    def body(j, pending):
        pending = list(pending)
        for hh in heads:
            ahead = hh + LOOKAHEAD
            if ahead < HEADS_PER_STEP:
                pending.append(produce(j, ahead, None))
            else:
                pending.append(produce(j + 1, ahead - HEADS_PER_STEP, diagonal_flag(j + 1)))
            consume(j, hh, pending.pop(0))
        return tuple(pending)

    half = TK // 2

    def produce_diagonal(hh):
        slot = hh % SCORE_SLOTS
        off = pl.multiple_of(qi * TK, TK)
        keys = jnp.concatenate([k_ref[0, hh, pl.ds(off, half), :], km_ref[hh]], axis=0)
        scores = _dot_nt(keys, q_ref[0, hh])
        top = scores[0:half, :] + pen_ref[1, 0:half, :]
        meta = scores[half:, :]
        low = (_dot_nt(k_ref[0, hh, pl.ds(off + half, half), :], q_ref[0, hh, half:, :])
               + pen_ref[1, half:, half:])
        s_ref[slot, 0:half, :] = top
        s_ref[slot, half:, half:] = low
        top_max = jnp.maximum(jnp.max(top, axis=0, keepdims=True), jnp.max(meta, axis=0, keepdims=True))
        low_max = jnp.max(low, axis=0, keepdims=True)
        col_max = jnp.concatenate([top_max[:, 0:half], jnp.maximum(top_max[:, half:], low_max)], axis=1)
        return col_max, meta

    def consume_diagonal(hh, produced):
        col_max, meta = produced
        slot = hh % SCORE_SLOTS
        m = m_ref[hh]
        m_new = jnp.maximum(m, col_max)
        p_top = jnp.exp2(s_ref[slot, 0:half, :] - m_new).astype(BF16)
        p_low = jnp.exp2(s_ref[slot, half:, half:] - m_new[:, half:]).astype(BF16)
        acc = (acc_ref[hh] * jnp.exp2(m - m_new) + _dot(vt_ref[0, hh, qi, :, 0:half], p_top)
               + _dot(vmt_ref[hh], meta_weights(meta, m_new)))
        acc_ref[hh, :, 0:half] = acc[:, 0:half]
        acc_ref[hh, :, half:] = acc[:, half:] + _dot(vt_ref[0, hh, qi, :, half:], p_low)
        m_ref[hh] = m_new

    def normalized(hh):
        return acc_ref[hh, 0:HEAD_DIM, :] / acc_ref[hh, HEAD_DIM:HEAD_DIM + 1, :]

    pending = lax.fori_loop(0, qi, body, tuple(produce(0, hh, diagonal_flag(0)) for hh in range(LOOKAHEAD)))
    pending = list(pending)
    for hh in heads:
        if hh + LOOKAHEAD < HEADS_PER_STEP:
            pending.append(produce_diagonal(hh + LOOKAHEAD))
        if hh < LOOKAHEAD:
            consume(qi, hh, pending.pop(0))
        else:
            consume_diagonal(hh, pending.pop(0))
        if hh % 2:
            cols = slice((hh - 1) * HEAD_DIM, (hh + 1) * HEAD_DIM)
            pair = jnp.concatenate([normalized(hh - 1), normalized(hh)], axis=0).T
            o_ref[0, :, cols] = (pair * _silu(ga_ref[0, :, cols])).astype(BF16)


def _out_body(x_ref, ya_ref, yc_ref, wout_ref, lig_ref, lib_ref, log_ref, lob_ref, o_ref):
    tm = x_ref.shape[1]
    for r0 in range(0, tm, OUT_ROWS):
        rows = slice(r0, r0 + OUT_ROWS)
        y = _dot(jnp.concatenate([ya_ref[0, rows, :], yc_ref[0, rows, :]], axis=1), wout_ref[...])
        scaled_h = _layer_norm(x_ref[0, rows, :], lig_ref[...], lib_ref[...])
        o_ref[0, rows, :] = _layer_norm(scaled_h + y, log_ref[...], lob_ref[...])


def _placement():
    pm = np.zeros((LANES, 2 * D_ATTN), np.float32)
    for h in range(N_HEADS):
        base = (h // 2) * LANES + (HEAD_DIM if h % 2 == 0 else 0)
        for g in range(3):
            pm[g * N_HEADS + h, C_Q + base + g] = 1.0
            pm[3 * N_HEADS + h, C_Q + base + 3 + g] = 1.0
            pm[3 * N_HEADS + h, C_K + base + g] = 1.0
            pm[g * N_HEADS + h, C_K + base + 3 + g] = -1.0
    return jnp.asarray(pm, BF16)


def _full(shape):
    return pl.BlockSpec(shape, lambda *_: (0,) * len(shape))


def _call(body, name, vmem_mib, operands, out_shape, grid=(), in_specs=None, out_specs=None, scratch=()):
    assert vmem_mib * 1024 * 1024 <= V7X_VMEM_BYTES
    grid_args = {} if in_specs is None else dict(grid=grid, in_specs=in_specs, out_specs=out_specs)
    return pl.pallas_call(
        body, out_shape=out_shape, scratch_shapes=list(scratch), name=name,
        compiler_params=pltpu.CompilerParams(dimension_semantics=("arbitrary",) * len(grid),
                                             vmem_limit_bytes=vmem_mib * 1024 * 1024),
        **grid_args)(*operands)


def kernel(x, meta, ln_in_g, ln_in_b, w_in, b_f, conv_w, conv_b, ln_conv_g, ln_conv_b,
           w_pw, w_out, ln_out_g, ln_out_b):
    B, S, D = x.shape
    assert D == D_MODEL and w_in.shape[0] == 1 and meta.shape == (N_META, D_MODEL)
    assert S % TM_PROJ == 0 and S % TM_OUT == 0 and S % TQ == 0 and TQ == TK == TM_PROJ
    assert TM_OUT % OUT_ROWS == 0 and TM_PROJ % CONV_ROWS == 0 and N_HEADS % HEADS_PER_STEP == 0
    assert HEADS_PER_STEP % SCORE_SLOTS == 0 and SCORE_SLOTS > LOOKAHEAD

    w = w_in[0]
    scale = HEAD_DIM ** -0.5 * LOG2E
    wf = jnp.pad(jnp.tile(w[:, 3 * D_ATTN:3 * D_ATTN + N_HEADS], (1, 3)), ((0, 0), (0, LANES - 3 * N_HEADS)))
    w_all = jnp.concatenate([
        w[:, 0:D_ATTN] * scale, w[:, D_ATTN:2 * D_ATTN], wf, w[:, 3 * D_ATTN + N_HEADS:]], axis=1).astype(BF16)
    wvt = w[:, 2 * D_ATTN:3 * D_ATTN].T.astype(BF16)
    bf_row = jnp.pad(jnp.tile(b_f[0], 3), (0, LANES - 3 * N_HEADS)).reshape(1, LANES)
    pm = _placement()
    lig = ln_in_g.reshape(1, D)
    lib = ln_in_b.reshape(1, D)
    meta_pad = jnp.pad(meta, ((0, META_PAD - N_META), (0, 0)))

    k_meta, vt_meta, glu_meta = _call(
        _meta_body, "meta_proj", 48, (meta_pad, lig, lib, w_all, wvt, bf_row, pm),
        out_shape=(jax.ShapeDtypeStruct((N_HEADS, N_META, LANES), BF16),
                   jax.ShapeDtypeStruct((N_HEADS, V_ROWS, META_PAD), BF16),
                   jax.ShapeDtypeStruct((HALO, D_CONV), F32)))

    nt = S // TM_PROJ
    head_blk = pl.BlockSpec((1, N_HEADS, TM_PROJ, LANES), lambda b, i: (b, 0, i, 0))
    row_blk = pl.BlockSpec((1, TM_PROJ, D_ATTN), lambda b, i: (b, i, 0))
    qp, kp, vtp, ga, yc = _call(
        _proj_body, "in_proj", 56,
        (x, lig, lib, w_all, wvt, bf_row, pm, glu_meta,
         conv_w[0], conv_b[0].reshape(1, D_CONV), ln_conv_g[0].reshape(1, D_CONV), ln_conv_b[0].reshape(1, D_CONV),
         w_pw[0].astype(BF16), jnp.zeros((SUBLANES, LANES), jnp.int32)),
        grid=(B, nt),
        in_specs=[pl.BlockSpec((1, TM_PROJ, D), lambda b, i: (b, i, 0)),
                  _full((1, D)), _full((1, D)), _full((D, C_END)), _full((D_ATTN, D)), _full((1, LANES)),
                  _full((LANES, 2 * D_ATTN)), _full((HALO, D_CONV)),
                  _full((CONV_WIDTH, D_CONV)), _full((1, D_CONV)), _full((1, D_CONV)), _full((1, D_CONV)),
                  _full((D_CONV, D_CONV)), _full((SUBLANES, LANES))],
        out_specs=(head_blk, head_blk,
                   pl.BlockSpec((1, N_HEADS, 1, V_ROWS, TM_PROJ), lambda b, i: (b, 0, i, 0, 0)),
                   row_blk, row_blk),
        out_shape=(jax.ShapeDtypeStruct((B, N_HEADS, S, LANES), BF16),) * 2
                  + (jax.ShapeDtypeStruct((B, N_HEADS, nt, V_ROWS, TM_PROJ), BF16),
                     jax.ShapeDtypeStruct((B, S, D_ATTN), F32),
                     jax.ShapeDtypeStruct((B, S, D_CONV), BF16)),
        scratch=[pltpu.VMEM((1, LANES), F32), pltpu.VMEM((TM_PROJ, TM_PROJ), BF16),
                 pltpu.VMEM((HALO + TM_PROJ, D_CONV), F32), pltpu.VMEM((TM_PROJ, D_CONV), F32)])

    hp = HEADS_PER_STEP
    att = _call(
        _attn_body, "fox_attn", 56, (qp, kp, vtp, k_meta, vt_meta, ga),
        grid=(B, N_HEADS // hp, S // TQ),
        in_specs=[pl.BlockSpec((1, hp, TQ, LANES), lambda b, h, i: (b, h, i, 0)),
                  pl.BlockSpec((1, hp, S, LANES), lambda b, h, i: (b, h, 0, 0)),
                  pl.BlockSpec((1, hp, nt, V_ROWS, TK), lambda b, h, i: (b, h, 0, 0, 0)),
                  pl.BlockSpec((hp, N_META, LANES), lambda b, h, i: (h, 0, 0)),
                  pl.BlockSpec((hp, V_ROWS, META_PAD), lambda b, h, i: (h, 0, 0)),
                  pl.BlockSpec((1, TQ, hp * HEAD_DIM), lambda b, h, i: (b, i, h))],
        out_specs=pl.BlockSpec((1, TQ, hp * HEAD_DIM), lambda b, h, i: (b, i, h)),
        out_shape=jax.ShapeDtypeStruct((B, S, D_ATTN), BF16),
        scratch=[pltpu.VMEM((SCORE_SLOTS, TK, TQ), F32), pltpu.VMEM((2, TK, TQ), F32),
                 pltpu.VMEM((hp, 1, TQ), F32), pltpu.VMEM((hp, V_ROWS, TQ), F32)])

    tile_blk = pl.BlockSpec((1, TM_OUT, D_ATTN), lambda b, i: (b, i, 0))
    return _call(
        _out_body, "out_proj", 56,
        (x, att, yc, w_out[0].astype(BF16), ALPHA * lig, ALPHA * lib,
         ln_out_g[0].reshape(1, D), ln_out_b[0].reshape(1, D)),
        grid=(B, S // TM_OUT),
        in_specs=[pl.BlockSpec((1, TM_OUT, D), lambda b, i: (b, i, 0)), tile_blk, tile_blk,
                  _full((D, D)), _full((1, D)), _full((1, D)), _full((1, D)), _full((1, D))],
        out_specs=pl.BlockSpec((1, TM_OUT, D), lambda b, i: (b, i, 0)),
        out_shape=jax.ShapeDtypeStruct((B, S, D), F32))
```

```python
import numpy as np

import jax
import jax.numpy as jnp
from jax import lax
from jax.experimental import pallas as pl
from jax.experimental.pallas import tpu as pltpu

F32 = jnp.float32
BF16 = jnp.bfloat16

D_MODEL = 1024
N_META = 16
D_ATTN = 512
D_CONV = 512
N_HEADS = 8
HEAD_DIM = 64
CONV_WIDTH = 31
LN_EPS = 1e-5
ALPHA = 2.0 ** 0.25
V7X_VMEM_BYTES = 64 * 1024 * 1024
LANES = 128
SUBLANES = 8
BF16_ROWS = 16
META_PAD = 128
HALO = 32
MASK_BIAS = 1e30
LOG2E = 1.4426950408889634
V_ROWS = HEAD_DIM + BF16_ROWS

C_Q = 0
C_K = D_ATTN
C_F = 2 * D_ATTN
C_GA = C_F + LANES
C_U = C_GA + D_ATTN
C_UG = C_U + D_CONV
C_GC = C_UG + D_CONV
C_END = C_GC + D_CONV

TM_PROJ = 512
TQ = 512
TK = 512
HEADS_PER_STEP = 8
LOOKAHEAD = 2
SCORE_SLOTS = 4
TM_OUT = 1024
OUT_ROWS = 256
CONV_ROWS = 64


def _dot(a, b):
    return jnp.dot(a, b, preferred_element_type=F32)


def _dot_nt(a, b):
    return lax.dot_general(a, b, (((1,), (1,)), ((), ())), preferred_element_type=F32)


def _layer_norm(x, g, b):
    mu = jnp.mean(x, axis=-1, keepdims=True)
    xc = x - mu
    var = jnp.mean(xc * xc, axis=-1, keepdims=True)
    return xc * lax.rsqrt(var + LN_EPS) * g + b


def _sigmoid(x):
    return 0.5 * jnp.tanh(0.5 * x) + 0.5


def _silu(x):
    t = 0.5 * x
    return t * jnp.tanh(t) + t


def _log_sigmoid(x):
    return jnp.minimum(x, 0.0) - jnp.log1p(jnp.exp(-jnp.abs(x)))


def _split3(c):
    hi = c.astype(BF16).astype(F32)
    r = c - hi
    mid = r.astype(BF16).astype(F32)
    lo = (r - mid).astype(BF16).astype(F32)
    return hi, mid, lo


def _bias_operand(c, lane):
    hi, mid, lo = _split3(c * LOG2E)
    one = jnp.where(lane < 4 * N_HEADS, 1.0, 0.0)
    return jnp.where(lane < N_HEADS, hi,
                     jnp.where(lane < 2 * N_HEADS, mid,
                               jnp.where(lane < 3 * N_HEADS, lo, one))).astype(BF16)


def _forget_logs(hb, w_ref, bf_ref, lane):
    f = _dot(hb, w_ref[:, C_F:C_F + LANES])
    return jnp.where(lane < 3 * N_HEADS, _log_sigmoid(f + bf_ref[...]), 0.0)


def _exact_tri_dot(tri, x):
    hi, mid, lo = _split3(x)
    return _dot(tri, hi.astype(BF16)) + _dot(tri, mid.astype(BF16)) + _dot(tri, lo.astype(BF16))


def _head_tiles(proj, bias_cols, lane):
    low_half = lane < HEAD_DIM
    tiles = []
    for p in range(proj.shape[1] // LANES):
        pair = proj[:, p * LANES:(p + 1) * LANES]
        extra = bias_cols[:, p * LANES:(p + 1) * LANES]
        tiles.append(jnp.where(low_half, pair, extra).astype(BF16))
        tiles.append(pltpu.roll(jnp.where(low_half, extra, pair), HEAD_DIM, axis=1).astype(BF16))
    return tiles


def _ones_rows(cols, valid_cols):
    r = lax.broadcasted_iota(jnp.int32, (BF16_ROWS, cols), 0)
    c = lax.broadcasted_iota(jnp.int32, (BF16_ROWS, cols), 1)
    return jnp.where((r == 0) & (c < valid_cols), 1.0, 0.0).astype(BF16)


def _meta_body(x_ref, lng_ref, lnb_ref, w_ref, wvt_ref, bf_ref, pm_ref,
               k_ref, vt_ref, glu_ref):
    rows = x_ref.shape[0]
    hb = _layer_norm(x_ref[...], lng_ref[...], lnb_ref[...]).astype(BF16)
    lane = lax.broadcasted_iota(jnp.int32, (rows, LANES), 1)
    lf = _forget_logs(hb, w_ref, bf_ref, lane)
    r = lax.broadcasted_iota(jnp.int32, (rows, rows), 0)
    c = lax.broadcasted_iota(jnp.int32, (rows, rows), 1)
    upper = jnp.where((c > r) & (c < N_META), 1.0, 0.0).astype(BF16)
    cmat = _bias_operand(-_exact_tri_dot(upper, lf), lane)
    k_tiles = _head_tiles(_dot(hb, w_ref[:, C_K:C_F]), _dot(cmat, pm_ref[:, C_K:C_F]), lane)
    vt = _dot_nt(wvt_ref[...], hb)
    real = lax.broadcasted_iota(jnp.int32, (D_ATTN, rows), 1) < N_META
    vt = jnp.where(real, vt, 0.0).astype(BF16)
    ones = _ones_rows(rows, N_META)
    for h in range(N_HEADS):
        k_ref[h] = k_tiles[h][0:N_META, :]
        vt_ref[h, 0:HEAD_DIM, :] = vt[h * HEAD_DIM:(h + 1) * HEAD_DIM, :]
        vt_ref[h, HEAD_DIM:V_ROWS, :] = ones
    u = _dot(hb, w_ref[:, C_U:C_UG])
    ug = _dot(hb, w_ref[:, C_UG:C_GC])
    glu = u * _sigmoid(ug)
    glu_ref[0:HALO - N_META, :] = jnp.zeros((HALO - N_META, D_CONV), F32)
    glu_ref[HALO - N_META:HALO, :] = glu[0:N_META, :]


def _shift_rows(part, r):
    if r == 0:
        return part[0:CONV_ROWS, :]
    from_next = lax.broadcasted_iota(jnp.int32, (SUBLANES, part.shape[1]), 0) < r
    slabs = [part[k:k + SUBLANES, :] for k in range(0, part.shape[0], SUBLANES)]
    return jnp.concatenate([pltpu.roll(jnp.where(from_next, nxt, cur), SUBLANES - r, axis=0)
                            for cur, nxt in zip(slabs[:-1], slabs[1:])], axis=0)


def _conv_rows(buf_ref, cw_ref, cb_ref, out_ref, base, zero):
    shift = HALO - (CONV_WIDTH - 1)
    for lo in range(0, D_CONV, LANES):
        cols = slice(lo, lo + LANES)
        out = jnp.broadcast_to(cb_ref[:, cols] + zero, (CONV_ROWS, LANES))
        for r in range(SUBLANES):
            window = CONV_ROWS + (SUBLANES if r else 0)
            part = None
            for j in range(CONV_WIDTH):
                if (shift + j) % SUBLANES != r:
                    continue
                start = base + shift + j - r
                term = cw_ref[j:j + 1, cols] * buf_ref[start:start + window, cols]
                part = term if part is None else part + term
            out = out + _shift_rows(part, r)
        out_ref[base:base + CONV_ROWS, cols] = out


def _proj_body(x_ref, lng_ref, lnb_ref, w_ref, wvt_ref, bf_ref, pm_ref, meta_glu_ref,
               cw_ref, cb_ref, lcg_ref, lcb_ref, wpw_ref, zero_ref,
               q_ref, k_ref, vt_ref, ga_ref, yc_ref,
               carry_ref, tri_ref, buf_ref, conv_ref):
    tm = x_ref.shape[1]
    first = pl.program_id(1) == 0

    @pl.when(first)
    def _start_sequence():
        carry_ref[...] = jnp.zeros_like(carry_ref)
        r = lax.broadcasted_iota(jnp.int32, (tm, tm), 0)
        c = lax.broadcasted_iota(jnp.int32, (tm, tm), 1)
        tri_ref[...] = jnp.where(r >= c, 1.0, 0.0).astype(BF16)
        buf_ref[0:HALO, :] = meta_glu_ref[...]

    @pl.when(jnp.logical_not(first))
    def _carry_conv_context():
        buf_ref[0:HALO, :] = buf_ref[tm:tm + HALO, :]

    def zero_after(tile):
        bits = lax.bitcast_convert_type(tile, jnp.int32) & zero_ref[...]
        return lax.bitcast_convert_type(bits, F32)[0:1, :]

    hb = _layer_norm(x_ref[0], lng_ref[...], lnb_ref[...]).astype(BF16)
    for lo in range(0, D_CONV, D_CONV // 2):
        u = _dot(hb, w_ref[:, C_U + lo:C_U + lo + D_CONV // 2])
        ug = _dot(hb, w_ref[:, C_UG + lo:C_UG + lo + D_CONV // 2])
        buf_ref[HALO:HALO + tm, lo:lo + D_CONV // 2] = u * _sigmoid(ug)

    lane = lax.broadcasted_iota(jnp.int32, (tm, LANES), 1)
    lf = _forget_logs(hb, w_ref, bf_ref, lane)
    c = _exact_tri_dot(tri_ref[...], lf) + carry_ref[...]
    carry_ref[...] = c[tm - 1:tm, :]
    cmat = _bias_operand(c, lane)
    ones = _ones_rows(tm, tm)

    def qk_heads(out_ref, col0, first_pair, n_pairs):
        cols = slice(col0 + first_pair * LANES, col0 + (first_pair + n_pairs) * LANES)
        ext = _dot(cmat, pm_ref[:, cols])
        tiles = _head_tiles(_dot(hb, w_ref[:, cols]), ext, lane)
        for h, tile in enumerate(tiles):
            out_ref[0, 2 * first_pair + h] = tile
        return zero_after(ext[tm - SUBLANES:, (n_pairs - 1) * LANES:])

    def v_heads(first_head, n_heads):
        rows = slice(first_head * HEAD_DIM, (first_head + n_heads) * HEAD_DIM)
        vt = _dot_nt(wvt_ref[rows, :], hb)
        for h in range(n_heads):
            vt_ref[0, first_head + h, 0, 0:HEAD_DIM, :] = vt[h * HEAD_DIM:(h + 1) * HEAD_DIM, :].astype(BF16)
            vt_ref[0, first_head + h, 0, HEAD_DIM:V_ROWS, :] = ones
        return zero_after(vt[n_heads * HEAD_DIM - SUBLANES:, tm - LANES:])

    def attn_gate():
        ga = _dot(hb, w_ref[:, C_GA:C_U])
        ga_ref[0] = ga
        return zero_after(ga[tm - SUBLANES:, D_ATTN - LANES:])

    half = N_HEADS // 2
    quarter = N_HEADS // 4
    gates = []

    def conv_gate():
        gates.append(_silu(_dot(hb, w_ref[:, C_GC:C_END])))
        return zero_after(gates[0][tm - SUBLANES:, D_CONV - LANES:])

    mxu_groups = [conv_gate, attn_gate, lambda: v_heads(0, half), lambda: v_heads(half, half),
                  lambda: qk_heads(q_ref, C_Q, 0, quarter), lambda: qk_heads(q_ref, C_Q, quarter, quarter),
                  lambda: qk_heads(k_ref, C_K, 0, quarter), lambda: qk_heads(k_ref, C_K, quarter, quarter)]
    chunks = list(range(0, tm, CONV_ROWS))
    per_chunk = -(-len(mxu_groups) // len(chunks))
    after = jnp.zeros((1, LANES), F32)
    for n, base in enumerate(chunks):
        _conv_rows(buf_ref, cw_ref, cb_ref, conv_ref, base, after)
        for group in mxu_groups[n * per_chunk:(n + 1) * per_chunk]:
            after = group()

    z = _silu(_layer_norm(conv_ref[...], lcg_ref[...], lcb_ref[...]))
    yc_ref[0] = (_dot(z.astype(BF16), wpw_ref[...]) * gates[0]).astype(BF16)


def _attn_body(q_ref, k_ref, vt_ref, km_ref, vmt_ref, ga_ref, o_ref, s_ref, pen_ref, m_ref, acc_ref):
    qi = pl.program_id(2)
    heads = range(HEADS_PER_STEP)

    @pl.when((pl.program_id(0) == 0) & (qi == 0))
    def _causal_penalty():
        key = lax.broadcasted_iota(jnp.int32, (TK, TQ), 0)
        qry = lax.broadcasted_iota(jnp.int32, (TK, TQ), 1)
        pen_ref[0] = jnp.zeros((TK, TQ), F32)
        pen_ref[1] = jnp.where(key <= qry, 0.0, -MASK_BIAS)

    pad = jnp.zeros((META_PAD - N_META, TQ), BF16)

    def meta_weights(s, m):
        return jnp.concatenate([jnp.exp2(s - m).astype(BF16), pad], axis=0)

    early = range(LOOKAHEAD)
    s_meta = [_dot_nt(km_ref[hh], q_ref[0, hh]) for hh in early]
    m_meta = [jnp.max(s, axis=0, keepdims=True) for s in s_meta]
    for hh in heads:
        if hh < LOOKAHEAD:
            m_ref[hh] = m_meta[hh]
            acc_ref[hh] = _dot(vmt_ref[hh], meta_weights(s_meta[hh], m_meta[hh]))
        else:
            m_ref[hh] = jnp.full((1, TQ), -MASK_BIAS, F32)
            acc_ref[hh] = jnp.zeros((V_ROWS, TQ), F32)

    def produce(j, hh, penalty):
        off = pl.multiple_of(j * TK, TK)
        s = _dot_nt(k_ref[0, hh, pl.ds(off, TK), :], q_ref[0, hh])
        if penalty is not None:
            s = s + pen_ref[penalty]
        s_ref[hh % SCORE_SLOTS] = s
        return jnp.max(s, axis=0, keepdims=True)

    def consume(j, hh, col_max):
        m = m_ref[hh]
        m_new = jnp.maximum(m, col_max)
        p = jnp.exp2(s_ref[hh % SCORE_SLOTS] - m_new).astype(BF16)
        acc_ref[hh] = acc_ref[hh] * jnp.exp2(m - m_new) + _dot(vt_ref[0, hh, j], p)
        m_ref[hh] = m_new

    def diagonal_flag(j):
        return jnp.where(j < qi, 0, 1)

    def body(j, pending):
        pending = list(pending)
        for hh in heads:
            ahead = hh + LOOKAHEAD
            if ahead < HEADS_PER_STEP:
                pending.append(produce(j, ahead, None))
            else:
                pending.append(produce(j + 1, ahead - HEADS_PER_STEP, diagonal_flag(j + 1)))
            consume(j, hh, pending.pop(0))
        return tuple(pending)

    half = TK // 2

    def produce_diagonal(hh):
        slot = hh % SCORE_SLOTS
        off = pl.multiple_of(qi * TK, TK)
        keys = jnp.concatenate([k_ref[0, hh, pl.ds(off, half), :], km_ref[hh]], axis=0)
        scores = _dot_nt(keys, q_ref[0, hh])
        top = scores[0:half, :] + pen_ref[1, 0:half, :]
        meta = scores[half:, :]
        low = (_dot_nt(k_ref[0, hh, pl.ds(off + half, half), :], q_ref[0, hh, half:, :])
               + pen_ref[1, half:, half:])
        s_ref[slot, 0:half, :] = top
        s_ref[slot, half:, half:] = low
        top_max = jnp.maximum(jnp.max(top, axis=0, keepdims=True), jnp.max(meta, axis=0, keepdims=True))
        low_max = jnp.max(low, axis=0, keepdims=True)
        col_max = jnp.concatenate([top_max[:, 0:half], jnp.maximum(top_max[:, half:], low_max)], axis=1)
        return col_max, meta

    def consume_diagonal(hh, produced):
        col_max, meta = produced
        slot = hh % SCORE_SLOTS
        m = m_ref[hh]
        m_new = jnp.maximum(m, col_max)
        p_top = jnp.exp2(s_ref[slot, 0:half, :] - m_new).astype(BF16)
        p_low = jnp.exp2(s_ref[slot, half:, half:] - m_new[:, half:]).astype(BF16)
        acc = acc_ref[hh] * jnp.exp2(m - m_new) + _dot(vt_ref[0, hh, qi, :, 0:half], p_top)
        if meta is not None:
            acc = acc + _dot(vmt_ref[hh], meta_weights(meta, m_new))
        acc_ref[hh, :, 0:half] = acc[:, 0:half]
        acc_ref[hh, :, half:] = acc[:, half:] + _dot(vt_ref[0, hh, qi, :, half:], p_low)
        m_ref[hh] = m_new

    def normalized(hh):
        return acc_ref[hh, 0:HEAD_DIM, :] / acc_ref[hh, HEAD_DIM:HEAD_DIM + 1, :]

    pending = lax.fori_loop(0, qi, body, tuple(produce(0, hh, diagonal_flag(0)) for hh in range(LOOKAHEAD)))
    pending = list(pending)
    for hh in heads:
        if hh + LOOKAHEAD < HEADS_PER_STEP:
            pending.append(produce_diagonal(hh + LOOKAHEAD))
        if hh < LOOKAHEAD:
            consume_diagonal(hh, (pending.pop(0), None))
        else:
            consume_diagonal(hh, pending.pop(0))
        if hh % 2:
            cols = slice((hh - 1) * HEAD_DIM, (hh + 1) * HEAD_DIM)
            pair = jnp.concatenate([normalized(hh - 1), normalized(hh)], axis=0).T
            o_ref[0, :, cols] = (pair * _silu(ga_ref[0, :, cols])).astype(BF16)


def _out_body(x_ref, ya_ref, yc_ref, wout_ref, lig_ref, lib_ref, log_ref, lob_ref, o_ref):
    tm = x_ref.shape[1]
    for r0 in range(0, tm, OUT_ROWS):
        rows = slice(r0, r0 + OUT_ROWS)
        y = _dot(jnp.concatenate([ya_ref[0, rows, :], yc_ref[0, rows, :]], axis=1), wout_ref[...])
        scaled_h = _layer_norm(x_ref[0, rows, :], lig_ref[...], lib_ref[...])
        o_ref[0, rows, :] = _layer_norm(scaled_h + y, log_ref[...], lob_ref[...])


def _placement():
    pm = np.zeros((LANES, 2 * D_ATTN), np.float32)
    for h in range(N_HEADS):
        base = (h // 2) * LANES + (HEAD_DIM if h % 2 == 0 else 0)
        for g in range(3):
            pm[g * N_HEADS + h, C_Q + base + g] = 1.0
            pm[3 * N_HEADS + h, C_Q + base + 3 + g] = 1.0
            pm[3 * N_HEADS + h, C_K + base + g] = 1.0
            pm[g * N_HEADS + h, C_K + base + 3 + g] = -1.0
    return jnp.asarray(pm, BF16)


def _full(shape):
    return pl.BlockSpec(shape, lambda *_: (0,) * len(shape))


def _call(body, name, vmem_mib, operands, out_shape, grid=(), in_specs=None, out_specs=None, scratch=()):
    assert vmem_mib * 1024 * 1024 <= V7X_VMEM_BYTES
    grid_args = {} if in_specs is None else dict(grid=grid, in_specs=in_specs, out_specs=out_specs)
    return pl.pallas_call(
        body, out_shape=out_shape, scratch_shapes=list(scratch), name=name,
        compiler_params=pltpu.CompilerParams(dimension_semantics=("arbitrary",) * len(grid),
                                             vmem_limit_bytes=vmem_mib * 1024 * 1024),
        **grid_args)(*operands)


def kernel(x, meta, ln_in_g, ln_in_b, w_in, b_f, conv_w, conv_b, ln_conv_g, ln_conv_b,
           w_pw, w_out, ln_out_g, ln_out_b):
    B, S, D = x.shape
    assert D == D_MODEL and w_in.shape[0] == 1 and meta.shape == (N_META, D_MODEL)
    assert S % TM_PROJ == 0 and S % TM_OUT == 0 and S % TQ == 0 and TQ == TK == TM_PROJ
    assert TM_OUT % OUT_ROWS == 0 and TM_PROJ % CONV_ROWS == 0 and N_HEADS % HEADS_PER_STEP == 0
    assert HEADS_PER_STEP % SCORE_SLOTS == 0 and SCORE_SLOTS > LOOKAHEAD

    w = w_in[0]
    scale = HEAD_DIM ** -0.5 * LOG2E
    wf = jnp.pad(jnp.tile(w[:, 3 * D_ATTN:3 * D_ATTN + N_HEADS], (1, 3)), ((0, 0), (0, LANES - 3 * N_HEADS)))
    w_all = jnp.concatenate([
        w[:, 0:D_ATTN] * scale, w[:, D_ATTN:2 * D_ATTN], wf, w[:, 3 * D_ATTN + N_HEADS:]], axis=1).astype(BF16)
    wvt = w[:, 2 * D_ATTN:3 * D_ATTN].T.astype(BF16)
    bf_row = jnp.pad(jnp.tile(b_f[0], 3), (0, LANES - 3 * N_HEADS)).reshape(1, LANES)
    pm = _placement()
    lig = ln_in_g.reshape(1, D)
    lib = ln_in_b.reshape(1, D)
    meta_pad = jnp.pad(meta, ((0, META_PAD - N_META), (0, 0)))

    k_meta, vt_meta, glu_meta = _call(
        _meta_body, "meta_proj", 48, (meta_pad, lig, lib, w_all, wvt, bf_row, pm),
        out_shape=(jax.ShapeDtypeStruct((N_HEADS, N_META, LANES), BF16),
                   jax.ShapeDtypeStruct((N_HEADS, V_ROWS, META_PAD), BF16),
                   jax.ShapeDtypeStruct((HALO, D_CONV), F32)))

    nt = S // TM_PROJ
    head_blk = pl.BlockSpec((1, N_HEADS, TM_PROJ, LANES), lambda b, i: (b, 0, i, 0))
    row_blk = pl.BlockSpec((1, TM_PROJ, D_ATTN), lambda b, i: (b, i, 0))
    qp, kp, vtp, ga, yc = _call(
        _proj_body, "in_proj", 56,
        (x, lig, lib, w_all, wvt, bf_row, pm, glu_meta,
         conv_w[0], conv_b[0].reshape(1, D_CONV), ln_conv_g[0].reshape(1, D_CONV), ln_conv_b[0].reshape(1, D_CONV),
         w_pw[0].astype(BF16), jnp.zeros((SUBLANES, LANES), jnp.int32)),
        grid=(B, nt),
        in_specs=[pl.BlockSpec((1, TM_PROJ, D), lambda b, i: (b, i, 0)),
                  _full((1, D)), _full((1, D)), _full((D, C_END)), _full((D_ATTN, D)), _full((1, LANES)),
                  _full((LANES, 2 * D_ATTN)), _full((HALO, D_CONV)),
                  _full((CONV_WIDTH, D_CONV)), _full((1, D_CONV)), _full((1, D_CONV)), _full((1, D_CONV)),
                  _full((D_CONV, D_CONV)), _full((SUBLANES, LANES))],
        out_specs=(head_blk, head_blk,
                   pl.BlockSpec((1, N_HEADS, 1, V_ROWS, TM_PROJ), lambda b, i: (b, 0, i, 0, 0)),
                   row_blk, row_blk),
        out_shape=(jax.ShapeDtypeStruct((B, N_HEADS, S, LANES), BF16),) * 2
                  + (jax.ShapeDtypeStruct((B, N_HEADS, nt, V_ROWS, TM_PROJ), BF16),
                     jax.ShapeDtypeStruct((B, S, D_ATTN), F32),
                     jax.ShapeDtypeStruct((B, S, D_CONV), BF16)),
        scratch=[pltpu.VMEM((1, LANES), F32), pltpu.VMEM((TM_PROJ, TM_PROJ), BF16),
                 pltpu.VMEM((HALO + TM_PROJ, D_CONV), F32), pltpu.VMEM((TM_PROJ, D_CONV), F32)])

    hp = HEADS_PER_STEP
    att = _call(
        _attn_body, "fox_attn", 56, (qp, kp, vtp, k_meta, vt_meta, ga),
        grid=(B, N_HEADS // hp, S // TQ),
        in_specs=[pl.BlockSpec((1, hp, TQ, LANES), lambda b, h, i: (b, h, i, 0)),
                  pl.BlockSpec((1, hp, S, LANES), lambda b, h, i: (b, h, 0, 0)),
                  pl.BlockSpec((1, hp, nt, V_ROWS, TK), lambda b, h, i: (b, h, 0, 0, 0)),
                  pl.BlockSpec((hp, N_META, LANES), lambda b, h, i: (h, 0, 0)),
                  pl.BlockSpec((hp, V_ROWS, META_PAD), lambda b, h, i: (h, 0, 0)),
                  pl.BlockSpec((1, TQ, hp * HEAD_DIM), lambda b, h, i: (b, i, h))],
        out_specs=pl.BlockSpec((1, TQ, hp * HEAD_DIM), lambda b, h, i: (b, i, h)),
        out_shape=jax.ShapeDtypeStruct((B, S, D_ATTN), BF16),
        scratch=[pltpu.VMEM((SCORE_SLOTS, TK, TQ), F32), pltpu.VMEM((2, TK, TQ), F32),
                 pltpu.VMEM((hp, 1, TQ), F32), pltpu.VMEM((hp, V_ROWS, TQ), F32)])

    tile_blk = pl.BlockSpec((1, TM_OUT, D_ATTN), lambda b, i: (b, i, 0))
    return _call(
        _out_body, "out_proj", 56,
        (x, att, yc, w_out[0].astype(BF16), ALPHA * lig, ALPHA * lib,
         ln_out_g[0].reshape(1, D), ln_out_b[0].reshape(1, D)),
        grid=(B, S // TM_OUT),
        in_specs=[pl.BlockSpec((1, TM_OUT, D), lambda b, i: (b, i, 0)), tile_blk, tile_blk,
                  _full((D, D)), _full((1, D)), _full((1, D)), _full((1, D)), _full((1, D))],
        out_specs=pl.BlockSpec((1, TM_OUT, D), lambda b, i: (b, i, 0)),
        out_shape=jax.ShapeDtypeStruct((B, S, D), F32))
```

```python
import numpy as np

import jax
import jax.numpy as jnp
from jax import lax
from jax.experimental import pallas as pl
from jax.experimental.pallas import tpu as pltpu

F32 = jnp.float32
BF16 = jnp.bfloat16

D_MODEL = 1024
N_META = 16
D_ATTN = 512
D_CONV = 512
N_HEADS = 8
HEAD_DIM = 64
CONV_WIDTH = 31
LN_EPS = 1e-5
ALPHA = 2.0 ** 0.25
V7X_VMEM_BYTES = 64 * 1024 * 1024
LANES = 128
SUBLANES = 8
BF16_ROWS = 16
META_PAD = 128
HALO = 32
MASK_BIAS = 1e30
LOG2E = 1.4426950408889634
V_ROWS = HEAD_DIM + BF16_ROWS

C_Q = 0
C_K = D_ATTN
C_F = 2 * D_ATTN
C_GA = C_F + LANES
C_U = C_GA + D_ATTN
C_UG = C_U + D_CONV
C_GC = C_UG + D_CONV
C_END = C_GC + D_CONV

TM_PROJ = 512
TQ = 512
TK = 512
HEADS_PER_STEP = 8
LOOKAHEAD = 2
SCORE_SLOTS = 4
TM_OUT = 1024
OUT_ROWS = 256
CONV_ROWS = 64


def _dot(a, b):
    return jnp.dot(a, b, preferred_element_type=F32)


def _dot_nt(a, b):
    return lax.dot_general(a, b, (((1,), (1,)), ((), ())), preferred_element_type=F32)


def _layer_norm(x, g, b):
    mu = jnp.mean(x, axis=-1, keepdims=True)
    xc = x - mu
    var = jnp.mean(xc * xc, axis=-1, keepdims=True)
    return xc * lax.rsqrt(var + LN_EPS) * g + b


def _sigmoid(x):
    return 0.5 * jnp.tanh(0.5 * x) + 0.5


def _silu(x):
    t = 0.5 * x
    return t * jnp.tanh(t) + t


def _log_sigmoid(x):
    return jnp.minimum(x, 0.0) - jnp.log1p(jnp.exp(-jnp.abs(x)))


def _split3(c):
    hi = c.astype(BF16).astype(F32)
    r = c - hi
    mid = r.astype(BF16).astype(F32)
    lo = (r - mid).astype(BF16).astype(F32)
    return hi, mid, lo


def _bias_operand(c, lane):
    hi, mid, lo = _split3(c * LOG2E)
    one = jnp.where(lane < 4 * N_HEADS, 1.0, 0.0)
    return jnp.where(lane < N_HEADS, hi,
                     jnp.where(lane < 2 * N_HEADS, mid,
                               jnp.where(lane < 3 * N_HEADS, lo, one))).astype(BF16)


def _forget_logs(hb, w_ref, bf_ref, lane):
    f = _dot(hb, w_ref[:, C_F:C_F + LANES])
    return jnp.where(lane < 3 * N_HEADS, _log_sigmoid(f + bf_ref[...]), 0.0)


def _exact_tri_dot(tri, x):
    hi, mid, lo = _split3(x)
    return _dot(tri, hi.astype(BF16)) + _dot(tri, mid.astype(BF16)) + _dot(tri, lo.astype(BF16))


def _head_tiles(proj, bias_cols, lane):
    low_half = lane < HEAD_DIM
    tiles = []
    for p in range(proj.shape[1] // LANES):
        pair = proj[:, p * LANES:(p + 1) * LANES]
        extra = bias_cols[:, p * LANES:(p + 1) * LANES]
        tiles.append(jnp.where(low_half, pair, extra).astype(BF16))
        tiles.append(pltpu.roll(jnp.where(low_half, extra, pair), HEAD_DIM, axis=1).astype(BF16))
    return tiles


def _ones_rows(cols, valid_cols):
    r = lax.broadcasted_iota(jnp.int32, (BF16_ROWS, cols), 0)
    c = lax.broadcasted_iota(jnp.int32, (BF16_ROWS, cols), 1)
    return jnp.where((r == 0) & (c < valid_cols), 1.0, 0.0).astype(BF16)


def _meta_body(x_ref, lng_ref, lnb_ref, w_ref, wvt_ref, bf_ref, pm_ref,
               k_ref, vt_ref, glu_ref):
    rows = x_ref.shape[0]
    hb = _layer_norm(x_ref[...], lng_ref[...], lnb_ref[...]).astype(BF16)
    lane = lax.broadcasted_iota(jnp.int32, (rows, LANES), 1)
    lf = _forget_logs(hb, w_ref, bf_ref, lane)
    r = lax.broadcasted_iota(jnp.int32, (rows, rows), 0)
    c = lax.broadcasted_iota(jnp.int32, (rows, rows), 1)
    upper = jnp.where((c > r) & (c < N_META), 1.0, 0.0).astype(BF16)
    cmat = _bias_operand(-_exact_tri_dot(upper, lf), lane)
    k_tiles = _head_tiles(_dot(hb, w_ref[:, C_K:C_F]), _dot(cmat, pm_ref[:, C_K:C_F]), lane)
    vt = _dot_nt(wvt_ref[...], hb)
    real = lax.broadcasted_iota(jnp.int32, (D_ATTN, rows), 1) < N_META
    vt = jnp.where(real, vt, 0.0).astype(BF16)
    ones = _ones_rows(rows, N_META)
    for h in range(N_HEADS):
        k_ref[h] = k_tiles[h][0:N_META, :]
        vt_ref[h, 0:HEAD_DIM, :] = vt[h * HEAD_DIM:(h + 1) * HEAD_DIM, :]
        vt_ref[h, HEAD_DIM:V_ROWS, :] = ones
    u = _dot(hb, w_ref[:, C_U:C_UG])
    ug = _dot(hb, w_ref[:, C_UG:C_GC])
    glu = u * _sigmoid(ug)
    glu_ref[0:HALO - N_META, :] = jnp.zeros((HALO - N_META, D_CONV), F32)
    glu_ref[HALO - N_META:HALO, :] = glu[0:N_META, :]


def _shift_rows(part, r):
    if r == 0:
        return part[0:CONV_ROWS, :]
    from_next = lax.broadcasted_iota(jnp.int32, (SUBLANES, part.shape[1]), 0) < r
    slabs = [part[k:k + SUBLANES, :] for k in range(0, part.shape[0], SUBLANES)]
    return jnp.concatenate([pltpu.roll(jnp.where(from_next, nxt, cur), SUBLANES - r, axis=0)
                            for cur, nxt in zip(slabs[:-1], slabs[1:])], axis=0)


def _conv_rows(buf_ref, cw_ref, cb_ref, out_ref, base, zero):
    shift = HALO - (CONV_WIDTH - 1)
    for lo in range(0, D_CONV, LANES):
        cols = slice(lo, lo + LANES)
        out = jnp.broadcast_to(cb_ref[:, cols] + zero, (CONV_ROWS, LANES))
        for r in range(SUBLANES):
            window = CONV_ROWS + (SUBLANES if r else 0)
            part = None
            for j in range(CONV_WIDTH):
                if (shift + j) % SUBLANES != r:
                    continue
                start = base + shift + j - r
                term = cw_ref[j:j + 1, cols] * buf_ref[start:start + window, cols]
                part = term if part is None else part + term
            out = out + _shift_rows(part, r)
        out_ref[base:base + CONV_ROWS, cols] = out


def _proj_body(x_ref, lng_ref, lnb_ref, w_ref, wvt_ref, bf_ref, pm_ref, meta_glu_ref,
               cw_ref, cb_ref, lcg_ref, lcb_ref, wpw_ref, zero_ref,
               q_ref, k_ref, vt_ref, ga_ref, yc_ref,
               carry_ref, tri_ref, buf_ref, conv_ref):
    tm = x_ref.shape[1]
    first = pl.program_id(1) == 0

    @pl.when(first)
    def _start_sequence():
        carry_ref[...] = jnp.zeros_like(carry_ref)
        r = lax.broadcasted_iota(jnp.int32, (tm, tm), 0)
        c = lax.broadcasted_iota(jnp.int32, (tm, tm), 1)
        tri_ref[...] = jnp.where(r >= c, 1.0, 0.0).astype(BF16)
        buf_ref[0:HALO, :] = meta_glu_ref[...]

    @pl.when(jnp.logical_not(first))
    def _carry_conv_context():
        buf_ref[0:HALO, :] = buf_ref[tm:tm + HALO, :]

    def zero_after(tile):
        bits = lax.bitcast_convert_type(tile, jnp.int32) & zero_ref[...]
        return lax.bitcast_convert_type(bits, F32)[0:1, :]

    hb = _layer_norm(x_ref[0], lng_ref[...], lnb_ref[...]).astype(BF16)
    for lo in range(0, D_CONV, D_CONV // 2):
        u = _dot(hb, w_ref[:, C_U + lo:C_U + lo + D_CONV // 2])
        ug = _dot(hb, w_ref[:, C_UG + lo:C_UG + lo + D_CONV // 2])
        buf_ref[HALO:HALO + tm, lo:lo + D_CONV // 2] = u * _sigmoid(ug)

    lane = lax.broadcasted_iota(jnp.int32, (tm, LANES), 1)
    lf = _forget_logs(hb, w_ref, bf_ref, lane)
    c = _exact_tri_dot(tri_ref[...], lf) + carry_ref[...]
    carry_ref[...] = c[tm - 1:tm, :]
    cmat = _bias_operand(c, lane)
    ones = _ones_rows(tm, tm)

    def qk_heads(out_ref, col0, first_pair, n_pairs):
        cols = slice(col0 + first_pair * LANES, col0 + (first_pair + n_pairs) * LANES)
        ext = _dot(cmat, pm_ref[:, cols])
        tiles = _head_tiles(_dot(hb, w_ref[:, cols]), ext, lane)
        for h, tile in enumerate(tiles):
            out_ref[0, 2 * first_pair + h] = tile
        return zero_after(ext[tm - SUBLANES:, (n_pairs - 1) * LANES:])

    def v_heads(first_head, n_heads):
        rows = slice(first_head * HEAD_DIM, (first_head + n_heads) * HEAD_DIM)
        vt = _dot_nt(wvt_ref[rows, :], hb)
        for h in range(n_heads):
            vt_ref[0, first_head + h, 0, 0:HEAD_DIM, :] = vt[h * HEAD_DIM:(h + 1) * HEAD_DIM, :].astype(BF16)
            vt_ref[0, first_head + h, 0, HEAD_DIM:V_ROWS, :] = ones
        return zero_after(vt[n_heads * HEAD_DIM - SUBLANES:, tm - LANES:])

    def attn_gate():
        ga = _dot(hb, w_ref[:, C_GA:C_U])
        ga_ref[0] = ga
        return zero_after(ga[tm - SUBLANES:, D_ATTN - LANES:])

    half = N_HEADS // 2
    quarter = N_HEADS // 4
    gates = []

    def conv_gate():
        gates.append(_silu(_dot(hb, w_ref[:, C_GC:C_END])))
        return zero_after(gates[0][tm - SUBLANES:, D_CONV - LANES:])

    mxu_groups = [conv_gate, attn_gate, lambda: v_heads(0, half), lambda: v_heads(half, half),
                  lambda: qk_heads(q_ref, C_Q, 0, quarter), lambda: qk_heads(q_ref, C_Q, quarter, quarter),
                  lambda: qk_heads(k_ref, C_K, 0, quarter), lambda: qk_heads(k_ref, C_K, quarter, quarter)]
    chunks = list(range(0, tm, CONV_ROWS))
    per_chunk = -(-len(mxu_groups) // len(chunks))
    after = jnp.zeros((1, LANES), F32)
    for n, base in enumerate(chunks):
        _conv_rows(buf_ref, cw_ref, cb_ref, conv_ref, base, after)
        for group in mxu_groups[n * per_chunk:(n + 1) * per_chunk]:
            after = group()

    z = _silu(_layer_norm(conv_ref[...], lcg_ref[...], lcb_ref[...]))
    yc_ref[0] = (_dot(z.astype(BF16), wpw_ref[...]) * gates[0]).astype(BF16)


def _attn_body(q_ref, k_ref, vt_ref, km_ref, vmt_ref, ga_ref, o_ref, s_ref, pen_ref, m_ref, acc_ref):
    qi = pl.program_id(2)
    heads = range(HEADS_PER_STEP)

    @pl.when((pl.program_id(0) == 0) & (qi == 0))
    def _causal_penalty():
        key = lax.broadcasted_iota(jnp.int32, (TK, TQ), 0)
        qry = lax.broadcasted_iota(jnp.int32, (TK, TQ), 1)
        pen_ref[0] = jnp.zeros((TK, TQ), F32)
        pen_ref[1] = jnp.where(key <= qry, 0.0, -MASK_BIAS)

    pad = jnp.zeros((META_PAD - N_META, TQ), BF16)

    def meta_weights(s, m):
        return jnp.concatenate([jnp.exp2(s - m).astype(BF16), pad], axis=0)

    early = range(LOOKAHEAD)
    s_meta = [_dot_nt(km_ref[hh], q_ref[0, hh]) for hh in early]
    m_meta = [jnp.max(s, axis=0, keepdims=True) for s in s_meta]
    for hh in heads:
        if hh < LOOKAHEAD:
            m_ref[hh] = m_meta[hh]
            acc_ref[hh] = _dot(vmt_ref[hh], meta_weights(s_meta[hh], m_meta[hh]))
        else:
            m_ref[hh] = jnp.full((1, TQ), -MASK_BIAS, F32)
            acc_ref[hh] = jnp.zeros((V_ROWS, TQ), F32)

    def produce(j, hh, penalty):
        off = pl.multiple_of(j * TK, TK)
        s = _dot_nt(k_ref[0, hh, pl.ds(off, TK), :], q_ref[0, hh])
        if penalty is not None:
            s = s + pen_ref[penalty]
        s_ref[hh % SCORE_SLOTS] = s
        return jnp.max(s, axis=0, keepdims=True)

    def consume(j, hh, col_max):
        m = m_ref[hh]
        m_new = jnp.maximum(m, col_max)
        p = jnp.exp2(s_ref[hh % SCORE_SLOTS] - m_new).astype(BF16)
        acc_ref[hh] = acc_ref[hh] * jnp.exp2(m - m_new) + _dot(vt_ref[0, hh, j], p)
        m_ref[hh] = m_new

    def diagonal_flag(j):
        return jnp.where(j < qi, 0, 1)

    def body(j, pending):
        pending = list(pending)
        for hh in heads:
            ahead = hh + LOOKAHEAD
            if ahead < HEADS_PER_STEP:
                pending.append(produce(j, ahead, None))
            else:
                pending.append(produce(j + 1, ahead - HEADS_PER_STEP, diagonal_flag(j + 1)))
            consume(j, hh, pending.pop(0))
        return tuple(pending)

    half = TK // 2

    def produce_diagonal(hh):
        slot = hh % SCORE_SLOTS
        off = pl.multiple_of(qi * TK, TK)
        keys = jnp.concatenate([k_ref[0, hh, pl.ds(off, half), :], km_ref[hh]], axis=0)
        scores = _dot_nt(keys, q_ref[0, hh])
        top = scores[0:half, :] + pen_ref[1, 0:half, :]
        meta = scores[half:, :]
        low = (_dot_nt(k_ref[0, hh, pl.ds(off + half, half), :], q_ref[0, hh, half:, :])
               + pen_ref[1, half:, half:])
        s_ref[slot, 0:half, :] = top
        s_ref[slot, half:, half:] = low
        top_max = jnp.maximum(jnp.max(top, axis=0, keepdims=True), jnp.max(meta, axis=0, keepdims=True))
        low_max = jnp.max(low, axis=0, keepdims=True)
        col_max = jnp.concatenate([top_max[:, 0:half], jnp.maximum(top_max[:, half:], low_max)], axis=1)
        return col_max, meta

    def consume_diagonal(hh, produced):
        col_max, meta = produced
        slot = hh % SCORE_SLOTS
        m = m_ref[hh]
        m_new = jnp.maximum(m, col_max)
        p_top = jnp.exp2(s_ref[slot, 0:half, :] - m_new).astype(BF16)
        p_low = jnp.exp2(s_ref[slot, half:, half:] - m_new[:, half:]).astype(BF16)
        acc = (acc_ref[hh] * jnp.exp2(m - m_new) + _dot(vt_ref[0, hh, qi, :, 0:half], p_top)
               + _dot(vmt_ref[hh], meta_weights(meta, m_new)))
        acc_ref[hh, :, 0:half] = acc[:, 0:half]
        acc_ref[hh, :, half:] = acc[:, half:] + _dot(vt_ref[0, hh, qi, :, half:], p_low)
        m_ref[hh] = m_new

    def normalized(hh):
        return acc_ref[hh, 0:HEAD_DIM, :] / acc_ref[hh, HEAD_DIM:HEAD_DIM + 1, :]

    pending = lax.fori_loop(0, qi, body, tuple(produce(0, hh, diagonal_flag(0)) for hh in range(LOOKAHEAD)))
    pending = list(pending)
    for hh in heads:
        if hh + LOOKAHEAD < HEADS_PER_STEP:
            pending.append(produce_diagonal(hh + LOOKAHEAD))
        if hh < LOOKAHEAD:
            consume(qi, hh, pending.pop(0))
        else:
            consume_diagonal(hh, pending.pop(0))
        if hh % 2:
            cols = slice((hh - 1) * HEAD_DIM, (hh + 1) * HEAD_DIM)
            pair = jnp.concatenate([normalized(hh - 1), normalized(hh)], axis=0).T
            o_ref[0, :, cols] = (pair * _silu(ga_ref[0, :, cols])).astype(BF16)


def _out_body(x_ref, ya_ref, yc_ref, wout_ref, lig_ref, lib_ref, log_ref, lob_ref, o_ref):
    tm = x_ref.shape[1]
    for r0 in range(0, tm, OUT_ROWS):
        rows = slice(r0, r0 + OUT_ROWS)
        y = _dot(jnp.concatenate([ya_ref[0, rows, :], yc_ref[0, rows, :]], axis=1), wout_ref[...])
        scaled_h = _layer_norm(x_ref[0, rows, :], lig_ref[...], lib_ref[...])
        o_ref[0, rows, :] = _layer_norm(scaled_h + y, log_ref[...], lob_ref[...])


def _placement():
    pm = np.zeros((LANES, 2 * D_ATTN), np.float32)
    for h in range(N_HEADS):
        base = (h // 2) * LANES + (HEAD_DIM if h % 2 == 0 else 0)
        for g in range(3):
            pm[g * N_HEADS + h, C_Q + base + g] = 1.0
            pm[3 * N_HEADS + h, C_Q + base + 3 + g] = 1.0
            pm[3 * N_HEADS + h, C_K + base + g] = 1.0
            pm[g * N_HEADS + h, C_K + base + 3 + g] = -1.0
    return jnp.asarray(pm, BF16)


def _full(shape):
    return pl.BlockSpec(shape, lambda *_: (0,) * len(shape))


def _call(body, name, vmem_mib, operands, out_shape, grid=(), in_specs=None, out_specs=None, scratch=(),
          fuse_producers=()):
    assert vmem_mib * 1024 * 1024 <= V7X_VMEM_BYTES
    grid_args = {} if in_specs is None else dict(grid=grid, in_specs=in_specs, out_specs=out_specs)
    fusion = [n in fuse_producers for n in range(len(operands))] if fuse_producers else None
    return pl.pallas_call(
        body, out_shape=out_shape, scratch_shapes=list(scratch), name=name,
        compiler_params=pltpu.CompilerParams(dimension_semantics=("arbitrary",) * len(grid),
                                             vmem_limit_bytes=vmem_mib * 1024 * 1024,
                                             allow_input_fusion=fusion),
        **grid_args)(*operands)


def kernel(x, meta, ln_in_g, ln_in_b, w_in, b_f, conv_w, conv_b, ln_conv_g, ln_conv_b,
           w_pw, w_out, ln_out_g, ln_out_b):
    B, S, D = x.shape
    assert D == D_MODEL and w_in.shape[0] == 1 and meta.shape == (N_META, D_MODEL)
    assert S % TM_PROJ == 0 and S % TM_OUT == 0 and S % TQ == 0 and TQ == TK == TM_PROJ
    assert TM_OUT % OUT_ROWS == 0 and TM_PROJ % CONV_ROWS == 0 and N_HEADS % HEADS_PER_STEP == 0
    assert HEADS_PER_STEP % SCORE_SLOTS == 0 and SCORE_SLOTS > LOOKAHEAD

    w = w_in[0]
    scale = HEAD_DIM ** -0.5 * LOG2E
    wf = jnp.pad(jnp.tile(w[:, 3 * D_ATTN:3 * D_ATTN + N_HEADS], (1, 3)), ((0, 0), (0, LANES - 3 * N_HEADS)))
    w_all = jnp.concatenate([
        w[:, 0:D_ATTN] * scale, w[:, D_ATTN:2 * D_ATTN], wf, w[:, 3 * D_ATTN + N_HEADS:]], axis=1).astype(BF16)
    wvt = w[:, 2 * D_ATTN:3 * D_ATTN].T.astype(BF16)
    bf_row = jnp.pad(jnp.tile(b_f[0], 3), (0, LANES - 3 * N_HEADS)).reshape(1, LANES)
    pm = _placement()
    lig = ln_in_g.reshape(1, D)
    lib = ln_in_b.reshape(1, D)
    meta_pad = jnp.pad(meta, ((0, META_PAD - N_META), (0, 0)))

    k_meta, vt_meta, glu_meta = _call(
        _meta_body, "meta_proj", 48, (meta_pad, lig, lib, w_all, wvt, bf_row, pm),
        out_shape=(jax.ShapeDtypeStruct((N_HEADS, N_META, LANES), BF16),
                   jax.ShapeDtypeStruct((N_HEADS, V_ROWS, META_PAD), BF16),
                   jax.ShapeDtypeStruct((HALO, D_CONV), F32)))

    nt = S // TM_PROJ
    head_blk = pl.BlockSpec((1, N_HEADS, TM_PROJ, LANES), lambda b, i: (b, 0, i, 0))
    row_blk = pl.BlockSpec((1, TM_PROJ, D_ATTN), lambda b, i: (b, i, 0))
    qp, kp, vtp, ga, yc = _call(
        _proj_body, "in_proj", 56,
        (x, lig, lib, w_all, wvt, bf_row, pm, glu_meta,
         conv_w[0], conv_b[0].reshape(1, D_CONV), ln_conv_g[0].reshape(1, D_CONV), ln_conv_b[0].reshape(1, D_CONV),
         w_pw[0].astype(BF16), jnp.zeros((SUBLANES, LANES), jnp.int32)),
        grid=(B, nt),
        in_specs=[pl.BlockSpec((1, TM_PROJ, D), lambda b, i: (b, i, 0)),
                  _full((1, D)), _full((1, D)), _full((D, C_END)), _full((D_ATTN, D)), _full((1, LANES)),
                  _full((LANES, 2 * D_ATTN)), _full((HALO, D_CONV)),
                  _full((CONV_WIDTH, D_CONV)), _full((1, D_CONV)), _full((1, D_CONV)), _full((1, D_CONV)),
                  _full((D_CONV, D_CONV)), _full((SUBLANES, LANES))],
        out_specs=(head_blk, head_blk,
                   pl.BlockSpec((1, N_HEADS, 1, V_ROWS, TM_PROJ), lambda b, i: (b, 0, i, 0, 0)),
                   row_blk, row_blk),
        out_shape=(jax.ShapeDtypeStruct((B, N_HEADS, S, LANES), BF16),) * 2
                  + (jax.ShapeDtypeStruct((B, N_HEADS, nt, V_ROWS, TM_PROJ), BF16),
                     jax.ShapeDtypeStruct((B, S, D_ATTN), F32),
                     jax.ShapeDtypeStruct((B, S, D_CONV), BF16)),
        scratch=[pltpu.VMEM((1, LANES), F32), pltpu.VMEM((TM_PROJ, TM_PROJ), BF16),
                 pltpu.VMEM((HALO + TM_PROJ, D_CONV), F32), pltpu.VMEM((TM_PROJ, D_CONV), F32)],
        fuse_producers=(3, 4, 12))

    hp = HEADS_PER_STEP
    att = _call(
        _attn_body, "fox_attn", 56, (qp, kp, vtp, k_meta, vt_meta, ga),
        grid=(B, N_HEADS // hp, S // TQ),
        in_specs=[pl.BlockSpec((1, hp, TQ, LANES), lambda b, h, i: (b, h, i, 0)),
                  pl.BlockSpec((1, hp, S, LANES), lambda b, h, i: (b, h, 0, 0)),
                  pl.BlockSpec((1, hp, nt, V_ROWS, TK), lambda b, h, i: (b, h, 0, 0, 0)),
                  pl.BlockSpec((hp, N_META, LANES), lambda b, h, i: (h, 0, 0)),
                  pl.BlockSpec((hp, V_ROWS, META_PAD), lambda b, h, i: (h, 0, 0)),
                  pl.BlockSpec((1, TQ, hp * HEAD_DIM), lambda b, h, i: (b, i, h))],
        out_specs=pl.BlockSpec((1, TQ, hp * HEAD_DIM), lambda b, h, i: (b, i, h)),
        out_shape=jax.ShapeDtypeStruct((B, S, D_ATTN), BF16),
        scratch=[pltpu.VMEM((SCORE_SLOTS, TK, TQ), F32), pltpu.VMEM((2, TK, TQ), F32),
                 pltpu.VMEM((hp, 1, TQ), F32), pltpu.VMEM((hp, V_ROWS, TQ), F32)])

    tile_blk = pl.BlockSpec((1, TM_OUT, D_ATTN), lambda b, i: (b, i, 0))
    return _call(
        _out_body, "out_proj", 56,
        (x, att, yc, w_out[0].astype(BF16), ALPHA * lig, ALPHA * lib,
         ln_out_g[0].reshape(1, D), ln_out_b[0].reshape(1, D)),
        grid=(B, S // TM_OUT),
        in_specs=[pl.BlockSpec((1, TM_OUT, D), lambda b, i: (b, i, 0)), tile_blk, tile_blk,
                  _full((D, D)), _full((1, D)), _full((1, D)), _full((1, D)), _full((1, D))],
        out_specs=pl.BlockSpec((1, TM_OUT, D), lambda b, i: (b, i, 0)),
        out_shape=jax.ShapeDtypeStruct((B, S, D), F32), fuse_producers=(3,))
```
